```python
import jax, jax.numpy as jnp
from jax import lax
import numpy as np

D_MODEL = 4096
BATCH = 1
SEQ = 16384
DEPTH = 1

A_WIDTH = D_MODEL // 2
A_GROUPS = 4
A_GROUP_DIM = A_WIDTH // A_GROUPS
A_CHUNK = 128
B_HEADS = 4
B_KEY_WIDTH = D_MODEL // 4
B_VAL_WIDTH = D_MODEL // 2
B_DK = B_KEY_WIDTH // B_HEADS
B_DV = B_VAL_WIDTH // B_HEADS
B_GATE_RANK = 16
B_GATE_TAU = 16.0
B_CHUNK = 64
N_EXPERTS = 32
TOP_K = 4
D_EXPERT = 1536
SWIGLU_LIMIT = 7.0
SWIGLU_ALPHA = 1.702
MOE_BLOCK = 256
LN_EPS = 1e-5
DEEPNORM_ALPHA = (2 * DEPTH) ** 0.25
DEEPNORM_BETA = (8 * DEPTH) ** -0.25
SPLITS = (A_WIDTH, A_WIDTH, B_KEY_WIDTH, B_KEY_WIDTH, B_VAL_WIDTH, B_VAL_WIDTH, B_GATE_RANK, D_MODEL, D_MODEL)
D_IN = sum(SPLITS)

kernel_name = 'hybrid_gmlp_gla_moe_deepnorm'


def _layer_norm(x, g, b):
    xf = x.astype(jnp.float32)
    mu = jnp.mean(xf, axis=-1, keepdims=True)
    var = jnp.mean(jnp.square(xf - mu), axis=-1, keepdims=True)
    y = (xf - mu) * lax.rsqrt(var + LN_EPS)
    return (y * g.astype(jnp.float32) + b.astype(jnp.float32)).astype(x.dtype)


def _spatial_gating(u, v, ln_g, ln_b, ws, bs):
    bsz, seq, _ = v.shape
    n_chunks = seq // A_CHUNK
    v = _layer_norm(v.reshape(bsz, seq, A_GROUPS, A_GROUP_DIM), ln_g, ln_b)
    v = v.reshape(bsz, n_chunks, A_CHUNK, A_GROUPS, A_GROUP_DIM)
    causal = jnp.tril(jnp.ones((A_CHUNK, A_CHUNK), dtype=bool))
    ws_c = jnp.where(causal, ws, 0).astype(v.dtype)
    mixed = jnp.einsum('gts,bcsgd->bctgd', ws_c, v) + bs.T[:, :, None].astype(v.dtype)
    return u * mixed.reshape(bsz, seq, A_WIDTH)


def _gla(q, k, v, lr, w_lr, b_lr):
    f32 = jnp.float32
    bsz, seq, _ = q.shape
    n = seq // B_CHUNK
    gate_logits = jnp.einsum('btr,rk->btk', lr.astype(f32), w_lr.astype(f32)) + b_lr.astype(f32)
    log_a = jax.nn.log_sigmoid(gate_logits) / B_GATE_TAU
    shp_k = (bsz, n, B_CHUNK, B_HEADS, B_DK)
    shp_v = (bsz, n, B_CHUNK, B_HEADS, B_DV)
    log_a = log_a.reshape(shp_k)
    q = q.astype(f32).reshape(shp_k) * (B_DK ** -0.5)
    k = k.astype(f32).reshape(shp_k)
    v = v.astype(f32).reshape(shp_v)
    cum = jnp.cumsum(log_a, axis=2)
    cum_last = cum[:, :, -1:]
    q_t = q * jnp.exp(cum)
    k_t = k * jnp.exp(-cum)
    k_last = k * jnp.exp(cum_last - cum)
    causal = jnp.tril(jnp.ones((B_CHUNK, B_CHUNK), dtype=bool))
    scores = jnp.where(causal, jnp.einsum('bnihd,bnjhd->bnhij', q_t, k_t), 0.0)
    o_intra = jnp.einsum('bnhij,bnjhv->bnihv', scores, v)
    decay = jnp.exp(cum_last[:, :, 0])

    def step(state, inp):
        qn, kn, vn, dn = inp
        o = jnp.einsum('bihd,bhdv->bihv', qn, state)
        state = dn[..., None] * state + jnp.einsum('bjhd,bjhv->bhdv', kn, vn)
        return state, o

    xs = (jnp.moveaxis(q_t, 1, 0), jnp.moveaxis(k_last, 1, 0), jnp.moveaxis(v, 1, 0), jnp.moveaxis(decay, 1, 0))
    s0 = jnp.zeros((bsz, B_HEADS, B_DK, B_DV), f32)
    _, o_inter = lax.scan(step, s0, xs)
    o = o_intra + jnp.moveaxis(o_inter, 0, 1)
    return o.reshape(bsz, seq, B_HEADS, B_DV)


def _token_mixer(x, w_in, b_in, a_ws, a_bs, a_ln_g, a_ln_b, gla_w_lr, gla_b_lr, gla_gn_g, w_br_a, w_br_b, w_o):
    f32 = jnp.float32
    bsz, seq, _ = x.shape
    split_points = tuple(int(s) for s in np.cumsum(SPLITS)[:-1])
    z = jnp.einsum('btd,de->bte', x, w_in) + b_in
    a_u, a_v, q, k, v, r, lr, g_a, g_b = jnp.split(z, split_points, axis=-1)
    y_a = _spatial_gating(jax.nn.gelu(a_u, approximate=False), jax.nn.gelu(a_v, approximate=False),
                          a_ln_g, a_ln_b, a_ws, a_bs)
    o = _gla(q, k, v, lr, gla_w_lr, gla_b_lr)
    o = o * lax.rsqrt(jnp.mean(jnp.square(o), axis=-1, keepdims=True) + LN_EPS) * gla_gn_g.astype(f32)
    y_b = (o * jax.nn.silu(r.astype(f32)).reshape(bsz, seq, B_HEADS, B_DV)).reshape(bsz, seq, B_VAL_WIDTH).astype(x.dtype)
    merged = jax.nn.sigmoid(g_a) * (y_a @ w_br_a) + jax.nn.sigmoid(g_b) * (y_b @ w_br_b)
    return merged @ w_o


def _moe(x, w_router, b_router, w_gu, b_gu, w_down, b_down):
    f32 = jnp.float32
    bsz, seq, d = x.shape
    n_tok = bsz * seq
    xt = x.reshape(n_tok, d)
    n_assign = n_tok * TOP_K
    n_blocks = -(-(n_assign + N_EXPERTS * (MOE_BLOCK - 1)) // MOE_BLOCK)
    n_pad = n_blocks * MOE_BLOCK
    logits = (xt @ w_router + b_router).astype(f32)
    top_logits, top_idx = lax.top_k(logits, TOP_K)
    top_w = jax.nn.softmax(top_logits, axis=-1)
    flat_e = top_idx.reshape(-1)
    flat_tok = jnp.arange(n_assign, dtype=jnp.int32) // TOP_K
    flat_w = top_w.reshape(-1)
    order = jnp.argsort(flat_e)
    sorted_e = flat_e[order]
    counts = jnp.bincount(flat_e, length=N_EXPERTS)
    group_start = jnp.cumsum(counts) - counts
    padded = (counts + MOE_BLOCK - 1) // MOE_BLOCK * MOE_BLOCK
    padded_end = jnp.cumsum(padded)
    padded_start = padded_end - padded
    rank = jnp.arange(n_assign, dtype=jnp.int32) - group_start[sorted_e]
    dest = padded_start[sorted_e] + rank
    tok_buf = jnp.zeros((n_pad,), jnp.int32).at[dest].set(flat_tok[order])
    w_buf = jnp.zeros((n_pad,), f32).at[dest].set(flat_w[order])
    block_e = jnp.minimum(jnp.searchsorted(padded_end, jnp.arange(n_blocks) * MOE_BLOCK, side='right'), N_EXPERTS - 1)

    def block(y, inp):
        tok, w, e = inp
        xb = xt[tok]
        gu = xb @ w_gu[e] + b_gu[e]
        gate, lin = jnp.split(gu, 2, axis=-1)
        gate = jnp.minimum(gate, SWIGLU_LIMIT)
        lin = jnp.clip(lin, -SWIGLU_LIMIT, SWIGLU_LIMIT)
        h = gate * jax.nn.sigmoid(SWIGLU_ALPHA * gate) * (lin + 1)
        out = (h @ w_down[e] + b_down[e]).astype(f32)
        return y.at[tok].add(out * w[:, None]), None

    y0 = jnp.zeros((n_tok, d), f32)
    y, _ = lax.scan(block, y0, (tok_buf.reshape(n_blocks, MOE_BLOCK), w_buf.reshape(n_blocks, MOE_BLOCK), block_e))
    return y.reshape(bsz, seq, d).astype(x.dtype)


def setup_inputs(seed: int = 0) -> dict:
    key = jax.random.key(seed)
    ks = jax.random.split(key, 23)

    def nrm(k, shape, scale):
        return jax.random.normal(k, shape, jnp.float32) * scale

    L = DEPTH
    return {
        'x': nrm(ks[0], (BATCH, SEQ, D_MODEL), 1.0),
        'w_in': nrm(ks[1], (L, D_MODEL, D_IN), D_MODEL ** -0.5),
        'b_in': nrm(ks[2], (L, D_IN), 0.02),
        'a_ws': nrm(ks[3], (L, A_GROUPS, A_CHUNK, A_CHUNK), A_CHUNK ** -0.5),
        'a_bs': 1.0 + nrm(ks[4], (L, A_GROUPS, A_CHUNK), 0.01),
        'a_ln_g': 1.0 + nrm(ks[5], (L, A_GROUPS, A_GROUP_DIM), 0.01),
        'a_ln_b': nrm(ks[6], (L, A_GROUPS, A_GROUP_DIM), 0.01),
        'gla_w_lr': nrm(ks[7], (L, B_GATE_RANK, B_KEY_WIDTH), B_GATE_RANK ** -0.5),
        'gla_b_lr': nrm(ks[8], (L, B_KEY_WIDTH), 0.02),
        'gla_gn_g': 1.0 + nrm(ks[9], (L, B_HEADS, B_DV), 0.01),
        'w_br_a': nrm(ks[10], (L, A_WIDTH, D_MODEL), DEEPNORM_BETA * A_WIDTH ** -0.5),
        'w_br_b': nrm(ks[11], (L, B_VAL_WIDTH, D_MODEL), DEEPNORM_BETA * B_VAL_WIDTH ** -0.5),
        'w_o': nrm(ks[12], (L, D_MODEL, D_MODEL), DEEPNORM_BETA * D_MODEL ** -0.5),
        'ln1_g': 1.0 + nrm(ks[13], (L, D_MODEL), 0.01),
        'ln1_b': nrm(ks[14], (L, D_MODEL), 0.01),
        'w_router': nrm(ks[15], (L, D_MODEL, N_EXPERTS), D_MODEL ** -0.5),
        'b_router': nrm(ks[16], (L, N_EXPERTS), 0.01),
        'w_gu': nrm(ks[17], (L, N_EXPERTS, D_MODEL, 2 * D_EXPERT), D_MODEL ** -0.5),
        'b_gu': nrm(ks[18], (L, N_EXPERTS, 2 * D_EXPERT), 0.01),
        'w_down': nrm(ks[19], (L, N_EXPERTS, D_EXPERT, D_MODEL), DEEPNORM_BETA * D_EXPERT ** -0.5),
        'b_down': nrm(ks[20], (L, N_EXPERTS, D_MODEL), 0.01),
        'ln2_g': 1.0 + nrm(ks[21], (L, D_MODEL), 0.01),
        'ln2_b': nrm(ks[22], (L, D_MODEL), 0.01),
    }


def reference(x, w_in, b_in, a_ws, a_bs, a_ln_g, a_ln_b, gla_w_lr, gla_b_lr, gla_gn_g, w_br_a, w_br_b, w_o,
              ln1_g, ln1_b, w_router, b_router, w_gu, b_gu, w_down, b_down, ln2_g, ln2_b):
    for l in range(DEPTH):
        mix = _token_mixer(x, w_in[l], b_in[l], a_ws[l], a_bs[l], a_ln_g[l], a_ln_b[l], gla_w_lr[l], gla_b_lr[l],
                           gla_gn_g[l], w_br_a[l], w_br_b[l], w_o[l])
        x = _layer_norm(DEEPNORM_ALPHA * x + mix, ln1_g[l], ln1_b[l])
        ffn = _moe(x, w_router[l], b_router[l], w_gu[l], b_gu[l], w_down[l], b_down[l])
        x = _layer_norm(DEEPNORM_ALPHA * x + ffn, ln2_g[l], ln2_b[l])
    return x
```

```python
import functools

import jax
import jax.numpy as jnp
from jax import lax
from jax.experimental import pallas as pl
from jax.experimental.pallas import tpu as pltpu

F32 = jnp.float32
BF16 = jnp.bfloat16
U32 = jnp.uint32
I32 = jnp.int32

LN_EPS = 1e-5
GLA_CHUNK = 64
GLA_GATE_TAU = 16.0
TOP_K = 4
SWIGLU_LIMIT = 7.0
SWIGLU_ALPHA = 1.702
LANES = 128
VMEM_LIMIT = 56 * 1024 * 1024
HI_MASK = 0xFFFF0000


def _pick(n, pref):
    t = min(pref, n)
    while n % t:
        t //= 2
    return t


def _dot(a, b):
    return jnp.dot(a, b, preferred_element_type=F32)


def _dot_t0(a, b):
    return lax.dot_general(a, b, (((0,), (0,)), ((), ())), preferred_element_type=F32)


def _dot_t1(a, b):
    return lax.dot_general(a, b, (((1,), (1,)), ((), ())), preferred_element_type=F32)


def _sigmoid(x):
    return 1.0 / (1.0 + jnp.exp(-x))


def _gelu(x):
    return 0.5 * x * (1.0 + lax.erf(x * (2.0 ** -0.5)))


def _layer_norm(x, g, b):
    mu = jnp.mean(x, axis=-1, keepdims=True)
    xc = x - mu
    var = jnp.mean(xc * xc, axis=-1, keepdims=True)
    return xc * lax.rsqrt(var + LN_EPS) * g + b


def _params(*sem):
    return pltpu.CompilerParams(dimension_semantics=sem, vmem_limit_bytes=VMEM_LIMIT)


def _mm_bias_kernel(x_ref, w_ref, b_ref, o_ref):
    o_ref[...] = (_dot(x_ref[...], w_ref[...]) + b_ref[...]).astype(o_ref.dtype)


def _matmul_bias(xb, w, b, out_dtype, tm, tn):
    m, k = xb.shape
    n = w.shape[1]
    tm, tn = _pick(m, tm), _pick(n, tn)
    return pl.pallas_call(
        _mm_bias_kernel,
        grid=(m // tm, n // tn),
        in_specs=[pl.BlockSpec((tm, k), lambda i, j: (i, 0)),
                  pl.BlockSpec((k, tn), lambda i, j: (0, j)),
                  pl.BlockSpec((1, tn), lambda i, j: (0, j))],
        out_specs=pl.BlockSpec((tm, tn), lambda i, j: (i, j)),
        out_shape=jax.ShapeDtypeStruct((m, n), out_dtype),
        compiler_params=_params("parallel", "parallel"),
        name="in_proj",
    )(xb, w, b)


def _merge_kernel(ya_ref, wa_ref, yb_ref, wb_ref, ga_ref, gb_ref, o_ref):
    a = _dot(ya_ref[...], wa_ref[...])
    b = _dot(yb_ref[...], wb_ref[...])
    o_ref[...] = (_sigmoid(ga_ref[...]) * a + _sigmoid(gb_ref[...]) * b).astype(o_ref.dtype)


def _merge(ya, wa, yb, wb, z, off_ga, off_gb, tm, tn):
    m, ka = ya.shape
    kb = yb.shape[1]
    n = wa.shape[1]
    tm, tn = _pick(m, tm), _pick(n, tn)
    while off_ga % tn or off_gb % tn:
        tn //= 2
    oa, ob = off_ga // tn, off_gb // tn
    return pl.pallas_call(
        _merge_kernel,
        grid=(m // tm, n // tn),
        in_specs=[pl.BlockSpec((tm, ka), lambda i, j: (i, 0)),
                  pl.BlockSpec((ka, tn), lambda i, j: (0, j)),
                  pl.BlockSpec((tm, kb), lambda i, j: (i, 0)),
                  pl.BlockSpec((kb, tn), lambda i, j: (0, j)),
                  pl.BlockSpec((tm, tn), lambda i, j: (i, oa + j)),
                  pl.BlockSpec((tm, tn), lambda i, j: (i, ob + j))],
        out_specs=pl.BlockSpec((tm, tn), lambda i, j: (i, j)),
        out_shape=jax.ShapeDtypeStruct((m, n), BF16),
        compiler_params=_params("parallel", "parallel"),
        name="merge",
    )(ya, wa, yb, wb, z, z)


def _mm_resid_kernel(m_ref, w_ref, x_ref, o_ref, *, alpha):
    o_ref[...] = alpha * x_ref[...] + _dot(m_ref[...], w_ref[...])


def _out_proj_resid(mb, w, x, alpha, tm, tn):
    m, k = mb.shape
    n = w.shape[1]
    tm, tn = _pick(m, tm), _pick(n, tn)
    return pl.pallas_call(
        functools.partial(_mm_resid_kernel, alpha=alpha),
        grid=(m // tm, n // tn),
        in_specs=[pl.BlockSpec((tm, k), lambda i, j: (i, 0)),
                  pl.BlockSpec((k, tn), lambda i, j: (0, j)),
                  pl.BlockSpec((tm, tn), lambda i, j: (i, j))],
        out_specs=pl.BlockSpec((tm, tn), lambda i, j: (i, j)),
        out_shape=jax.ShapeDtypeStruct((m, n), F32),
        compiler_params=_params("parallel", "parallel"),
        name="out_proj",
    )(mb, w, x)


def _branch_a_kernel(u_ref, v_ref, ws_ref, bs_ref, g_ref, b_ref, o_ref, *, n_groups, chunk):
    tm, aw = u_ref.shape
    gd = aw // n_groups
    row = lax.broadcasted_iota(I32, (chunk, chunk), 0)
    col = lax.broadcasted_iota(I32, (chunk, chunk), 1)
    causal = row >= col
    for g in range(n_groups):
        ws = jnp.where(causal, ws_ref[g], 0.0).astype(BF16)
        cs = slice(g * gd, (g + 1) * gd)
        for c in range(tm // chunk):
            rs = slice(c * chunk, (c + 1) * chunk)
            vn = _layer_norm(_gelu(v_ref[rs, cs]), g_ref[:, cs], b_ref[:, cs])
            mixed = _dot(ws, vn.astype(BF16)) + bs_ref[:, g:g + 1]
            o_ref[rs, cs] = (_gelu(u_ref[rs, cs]) * mixed).astype(o_ref.dtype)


def _branch_a(z, a_ws, a_bs, a_ln_g, a_ln_b, aw, tm):
    t = z.shape[0]
    n_groups, chunk, _ = a_ws.shape
    tm = max(_pick(t, tm), chunk)
    return pl.pallas_call(
        functools.partial(_branch_a_kernel, n_groups=n_groups, chunk=chunk),
        grid=(t // tm,),
        in_specs=[pl.BlockSpec((tm, aw), lambda i: (i, 0)),
                  pl.BlockSpec((tm, aw), lambda i: (i, 1)),
                  pl.BlockSpec((n_groups, chunk, chunk), lambda i: (0, 0, 0)),
                  pl.BlockSpec((chunk, n_groups), lambda i: (0, 0)),
                  pl.BlockSpec((1, aw), lambda i: (0, 0)),
                  pl.BlockSpec((1, aw), lambda i: (0, 0))],
        out_specs=pl.BlockSpec((tm, aw), lambda i: (i, 0)),
        out_shape=jax.ShapeDtypeStruct((t, aw), BF16),
        compiler_params=_params("parallel"),
        name="branch_a",
    )(z, z, a_ws, a_bs.T, a_ln_g.reshape(1, aw), a_ln_b.reshape(1, aw))


def _gla_kernel(q_ref, k_ref, v_ref, r_ref, lr_ref, wlr_ref, blr_ref, gn_ref, o_ref, s_ref, *, heads, chunk):
    @pl.when(pl.program_id(0) == 0)
    def _():
        s_ref[...] = jnp.zeros_like(s_ref)

    tg, kw = q_ref.shape
    vw = v_ref.shape[1]
    dk, dv = kw // heads, vw // heads
    row = lax.broadcasted_iota(I32, (chunk, chunk), 0)
    col = lax.broadcasted_iota(I32, (chunk, chunk), 1)
    causal = row >= col
    tri = causal.astype(BF16)
    ones = jnp.ones((chunk, LANES), BF16)
    scale = dk ** -0.5

    def body(c, carry):
        rs = pl.ds(pl.multiple_of(c * chunk, chunk), chunk)
        gl = _dot(lr_ref[rs, :].astype(BF16), wlr_ref[...]) + blr_ref[...]
        la = (jnp.minimum(gl, 0.0) - jnp.log1p(jnp.exp(-jnp.abs(gl)))) * (1.0 / GLA_GATE_TAU)
        la_hi = la.astype(BF16)
        la_lo = (la - la_hi.astype(F32)).astype(BF16)
        cum = _dot(tri, la_hi) + _dot(tri, la_lo)
        tot = cum[chunk - 1:chunk, :]
        tot_col = _dot_t0(la_hi, ones) + _dot_t0(la_lo, ones)
        dec_col = jnp.exp(tot_col)
        q = q_ref[rs, :] * scale
        k = k_ref[rs, :]
        qt = (q * jnp.exp(cum)).astype(BF16)
        kt = (k * jnp.exp(-cum)).astype(BF16)
        kl = (k * jnp.exp(tot - cum)).astype(BF16)
        for h in range(heads):
            ks = slice(h * dk, (h + 1) * dk)
            vs = slice(h * dv, (h + 1) * dv)
            vh = v_ref[rs, vs].astype(BF16)
            sc = jnp.where(causal, _dot_t1(qt[:, ks], kt[:, ks]), 0.0).astype(BF16)
            state = s_ref[h]
            o = _dot(sc, vh) + _dot(qt[:, ks], state.astype(BF16))
            dec = jnp.concatenate([dec_col[ks, :]] * (dv // LANES), axis=1) if dv >= LANES else dec_col[ks, :dv]
            s_ref[h] = dec * state + _dot_t0(kl[:, ks], vh)
            on = o * lax.rsqrt(jnp.mean(o * o, axis=-1, keepdims=True) + LN_EPS) * gn_ref[:, vs]
            rr = r_ref[rs, vs]
            o_ref[rs, vs] = (on * (rr * _sigmoid(rr))).astype(o_ref.dtype)
        return carry

    lax.fori_loop(0, tg // chunk, body, 0)


def _gla(z, lr, wlr, blr, gn, kw, vw, off_q, heads, tg):
    t = z.shape[0]
    tg = max(_pick(t, tg), GLA_CHUNK)
    oq = off_q // kw
    ok = oq + 1
    ov = (off_q + 2 * kw) // vw
    orr = ov + 1
    return pl.pallas_call(
        functools.partial(_gla_kernel, heads=heads, chunk=GLA_CHUNK),
        grid=(t // tg,),
        in_specs=[pl.BlockSpec((tg, kw), lambda i: (i, oq)),
                  pl.BlockSpec((tg, kw), lambda i: (i, ok)),
                  pl.BlockSpec((tg, vw), lambda i: (i, ov)),
                  pl.BlockSpec((tg, vw), lambda i: (i, orr)),
                  pl.BlockSpec((tg, LANES), lambda i: (i, 0)),
                  pl.BlockSpec((LANES, kw), lambda i: (0, 0)),
                  pl.BlockSpec((1, kw), lambda i: (0, 0)),
                  pl.BlockSpec((1, vw), lambda i: (0, 0))],
        out_specs=pl.BlockSpec((tg, vw), lambda i: (i, 0)),
        out_shape=jax.ShapeDtypeStruct((t, vw), BF16),
        scratch_shapes=[pltpu.VMEM((heads, kw // heads, vw // heads), F32)],
        compiler_params=_params("arbitrary"),
        name="gla",
    )(z, z, z, z, lr, wlr, blr, gn)


def _ln_router_kernel(h_ref, g_ref, b_ref, wr_ref, br_ref, x1_ref, xp_ref, idx_ref, wgt_ref, rank_ref, cnt_ref,
                      base_ref, *, top_k):
    @pl.when(pl.program_id(0) == 0)
    def _():
        base_ref[...] = jnp.zeros_like(base_ref)

    tm, d = h_ref.shape
    half = d // 2
    x1 = _layer_norm(h_ref[...], g_ref[...], b_ref[...])
    x1_ref[...] = x1
    lo = pltpu.bitcast(x1[:, :half].astype(BF16).astype(F32), U32) >> 16
    hi = pltpu.bitcast(x1[:, half:].astype(BF16).astype(F32), U32) & jnp.uint32(HI_MASK)
    xp_ref[...] = lo | hi

    logits = _dot(x1.astype(BF16), wr_ref[...]) + br_ref[...]
    lane = lax.broadcasted_iota(I32, (tm, LANES), 1).astype(F32)
    member = jnp.zeros((tm, LANES), F32)
    vals, idxs = [], []
    cur = logits
    for _ in range(top_k):
        mx = jnp.max(cur, axis=-1, keepdims=True)
        ix = jnp.min(jnp.where(cur == mx, lane, float(LANES)), axis=-1, keepdims=True)
        sel = lane == ix
        vals.append(mx)
        idxs.append(ix)
        member = member + sel.astype(F32)
        cur = jnp.where(sel, -jnp.inf, cur)
    exps = [jnp.exp(v - vals[0]) for v in vals]
    denom = exps[0]
    for e in exps[1:]:
        denom = denom + e
    r_i = lax.broadcasted_iota(I32, (tm, tm), 0)
    c_i = lax.broadcasted_iota(I32, (tm, tm), 1)
    before = _dot((r_i > c_i).astype(BF16), member.astype(BF16)) + base_ref[...]
    idx_out = jnp.zeros((tm, LANES), F32)
    wgt_out = jnp.zeros((tm, LANES), F32)
    rank_out = jnp.zeros((tm, LANES), F32)
    for j in range(top_k):
        rk = jnp.sum(jnp.where(lane == idxs[j], before, 0.0), axis=-1, keepdims=True)
        at = lane == float(j)
        idx_out = jnp.where(at, idxs[j], idx_out)
        wgt_out = jnp.where(at, exps[j] / denom, wgt_out)
        rank_out = jnp.where(at, rk, rank_out)
    idx_ref[...] = idx_out.astype(I32)
    wgt_ref[...] = wgt_out
    rank_ref[...] = rank_out.astype(I32)
    base_ref[...] = base_ref[...] + jnp.sum(member, axis=0, keepdims=True)
    cnt_ref[...] = base_ref[...]


def _ln_router(h1, g, b, wr, br, tm):
    t, d = h1.shape
    tm = _pick(t, tm)
    row = lambda i: (i, 0)
    fixed = lambda i: (0, 0)
    return pl.pallas_call(
        functools.partial(_ln_router_kernel, top_k=TOP_K),
        grid=(t // tm,),
        in_specs=[pl.BlockSpec((tm, d), row),
                  pl.BlockSpec((1, d), fixed),
                  pl.BlockSpec((1, d), fixed),
                  pl.BlockSpec((d, LANES), fixed),
                  pl.BlockSpec((1, LANES), fixed)],
        out_specs=[pl.BlockSpec((tm, d), row),
                   pl.BlockSpec((tm, d // 2), row),
                   pl.BlockSpec((tm, LANES), row),
                   pl.BlockSpec((tm, LANES), row),
                   pl.BlockSpec((tm, LANES), row),
                   pl.BlockSpec((1, LANES), fixed)],
        out_shape=[jax.ShapeDtypeStruct((t, d), F32),
                   jax.ShapeDtypeStruct((t, d // 2), U32),
                   jax.ShapeDtypeStruct((t, LANES), I32),
                   jax.ShapeDtypeStruct((t, LANES), F32),
                   jax.ShapeDtypeStruct((t, LANES), I32),
                   jax.ShapeDtypeStruct((1, LANES), F32)],
        scratch_shapes=[pltpu.VMEM((1, LANES), F32)],
        compiler_params=_params("arbitrary"),
        name="ln_router",
    )(h1, g, b, wr, br)


def _dispatch_kernel(gend_ref, npad_ref, nb_ref, dest_ref, xp_hbm, zblk_hbm, xs_hbm, sem, *, tt, top_k, n_exp,
                     tm_e, n_blocks):
    i = pl.program_id(0)

    def row_copy(t, j):
        return pltpu.make_async_copy(xp_hbm.at[pl.ds(i * tt + t, 1)],
                                     xs_hbm.at[pl.ds(dest_ref[t * top_k + j], 1)], sem)

    def issue(t, c):
        for j in range(top_k):
            row_copy(t, j).start()
        return c

    def drain(t, c):
        for j in range(top_k):
            row_copy(t, j).wait()
        return c

    lax.fori_loop(0, tt, issue, 0)
    lax.fori_loop(0, tt, drain, 0)

    @pl.when(i == 0)
    def _():
        def per_expert(e, c):
            base = gend_ref[e]

            def zero_copy(r):
                return pltpu.make_async_copy(zblk_hbm.at[pl.ds(0, 1)], xs_hbm.at[pl.ds(base + r, 1)], sem)

            def z_issue(r, cc):
                zero_copy(r).start()
                return cc

            def z_drain(r, cc):
                zero_copy(r).wait()
                return cc

            lax.fori_loop(0, npad_ref[e], z_issue, 0)
            lax.fori_loop(0, npad_ref[e], z_drain, 0)
            return c

        lax.fori_loop(0, n_exp, per_expert, 0)

        def tail_copy(b):
            return pltpu.make_async_copy(zblk_hbm, xs_hbm.at[pl.ds(pl.multiple_of(b * tm_e, tm_e), tm_e)], sem)

        def t_issue(b, c):
            tail_copy(b).start()
            return c

        def t_drain(b, c):
            tail_copy(b).wait()
            return c

        lax.fori_loop(nb_ref[0], n_blocks, t_issue, 0)
        lax.fori_loop(nb_ref[0], n_blocks, t_drain, 0)


def _dispatch(dest_flat, gend, npad, nb_used, xp, n_rows, tm_e, tt):
    t, dh = xp.shape
    n_exp = gend.shape[0]
    tt = _pick(t, tt)
    zblk = jnp.zeros((tm_e, dh), U32)
    grid_spec = pltpu.PrefetchScalarGridSpec(
        num_scalar_prefetch=3,
        grid=(t // tt,),
        in_specs=[pl.BlockSpec((tt * TOP_K,), lambda i, *_: (i,), memory_space=pltpu.SMEM),
                  pl.BlockSpec(memory_space=pl.ANY),
                  pl.BlockSpec(memory_space=pl.ANY)],
        out_specs=pl.BlockSpec(memory_space=pl.ANY),
        scratch_shapes=[pltpu.SemaphoreType.DMA(())],
    )
    return pl.pallas_call(
        functools.partial(_dispatch_kernel, tt=tt, top_k=TOP_K, n_exp=n_exp, tm_e=tm_e, n_blocks=n_rows // tm_e),
        grid_spec=grid_spec,
        out_shape=jax.ShapeDtypeStruct((n_rows, dh), U32),
        compiler_params=_params("arbitrary"),
        name="dispatch",
    )(gend, npad, nb_used, dest_flat, xp, zblk)


def _moe1_kernel(be_ref, nb_ref, xs_ref, wg_ref, wl_ref, bg_ref, bl_ref, h_ref):
    used = pl.program_id(1) < nb_ref[0]

    @pl.when(jnp.logical_not(used))
    def _():
        h_ref[...] = jnp.zeros_like(h_ref)

    @pl.when(used)
    def _():
        w = xs_ref[...]
        lo = pltpu.bitcast(w << 16, F32).astype(BF16)
        hi = pltpu.bitcast(w & jnp.uint32(HI_MASK), F32).astype(BF16)
        x = jnp.concatenate([lo, hi], axis=1)
        gate = jnp.minimum(_dot(x, wg_ref[...]) + bg_ref[...], SWIGLU_LIMIT)
        lin = jnp.clip(_dot(x, wl_ref[...]) + bl_ref[...], -SWIGLU_LIMIT, SWIGLU_LIMIT)
        h_ref[...] = (gate * _sigmoid(SWIGLU_ALPHA * gate) * (lin + 1.0)).astype(h_ref.dtype)


def _moe1(block_e, nb_used, xs, w_gu, b_gu, tm, tn):
    n_rows, dh = xs.shape
    n_exp, d, de2 = w_gu.shape
    de = de2 // 2
    tn = _pick(de, tn)
    nj = de // tn

    def blk(b, nb):
        return jnp.minimum(b, nb[0] - 1)

    grid_spec = pltpu.PrefetchScalarGridSpec(
        num_scalar_prefetch=2,
        grid=(nj, n_rows // tm),
        in_specs=[pl.BlockSpec((tm, dh), lambda j, b, be, nb: (blk(b, nb), 0)),
                  pl.BlockSpec((None, d, tn), lambda j, b, be, nb: (be[blk(b, nb)], 0, j)),
                  pl.BlockSpec((None, d, tn), lambda j, b, be, nb: (be[blk(b, nb)], 0, nj + j)),
                  pl.BlockSpec((None, 1, tn), lambda j, b, be, nb: (be[blk(b, nb)], 0, j)),
                  pl.BlockSpec((None, 1, tn), lambda j, b, be, nb: (be[blk(b, nb)], 0, nj + j))],
        out_specs=pl.BlockSpec((tm, tn), lambda j, b, be, nb: (b, j)),
    )
    return pl.pallas_call(
        _moe1_kernel,
        grid_spec=grid_spec,
        out_shape=jax.ShapeDtypeStruct((n_rows, de), BF16),
        compiler_params=_params("arbitrary", "arbitrary"),
        name="moe_up",
    )(block_e, nb_used, xs, w_gu, w_gu, b_gu, b_gu)


def _moe2_kernel(be_ref, nb_ref, h_ref, w_ref, b_ref, y_ref):
    used = pl.program_id(1) < nb_ref[0]

    @pl.when(jnp.logical_not(used))
    def _():
        y_ref[...] = jnp.zeros_like(y_ref)

    @pl.when(used)
    def _():
        y_ref[...] = _dot(h_ref[...], w_ref[...]) + b_ref[...]


def _moe2(block_e, nb_used, h, w_down, b_down, tm, tn):
    n_rows, de = h.shape
    n_exp, _, d = w_down.shape
    tn = _pick(d, tn)

    def blk(b, nb):
        return jnp.minimum(b, nb[0] - 1)

    grid_spec = pltpu.PrefetchScalarGridSpec(
        num_scalar_prefetch=2,
        grid=(d // tn, n_rows // tm),
        in_specs=[pl.BlockSpec((tm, de), lambda j, b, be, nb: (blk(b, nb), 0)),
                  pl.BlockSpec((None, de, tn), lambda j, b, be, nb: (be[blk(b, nb)], 0, j)),
                  pl.BlockSpec((None, 1, tn), lambda j, b, be, nb: (be[blk(b, nb)], 0, j))],
        out_specs=pl.BlockSpec((tm, tn), lambda j, b, be, nb: (b, j)),
    )
    return pl.pallas_call(
        _moe2_kernel,
        grid_spec=grid_spec,
        out_shape=jax.ShapeDtypeStruct((n_rows, d), F32),
        compiler_params=_params("arbitrary", "arbitrary"),
        name="moe_down",
    )(block_e, nb_used, h, w_down, b_down)


def _combine_kernel(dest_ref, ys_hbm, x1_ref, wgt_ref, g_ref, b_ref, o_ref, buf, sem, *, tc, top_k, alpha):
    def row_copy(t, j):
        return pltpu.make_async_copy(ys_hbm.at[pl.ds(dest_ref[t * top_k + j], 1)], buf.at[j, pl.ds(t, 1)], sem)

    def issue(t, c):
        for j in range(top_k):
            row_copy(t, j).start()
        return c

    def drain(t, c):
        for j in range(top_k):
            row_copy(t, j).wait()
        return c

    lax.fori_loop(0, tc, issue, 0)
    lax.fori_loop(0, tc, drain, 0)
    y = wgt_ref[:, 0:1] * buf[0]
    for j in range(1, top_k):
        y = y + wgt_ref[:, j:j + 1] * buf[j]
    o_ref[...] = _layer_norm(alpha * x1_ref[...] + y, g_ref[...], b_ref[...])


def _combine(dest_flat, ys, x1, wgt, g, b, alpha, tc):
    t, d = x1.shape
    tc = _pick(t, tc)
    return pl.pallas_call(
        functools.partial(_combine_kernel, tc=tc, top_k=TOP_K, alpha=alpha),
        grid=(t // tc,),
        in_specs=[pl.BlockSpec((tc * TOP_K,), lambda i: (i,), memory_space=pltpu.SMEM),
                  pl.BlockSpec(memory_space=pl.ANY),
                  pl.BlockSpec((tc, d), lambda i: (i, 0)),
                  pl.BlockSpec((tc, LANES), lambda i: (i, 0)),
                  pl.BlockSpec((1, d), lambda i: (0, 0)),
                  pl.BlockSpec((1, d), lambda i: (0, 0))],
        out_specs=pl.BlockSpec((tc, d), lambda i: (i, 0)),
        out_shape=jax.ShapeDtypeStruct((t, d), F32),
        scratch_shapes=[pltpu.VMEM((TOP_K, tc, d), F32), pltpu.SemaphoreType.DMA(())],
        compiler_params=_params("arbitrary"),
        name="combine",
    )(dest_flat, ys, x1, wgt, g, b)


def _layer(x2d, w_in, b_in, a_ws, a_bs, a_ln_g, a_ln_b, gla_w_lr, gla_b_lr, gla_gn_g, w_br_a, w_br_b, w_o,
           ln1_g, ln1_b, w_router, b_router, w_gu, b_gu, w_down, b_down, ln2_g, ln2_b, alpha):
    t, d = x2d.shape
    n_groups, a_chunk, _ = a_ws.shape
    aw = a_ln_g.size
    rank, kw = gla_w_lr.shape
    heads, dv = gla_gn_g.shape
    vw = heads * dv
    n_exp = w_router.shape[1]
    de = w_down.shape[1]
    off_lr = 2 * aw + 2 * kw + 2 * vw
    off_q = 2 * aw
    off_ga = off_lr
    off_gb = off_ga + d

    w_main = jnp.concatenate([w_in[:, :off_lr], w_in[:, off_lr + rank:]], axis=1).astype(BF16)
    b_main = jnp.concatenate([b_in[:off_lr], b_in[off_lr + rank:]]).reshape(1, -1)
    w_lr_in = jnp.pad(w_in[:, off_lr:off_lr + rank], ((0, 0), (0, LANES - rank))).astype(BF16)
    b_lr_in = jnp.pad(b_in[off_lr:off_lr + rank], (0, LANES - rank)).reshape(1, LANES)
    wlr = jnp.pad(gla_w_lr, ((0, LANES - rank), (0, 0))).astype(BF16)
    wr = jnp.pad(w_router, ((0, 0), (0, LANES - n_exp))).astype(BF16)
    br = jnp.pad(b_router, (0, LANES - n_exp), constant_values=-jnp.inf).reshape(1, LANES)

    xb = x2d.astype(BF16)
    z = _matmul_bias(xb, w_main, b_main, F32, 1024, 1024)
    lr = _matmul_bias(xb, w_lr_in, b_lr_in, F32, 2048, LANES)

    ya = _branch_a(z, a_ws, a_bs, a_ln_g, a_ln_b, aw, 2 * a_chunk)
    yb = _gla(z, lr, wlr, gla_b_lr.reshape(1, kw), gla_gn_g.reshape(1, vw), kw, vw, off_q, heads, 256)
    merged = _merge(ya, w_br_a.astype(BF16), yb, w_br_b.astype(BF16), z, off_ga, off_gb, 1024, 512)
    h1 = _out_proj_resid(merged, w_o.astype(BF16), x2d, alpha, 1024, 512)

    x1, xp, idx, wgt, rnk, cnt = _ln_router(h1, ln1_g.reshape(1, d), ln1_b.reshape(1, d), wr, br, 256)

    tm_e = 512 if t * TOP_K >= 512 * n_exp else 64
    n_assign = t * TOP_K
    n_blocks = -(-(n_assign + n_exp * (tm_e - 1)) // tm_e)
    n_rows = n_blocks * tm_e
    counts = cnt[0, :n_exp].astype(I32)
    padded = (counts + tm_e - 1) // tm_e * tm_e
    pend = jnp.cumsum(padded)
    pstart = pend - padded
    sel = idx[:, :TOP_K, None] == jnp.arange(n_exp, dtype=I32)[None, None, :]
    dest = jnp.sum(jnp.where(sel, pstart[None, None, :], 0), axis=-1) + rnk[:, :TOP_K]
    dest_flat = dest.reshape(-1).astype(I32)
    nb_used = (pend[-1] // tm_e).astype(I32).reshape(1)
    block_e = jnp.minimum(jnp.searchsorted(pend, jnp.arange(n_blocks, dtype=I32) * tm_e, side='right'),
                          n_exp - 1).astype(I32)

    xs = _dispatch(dest_flat, (pstart + counts).astype(I32), (padded - counts).astype(I32), nb_used, xp, n_rows,
                   tm_e, 256)
    hmid = _moe1(block_e, nb_used, xs, w_gu.astype(BF16), b_gu.reshape(n_exp, 1, 2 * de), tm_e, 512)
    ys = _moe2(block_e, nb_used, hmid, w_down.astype(BF16), b_down.reshape(n_exp, 1, d), tm_e, 2048)
    return _combine(dest_flat, ys, x1, wgt, ln2_g.reshape(1, d), ln2_b.reshape(1, d), alpha, 128)


def kernel(x, w_in, b_in, a_ws, a_bs, a_ln_g, a_ln_b, gla_w_lr, gla_b_lr, gla_gn_g, w_br_a, w_br_b, w_o, ln1_g, ln1_b, w_router, b_router, w_gu, b_gu, w_down, b_down, ln2_g, ln2_b):
    bsz, seq, d = x.shape
    depth = w_in.shape[0]
    alpha = (2 * depth) ** 0.25
    outs = []
    for bi in range(bsz):
        h = x[bi]
        for l in range(depth):
            h = _layer(h, w_in[l], b_in[l], a_ws[l], a_bs[l], a_ln_g[l], a_ln_b[l], gla_w_lr[l], gla_b_lr[l],
                       gla_gn_g[l], w_br_a[l], w_br_b[l], w_o[l], ln1_g[l], ln1_b[l], w_router[l], b_router[l],
                       w_gu[l], b_gu[l], w_down[l], b_down[l], ln2_g[l], ln2_b[l], alpha)
        outs.append(h)
    return jnp.stack(outs) if bsz > 1 else outs[0][None]
```

```python
import functools

import jax
import jax.numpy as jnp
from jax import lax
from jax.experimental import pallas as pl
from jax.experimental.pallas import tpu as pltpu

F32 = jnp.float32
BF16 = jnp.bfloat16
U32 = jnp.uint32
I32 = jnp.int32

LN_EPS = 1e-5
GLA_CHUNK = 64
GLA_GATE_TAU = 16.0
TOP_K = 4
SWIGLU_LIMIT = 7.0
SWIGLU_ALPHA = 1.702
LANES = 128
VMEM_LIMIT = 56 * 1024 * 1024
HI_MASK = 0xFFFF0000


def _pick(n, pref):
    t = min(pref, n)
    while n % t:
        t //= 2
    return t


def _dot(a, b):
    return jnp.dot(a, b, preferred_element_type=F32)


def _dot_t0(a, b):
    return lax.dot_general(a, b, (((0,), (0,)), ((), ())), preferred_element_type=F32)


def _dot_t1(a, b):
    return lax.dot_general(a, b, (((1,), (1,)), ((), ())), preferred_element_type=F32)


def _sigmoid(x):
    return 1.0 / (1.0 + jnp.exp(-x))


def _gelu(x):
    return 0.5 * x * (1.0 + lax.erf(x * (2.0 ** -0.5)))


def _layer_norm(x, g, b):
    mu = jnp.mean(x, axis=-1, keepdims=True)
    xc = x - mu
    var = jnp.mean(xc * xc, axis=-1, keepdims=True)
    return xc * lax.rsqrt(var + LN_EPS) * g + b


def _params(*sem):
    return pltpu.CompilerParams(dimension_semantics=sem, vmem_limit_bytes=VMEM_LIMIT)


def _mm_bias_kernel(x_ref, w_ref, b_ref, o_ref):
    o_ref[...] = (_dot(x_ref[...], w_ref[...]) + b_ref[...]).astype(o_ref.dtype)


def _matmul_bias(xb, w, b, out_dtype, tm, tn):
    m, k = xb.shape
    n = w.shape[1]
    tm, tn = _pick(m, tm), _pick(n, tn)
    return pl.pallas_call(
        _mm_bias_kernel,
        grid=(m // tm, n // tn),
        in_specs=[pl.BlockSpec((tm, k), lambda i, j: (i, 0)),
                  pl.BlockSpec((k, tn), lambda i, j: (0, j)),
                  pl.BlockSpec((1, tn), lambda i, j: (0, j))],
        out_specs=pl.BlockSpec((tm, tn), lambda i, j: (i, j)),
        out_shape=jax.ShapeDtypeStruct((m, n), out_dtype),
        compiler_params=_params("parallel", "parallel"),
        name="in_proj",
    )(xb, w, b)


def _merge_kernel(ya_ref, wa_ref, yb_ref, wb_ref, ga_ref, gb_ref, o_ref):
    a = _dot(ya_ref[...], wa_ref[...])
    b = _dot(yb_ref[...], wb_ref[...])
    o_ref[...] = (_sigmoid(ga_ref[...]) * a + _sigmoid(gb_ref[...]) * b).astype(o_ref.dtype)


def _merge(ya, wa, yb, wb, z, off_ga, off_gb, tm, tn):
    m, ka = ya.shape
    kb = yb.shape[1]
    n = wa.shape[1]
    tm, tn = _pick(m, tm), _pick(n, tn)
    while off_ga % tn or off_gb % tn:
        tn //= 2
    oa, ob = off_ga // tn, off_gb // tn
    return pl.pallas_call(
        _merge_kernel,
        grid=(m // tm, n // tn),
        in_specs=[pl.BlockSpec((tm, ka), lambda i, j: (i, 0)),
                  pl.BlockSpec((ka, tn), lambda i, j: (0, j)),
                  pl.BlockSpec((tm, kb), lambda i, j: (i, 0)),
                  pl.BlockSpec((kb, tn), lambda i, j: (0, j)),
                  pl.BlockSpec((tm, tn), lambda i, j: (i, oa + j)),
                  pl.BlockSpec((tm, tn), lambda i, j: (i, ob + j))],
        out_specs=pl.BlockSpec((tm, tn), lambda i, j: (i, j)),
        out_shape=jax.ShapeDtypeStruct((m, n), BF16),
        compiler_params=_params("parallel", "parallel"),
        name="merge",
    )(ya, wa, yb, wb, z, z)


def _mm_resid_kernel(m_ref, w_ref, x_ref, o_ref, *, alpha):
    o_ref[...] = alpha * x_ref[...] + _dot(m_ref[...], w_ref[...])


def _out_proj_resid(mb, w, x, alpha, tm, tn):
    m, k = mb.shape
    n = w.shape[1]
    tm, tn = _pick(m, tm), _pick(n, tn)
    return pl.pallas_call(
        functools.partial(_mm_resid_kernel, alpha=alpha),
        grid=(m // tm, n // tn),
        in_specs=[pl.BlockSpec((tm, k), lambda i, j: (i, 0)),
                  pl.BlockSpec((k, tn), lambda i, j: (0, j)),
                  pl.BlockSpec((tm, tn), lambda i, j: (i, j))],
        out_specs=pl.BlockSpec((tm, tn), lambda i, j: (i, j)),
        out_shape=jax.ShapeDtypeStruct((m, n), F32),
        compiler_params=_params("parallel", "parallel"),
        name="out_proj",
    )(mb, w, x)


def _branch_a_kernel(u_ref, v_ref, ws_ref, bs_ref, g_ref, b_ref, o_ref, *, n_groups, chunk):
    tm, aw = u_ref.shape
    gd = aw // n_groups
    row = lax.broadcasted_iota(I32, (chunk, chunk), 0)
    col = lax.broadcasted_iota(I32, (chunk, chunk), 1)
    causal = row >= col
    for g in range(n_groups):
        ws = jnp.where(causal, ws_ref[g], 0.0).astype(BF16)
        cs = slice(g * gd, (g + 1) * gd)
        for c in range(tm // chunk):
            rs = slice(c * chunk, (c + 1) * chunk)
            vn = _layer_norm(_gelu(v_ref[rs, cs]), g_ref[:, cs], b_ref[:, cs])
            mixed = _dot(ws, vn.astype(BF16)) + bs_ref[:, g:g + 1]
            o_ref[rs, cs] = (_gelu(u_ref[rs, cs]) * mixed).astype(o_ref.dtype)


def _branch_a(z, a_ws, a_bs, a_ln_g, a_ln_b, aw, tm):
    t = z.shape[0]
    n_groups, chunk, _ = a_ws.shape
    tm = max(_pick(t, tm), chunk)
    return pl.pallas_call(
        functools.partial(_branch_a_kernel, n_groups=n_groups, chunk=chunk),
        grid=(t // tm,),
        in_specs=[pl.BlockSpec((tm, aw), lambda i: (i, 0)),
                  pl.BlockSpec((tm, aw), lambda i: (i, 1)),
                  pl.BlockSpec((n_groups, chunk, chunk), lambda i: (0, 0, 0)),
                  pl.BlockSpec((chunk, n_groups), lambda i: (0, 0)),
                  pl.BlockSpec((1, aw), lambda i: (0, 0)),
                  pl.BlockSpec((1, aw), lambda i: (0, 0))],
        out_specs=pl.BlockSpec((tm, aw), lambda i: (i, 0)),
        out_shape=jax.ShapeDtypeStruct((t, aw), BF16),
        compiler_params=_params("parallel"),
        name="branch_a",
    )(z, z, a_ws, a_bs.T, a_ln_g.reshape(1, aw), a_ln_b.reshape(1, aw))


def _gla_kernel(q_ref, k_ref, v_ref, r_ref, lr_ref, wlr_ref, blr_ref, gn_ref, o_ref, s_ref, *, heads, chunk):
    @pl.when(pl.program_id(0) == 0)
    def _():
        s_ref[...] = jnp.zeros_like(s_ref)

    tg, kw = q_ref.shape
    vw = v_ref.shape[1]
    dk, dv = kw // heads, vw // heads
    row = lax.broadcasted_iota(I32, (chunk, chunk), 0)
    col = lax.broadcasted_iota(I32, (chunk, chunk), 1)
    causal = row >= col
    tri = causal.astype(BF16)
    ones = jnp.ones((chunk, LANES), BF16)
    scale = dk ** -0.5

    def body(c, carry):
        rs = pl.ds(pl.multiple_of(c * chunk, chunk), chunk)
        gl = _dot(lr_ref[rs, :].astype(BF16), wlr_ref[...]) + blr_ref[...]
        la = (jnp.minimum(gl, 0.0) - jnp.log1p(jnp.exp(-jnp.abs(gl)))) * (1.0 / GLA_GATE_TAU)
        la_hi = la.astype(BF16)
        la_lo = (la - la_hi.astype(F32)).astype(BF16)
        cum = _dot(tri, la_hi) + _dot(tri, la_lo)
        tot = cum[chunk - 1:chunk, :]
        tot_col = _dot_t0(la_hi, ones) + _dot_t0(la_lo, ones)
        dec_col = jnp.exp(tot_col)
        q = q_ref[rs, :] * scale
        k = k_ref[rs, :]
        qt = (q * jnp.exp(cum)).astype(BF16)
        kt = (k * jnp.exp(-cum)).astype(BF16)
        kl = (k * jnp.exp(tot - cum)).astype(BF16)
        for h in range(heads):
            ks = slice(h * dk, (h + 1) * dk)
            vs = slice(h * dv, (h + 1) * dv)
            vh = v_ref[rs, vs].astype(BF16)
            sc = jnp.where(causal, _dot_t1(qt[:, ks], kt[:, ks]), 0.0).astype(BF16)
            state = s_ref[h]
            o = _dot(sc, vh) + _dot(qt[:, ks], state.astype(BF16))
            dec = jnp.concatenate([dec_col[ks, :]] * (dv // LANES), axis=1) if dv >= LANES else dec_col[ks, :dv]
            s_ref[h] = dec * state + _dot_t0(kl[:, ks], vh)
            on = o * lax.rsqrt(jnp.mean(o * o, axis=-1, keepdims=True) + LN_EPS) * gn_ref[:, vs]
            rr = r_ref[rs, vs]
            o_ref[rs, vs] = (on * (rr * _sigmoid(rr))).astype(o_ref.dtype)
        return carry

    lax.fori_loop(0, tg // chunk, body, 0)


def _gla(z, lr, wlr, blr, gn, kw, vw, off_q, heads, tg):
    t = z.shape[0]
    tg = max(_pick(t, tg), GLA_CHUNK)
    oq = off_q // kw
    ok = oq + 1
    ov = (off_q + 2 * kw) // vw
    orr = ov + 1
    return pl.pallas_call(
        functools.partial(_gla_kernel, heads=heads, chunk=GLA_CHUNK),
        grid=(t // tg,),
        in_specs=[pl.BlockSpec((tg, kw), lambda i: (i, oq)),
                  pl.BlockSpec((tg, kw), lambda i: (i, ok)),
                  pl.BlockSpec((tg, vw), lambda i: (i, ov)),
                  pl.BlockSpec((tg, vw), lambda i: (i, orr)),
                  pl.BlockSpec((tg, LANES), lambda i: (i, 0)),
                  pl.BlockSpec((LANES, kw), lambda i: (0, 0)),
                  pl.BlockSpec((1, kw), lambda i: (0, 0)),
                  pl.BlockSpec((1, vw), lambda i: (0, 0))],
        out_specs=pl.BlockSpec((tg, vw), lambda i: (i, 0)),
        out_shape=jax.ShapeDtypeStruct((t, vw), BF16),
        scratch_shapes=[pltpu.VMEM((heads, kw // heads, vw // heads), F32)],
        compiler_params=_params("arbitrary"),
        name="gla",
    )(z, z, z, z, lr, wlr, blr, gn)


def _ln_router_kernel(h_ref, g_ref, b_ref, wr_ref, br_ref, x1_ref, xp_ref, idx_ref, wgt_ref, rank_ref, cnt_ref,
                      base_ref, *, top_k):
    @pl.when(pl.program_id(0) == 0)
    def _():
        base_ref[...] = jnp.zeros_like(base_ref)

    tm, d = h_ref.shape
    half = d // 2
    x1 = _layer_norm(h_ref[...], g_ref[...], b_ref[...])
    x1_ref[...] = x1
    lo = pltpu.bitcast(x1[:, :half].astype(BF16).astype(F32), U32) >> 16
    hi = pltpu.bitcast(x1[:, half:].astype(BF16).astype(F32), U32) & jnp.uint32(HI_MASK)
    xp_ref[...] = lo | hi

    logits = _dot(x1.astype(BF16), wr_ref[...]) + br_ref[...]
    lane = lax.broadcasted_iota(I32, (tm, LANES), 1).astype(F32)
    member = jnp.zeros((tm, LANES), F32)
    vals, idxs = [], []
    cur = logits
    for _ in range(top_k):
        mx = jnp.max(cur, axis=-1, keepdims=True)
        ix = jnp.min(jnp.where(cur == mx, lane, float(LANES)), axis=-1, keepdims=True)
        sel = lane == ix
        vals.append(mx)
        idxs.append(ix)
        member = member + sel.astype(F32)
        cur = jnp.where(sel, -jnp.inf, cur)
    exps = [jnp.exp(v - vals[0]) for v in vals]
    denom = exps[0]
    for e in exps[1:]:
        denom = denom + e
    r_i = lax.broadcasted_iota(I32, (tm, tm), 0)
    c_i = lax.broadcasted_iota(I32, (tm, tm), 1)
    before = _dot((r_i > c_i).astype(BF16), member.astype(BF16)) + base_ref[...]
    idx_out = jnp.zeros((tm, LANES), F32)
    wgt_out = jnp.zeros((tm, LANES), F32)
    rank_out = jnp.zeros((tm, LANES), F32)
    for j in range(top_k):
        rk = jnp.sum(jnp.where(lane == idxs[j], before, 0.0), axis=-1, keepdims=True)
        at = lane == float(j)
        idx_out = jnp.where(at, idxs[j], idx_out)
        wgt_out = jnp.where(at, exps[j] / denom, wgt_out)
        rank_out = jnp.where(at, rk, rank_out)
    idx_ref[...] = idx_out.astype(I32)
    wgt_ref[...] = wgt_out
    rank_ref[...] = rank_out.astype(I32)
    base_ref[...] = base_ref[...] + jnp.sum(member, axis=0, keepdims=True)
    cnt_ref[...] = base_ref[...]


def _ln_router(h1, g, b, wr, br, tm):
    t, d = h1.shape
    tm = _pick(t, tm)
    row = lambda i: (i, 0)
    fixed = lambda i: (0, 0)
    return pl.pallas_call(
        functools.partial(_ln_router_kernel, top_k=TOP_K),
        grid=(t // tm,),
        in_specs=[pl.BlockSpec((tm, d), row),
                  pl.BlockSpec((1, d), fixed),
                  pl.BlockSpec((1, d), fixed),
                  pl.BlockSpec((d, LANES), fixed),
                  pl.BlockSpec((1, LANES), fixed)],
        out_specs=[pl.BlockSpec((tm, d), row),
                   pl.BlockSpec((tm, d // 2), row),
                   pl.BlockSpec((tm, LANES), row),
                   pl.BlockSpec((tm, LANES), row),
                   pl.BlockSpec((tm, LANES), row),
                   pl.BlockSpec((1, LANES), fixed)],
        out_shape=[jax.ShapeDtypeStruct((t, d), F32),
                   jax.ShapeDtypeStruct((t, d // 2), U32),
                   jax.ShapeDtypeStruct((t, LANES), I32),
                   jax.ShapeDtypeStruct((t, LANES), F32),
                   jax.ShapeDtypeStruct((t, LANES), I32),
                   jax.ShapeDtypeStruct((1, LANES), F32)],
        scratch_shapes=[pltpu.VMEM((1, LANES), F32)],
        compiler_params=_params("arbitrary"),
        name="ln_router",
    )(h1, g, b, wr, br)


def _dispatch_kernel(gend_ref, npad_ref, nb_ref, dest_ref, xp_ref, zblk_hbm, xs_hbm, zrow, sem, *, tt, top_k, n_exp,
                     tm_e, n_blocks):
    i = pl.program_id(0)

    def row_copy(t, j):
        return pltpu.make_async_copy(xp_ref.at[pl.ds(t, 1)], xs_hbm.at[pl.ds(dest_ref[t * top_k + j], 1)], sem)

    def issue(t, c):
        for j in range(top_k):
            row_copy(t, j).start()
        return c

    def drain(t, c):
        for j in range(top_k):
            row_copy(t, j).wait()
        return c

    lax.fori_loop(0, tt, issue, 0)
    lax.fori_loop(0, tt, drain, 0)

    @pl.when(i == 0)
    def _():
        zrow[...] = jnp.zeros_like(zrow)

        def per_expert(e, c):
            base = gend_ref[e]

            def zero_copy(r):
                return pltpu.make_async_copy(zrow.at[pl.ds(0, 1)], xs_hbm.at[pl.ds(base + r, 1)], sem)

            def z_issue(r, cc):
                zero_copy(r).start()
                return cc

            def z_drain(r, cc):
                zero_copy(r).wait()
                return cc

            lax.fori_loop(0, npad_ref[e], z_issue, 0)
            lax.fori_loop(0, npad_ref[e], z_drain, 0)
            return c

        lax.fori_loop(0, n_exp, per_expert, 0)

        def tail_copy(b):
            return pltpu.make_async_copy(zblk_hbm, xs_hbm.at[pl.ds(pl.multiple_of(b * tm_e, tm_e), tm_e)], sem)

        def t_issue(b, c):
            tail_copy(b).start()
            return c

        def t_drain(b, c):
            tail_copy(b).wait()
            return c

        lax.fori_loop(nb_ref[0], n_blocks, t_issue, 0)
        lax.fori_loop(nb_ref[0], n_blocks, t_drain, 0)


def _dispatch(dest_flat, gend, npad, nb_used, xp, n_rows, tm_e, tt):
    t, dh = xp.shape
    n_exp = gend.shape[0]
    tt = _pick(t, tt)
    zblk = jnp.zeros((tm_e, dh), U32)
    grid_spec = pltpu.PrefetchScalarGridSpec(
        num_scalar_prefetch=3,
        grid=(t // tt,),
        in_specs=[pl.BlockSpec((tt * TOP_K,), lambda i, *_: (i,), memory_space=pltpu.SMEM),
                  pl.BlockSpec((tt, dh), lambda i, *_: (i, 0)),
                  pl.BlockSpec(memory_space=pl.ANY)],
        out_specs=pl.BlockSpec(memory_space=pl.ANY),
        scratch_shapes=[pltpu.VMEM((8, dh), U32), pltpu.SemaphoreType.DMA(())],
    )
    return pl.pallas_call(
        functools.partial(_dispatch_kernel, tt=tt, top_k=TOP_K, n_exp=n_exp, tm_e=tm_e, n_blocks=n_rows // tm_e),
        grid_spec=grid_spec,
        out_shape=jax.ShapeDtypeStruct((n_rows, dh), U32),
        compiler_params=_params("arbitrary"),
        name="dispatch",
    )(gend, npad, nb_used, dest_flat, xp, zblk)


def _expert_changed(be_ref, nb_ref):
    b = pl.program_id(1)
    fresh = jnp.logical_or(b == 0, be_ref[b] != be_ref[jnp.maximum(b - 1, 0)])
    return b < nb_ref[0], fresh


def _moe1_kernel(be_ref, nb_ref, xs_ref, wg_ref, wl_ref, bg_ref, bl_ref, h_ref, wg_s, wl_s):
    used, fresh = _expert_changed(be_ref, nb_ref)

    @pl.when(jnp.logical_not(used))
    def _():
        h_ref[...] = jnp.zeros_like(h_ref)

    @pl.when(jnp.logical_and(used, fresh))
    def _():
        wg_s[...] = wg_ref[...].astype(BF16)
        wl_s[...] = wl_ref[...].astype(BF16)

    @pl.when(used)
    def _():
        w = xs_ref[...]
        lo = pltpu.bitcast(w << 16, F32).astype(BF16)
        hi = pltpu.bitcast(w & jnp.uint32(HI_MASK), F32).astype(BF16)
        x = jnp.concatenate([lo, hi], axis=1)
        gate = jnp.minimum(_dot(x, wg_s[...]) + bg_ref[...], SWIGLU_LIMIT)
        lin = jnp.clip(_dot(x, wl_s[...]) + bl_ref[...], -SWIGLU_LIMIT, SWIGLU_LIMIT)
        h_ref[...] = (gate * _sigmoid(SWIGLU_ALPHA * gate) * (lin + 1.0)).astype(h_ref.dtype)


def _moe1(block_e, nb_used, xs, w_gu, b_gu, tm, tn):
    n_rows, dh = xs.shape
    n_exp, d, de2 = w_gu.shape
    de = de2 // 2
    tn = _pick(de, tn)
    nj = de // tn

    def blk(b, nb):
        return jnp.minimum(b, nb[0] - 1)

    grid_spec = pltpu.PrefetchScalarGridSpec(
        num_scalar_prefetch=2,
        grid=(nj, n_rows // tm),
        in_specs=[pl.BlockSpec((tm, dh), lambda j, b, be, nb: (blk(b, nb), 0)),
                  pl.BlockSpec((None, d, tn), lambda j, b, be, nb: (be[blk(b, nb)], 0, j)),
                  pl.BlockSpec((None, d, tn), lambda j, b, be, nb: (be[blk(b, nb)], 0, nj + j)),
                  pl.BlockSpec((None, 1, tn), lambda j, b, be, nb: (be[blk(b, nb)], 0, j)),
                  pl.BlockSpec((None, 1, tn), lambda j, b, be, nb: (be[blk(b, nb)], 0, nj + j))],
        out_specs=pl.BlockSpec((tm, tn), lambda j, b, be, nb: (b, j)),
        scratch_shapes=[pltpu.VMEM((d, tn), BF16), pltpu.VMEM((d, tn), BF16)],
    )
    return pl.pallas_call(
        _moe1_kernel,
        grid_spec=grid_spec,
        out_shape=jax.ShapeDtypeStruct((n_rows, de), BF16),
        compiler_params=_params("arbitrary", "arbitrary"),
        name="moe_up",
    )(block_e, nb_used, xs, w_gu, w_gu, b_gu, b_gu)


def _moe2_kernel(be_ref, nb_ref, h_ref, w_ref, b_ref, y_ref, w_s):
    used, fresh = _expert_changed(be_ref, nb_ref)

    @pl.when(jnp.logical_not(used))
    def _():
        y_ref[...] = jnp.zeros_like(y_ref)

    @pl.when(jnp.logical_and(used, fresh))
    def _():
        w_s[...] = w_ref[...].astype(BF16)

    @pl.when(used)
    def _():
        y_ref[...] = _dot(h_ref[...], w_s[...]) + b_ref[...]


def _moe2(block_e, nb_used, h, w_down, b_down, tm, tn):
    n_rows, de = h.shape
    n_exp, _, d = w_down.shape
    tn = _pick(d, tn)

    def blk(b, nb):
        return jnp.minimum(b, nb[0] - 1)

    grid_spec = pltpu.PrefetchScalarGridSpec(
        num_scalar_prefetch=2,
        grid=(d // tn, n_rows // tm),
        in_specs=[pl.BlockSpec((tm, de), lambda j, b, be, nb: (blk(b, nb), 0)),
                  pl.BlockSpec((None, de, tn), lambda j, b, be, nb: (be[blk(b, nb)], 0, j)),
                  pl.BlockSpec((None, 1, tn), lambda j, b, be, nb: (be[blk(b, nb)], 0, j))],
        out_specs=pl.BlockSpec((tm, tn), lambda j, b, be, nb: (b, j)),
        scratch_shapes=[pltpu.VMEM((de, tn), BF16)],
    )
    return pl.pallas_call(
        _moe2_kernel,
        grid_spec=grid_spec,
        out_shape=jax.ShapeDtypeStruct((n_rows, d), F32),
        compiler_params=_params("arbitrary", "arbitrary"),
        name="moe_down",
    )(block_e, nb_used, h, w_down, b_down)


def _combine_kernel(dest_ref, ys_hbm, x1_ref, wgt_ref, g_ref, b_ref, o_ref, buf, sem, *, tc, top_k, alpha):
    def row_copy(t, j):
        return pltpu.make_async_copy(ys_hbm.at[pl.ds(dest_ref[t * top_k + j], 1)], buf.at[j, pl.ds(t, 1)], sem)

    def issue(t, c):
        for j in range(top_k):
            row_copy(t, j).start()
        return c

    def drain(t, c):
        for j in range(top_k):
            row_copy(t, j).wait()
        return c

    lax.fori_loop(0, tc, issue, 0)
    lax.fori_loop(0, tc, drain, 0)
    y = wgt_ref[:, 0:1] * buf[0]
    for j in range(1, top_k):
        y = y + wgt_ref[:, j:j + 1] * buf[j]
    o_ref[...] = _layer_norm(alpha * x1_ref[...] + y, g_ref[...], b_ref[...])


def _combine(dest_flat, ys, x1, wgt, g, b, alpha, tc):
    t, d = x1.shape
    tc = _pick(t, tc)
    return pl.pallas_call(
        functools.partial(_combine_kernel, tc=tc, top_k=TOP_K, alpha=alpha),
        grid=(t // tc,),
        in_specs=[pl.BlockSpec((tc * TOP_K,), lambda i: (i,), memory_space=pltpu.SMEM),
                  pl.BlockSpec(memory_space=pl.ANY),
                  pl.BlockSpec((tc, d), lambda i: (i, 0)),
                  pl.BlockSpec((tc, LANES), lambda i: (i, 0)),
                  pl.BlockSpec((1, d), lambda i: (0, 0)),
                  pl.BlockSpec((1, d), lambda i: (0, 0))],
        out_specs=pl.BlockSpec((tc, d), lambda i: (i, 0)),
        out_shape=jax.ShapeDtypeStruct((t, d), F32),
        scratch_shapes=[pltpu.VMEM((TOP_K, tc, d), F32), pltpu.SemaphoreType.DMA(())],
        compiler_params=_params("arbitrary"),
        name="combine",
    )(dest_flat, ys, x1, wgt, g, b)


def _layer(x2d, w_in, b_in, a_ws, a_bs, a_ln_g, a_ln_b, gla_w_lr, gla_b_lr, gla_gn_g, w_br_a, w_br_b, w_o,
           ln1_g, ln1_b, w_router, b_router, w_gu, b_gu, w_down, b_down, ln2_g, ln2_b, alpha):
    t, d = x2d.shape
    n_groups, a_chunk, _ = a_ws.shape
    aw = a_ln_g.size
    rank, kw = gla_w_lr.shape
    heads, dv = gla_gn_g.shape
    vw = heads * dv
    n_exp = w_router.shape[1]
    de = w_down.shape[1]
    off_lr = 2 * aw + 2 * kw + 2 * vw
    off_q = 2 * aw

    w_mix = w_in[:, :off_lr].astype(BF16)
    w_gate = w_in[:, off_lr + rank:].astype(BF16)
    w_lr_in = jnp.pad(w_in[:, off_lr:off_lr + rank], ((0, 0), (0, LANES - rank))).astype(BF16)
    b_lr_in = jnp.pad(b_in[off_lr:off_lr + rank], (0, LANES - rank)).reshape(1, LANES)
    wlr = jnp.pad(gla_w_lr, ((0, LANES - rank), (0, 0))).astype(BF16)
    wr = jnp.pad(w_router, ((0, 0), (0, LANES - n_exp))).astype(BF16)
    br = jnp.pad(b_router, (0, LANES - n_exp), constant_values=-jnp.inf).reshape(1, LANES)

    xb = x2d.astype(BF16)
    z = _matmul_bias(xb, w_mix, b_in[:off_lr].reshape(1, -1), F32, 1024, 1024)
    zg = _matmul_bias(xb, w_gate, b_in[off_lr + rank:].reshape(1, -1), F32, 1024, 1024)
    lr = _matmul_bias(xb, w_lr_in, b_lr_in, F32, 2048, LANES)

    ya = _branch_a(z, a_ws, a_bs, a_ln_g, a_ln_b, aw, 2 * a_chunk)
    yb = _gla(z, lr, wlr, gla_b_lr.reshape(1, kw), gla_gn_g.reshape(1, vw), kw, vw, off_q, heads, 256)
    merged = _merge(ya, w_br_a.astype(BF16), yb, w_br_b.astype(BF16), zg, 0, d, 1024, 512)
    h1 = _out_proj_resid(merged, w_o.astype(BF16), x2d, alpha, 1024, 512)

    x1, xp, idx, wgt, rnk, cnt = _ln_router(h1, ln1_g.reshape(1, d), ln1_b.reshape(1, d), wr, br, 256)

    tm_e = 512 if t * TOP_K >= 512 * n_exp else 64
    n_assign = t * TOP_K
    n_blocks = -(-(n_assign + n_exp * (tm_e - 1)) // tm_e)
    n_rows = n_blocks * tm_e
    counts = cnt[0, :n_exp].astype(I32)
    padded = (counts + tm_e - 1) // tm_e * tm_e
    pend = jnp.cumsum(padded)
    pstart = pend - padded
    sel = idx[:, :TOP_K, None] == jnp.arange(n_exp, dtype=I32)[None, None, :]
    dest = jnp.sum(jnp.where(sel, pstart[None, None, :], 0), axis=-1) + rnk[:, :TOP_K]
    dest_flat = dest.reshape(-1).astype(I32)
    nb_used = (pend[-1] // tm_e).astype(I32).reshape(1)
    block_start = jnp.arange(n_blocks, dtype=I32) * tm_e
    block_e = jnp.minimum(jnp.sum((pend[None, :] <= block_start[:, None]).astype(I32), axis=1), n_exp - 1)

    xs = _dispatch(dest_flat, (pstart + counts).astype(I32), (padded - counts).astype(I32), nb_used, xp, n_rows,
                   tm_e, 256)
    hmid = _moe1(block_e, nb_used, xs, w_gu, b_gu.reshape(n_exp, 1, 2 * de), tm_e, 512)
    ys = _moe2(block_e, nb_used, hmid, w_down, b_down.reshape(n_exp, 1, d), tm_e, 2048)
    return _combine(dest_flat, ys, x1, wgt, ln2_g.reshape(1, d), ln2_b.reshape(1, d), alpha, 128)


def kernel(x, w_in, b_in, a_ws, a_bs, a_ln_g, a_ln_b, gla_w_lr, gla_b_lr, gla_gn_g, w_br_a, w_br_b, w_o, ln1_g, ln1_b, w_router, b_router, w_gu, b_gu, w_down, b_down, ln2_g, ln2_b):
    bsz, seq, d = x.shape
    depth = w_in.shape[0]
    alpha = (2 * depth) ** 0.25
    outs = []
    for bi in range(bsz):
        h = x[bi]
        for l in range(depth):
            h = _layer(h, w_in[l], b_in[l], a_ws[l], a_bs[l], a_ln_g[l], a_ln_b[l], gla_w_lr[l], gla_b_lr[l],
                       gla_gn_g[l], w_br_a[l], w_br_b[l], w_o[l], ln1_g[l], ln1_b[l], w_router[l], b_router[l],
                       w_gu[l], b_gu[l], w_down[l], b_down[l], ln2_g[l], ln2_b[l], alpha)
        outs.append(h)
    return jnp.stack(outs) if bsz > 1 else outs[0][None]
```

```python
import functools

import jax
import jax.numpy as jnp
from jax import lax
from jax.experimental import pallas as pl
from jax.experimental.pallas import tpu as pltpu

F32 = jnp.float32
BF16 = jnp.bfloat16
U32 = jnp.uint32
I32 = jnp.int32

LN_EPS = 1e-5
GLA_CHUNK = 64
GLA_GATE_TAU = 16.0
TOP_K = 4
SWIGLU_LIMIT = 7.0
SWIGLU_ALPHA = 1.702
LANES = 128
VMEM_LIMIT = 56 * 1024 * 1024
HI_MASK = 0xFFFF0000


def _pick(n, pref):
    t = min(pref, n)
    while n % t:
        t //= 2
    return t


def _dot(a, b):
    return jnp.dot(a, b, preferred_element_type=F32)


def _dot_t0(a, b):
    return lax.dot_general(a, b, (((0,), (0,)), ((), ())), preferred_element_type=F32)


def _dot_t1(a, b):
    return lax.dot_general(a, b, (((1,), (1,)), ((), ())), preferred_element_type=F32)


def _sigmoid(x):
    return 1.0 / (1.0 + jnp.exp(-x))


def _gelu(x):
    return 0.5 * x * (1.0 + lax.erf(x * (2.0 ** -0.5)))


def _layer_norm(x, g, b):
    mu = jnp.mean(x, axis=-1, keepdims=True)
    xc = x - mu
    var = jnp.mean(xc * xc, axis=-1, keepdims=True)
    return xc * lax.rsqrt(var + LN_EPS) * g + b


def _params(*sem):
    return pltpu.CompilerParams(dimension_semantics=sem, vmem_limit_bytes=VMEM_LIMIT)


def _mm_bias_kernel(x_ref, w_ref, b_ref, o_ref):
    o_ref[...] = (_dot(x_ref[...], w_ref[...]) + b_ref[...]).astype(o_ref.dtype)


def _matmul_bias(xb, w, b, out_dtype, tm, tn):
    m, k = xb.shape
    n = w.shape[1]
    tm, tn = _pick(m, tm), _pick(n, tn)
    return pl.pallas_call(
        _mm_bias_kernel,
        grid=(m // tm, n // tn),
        in_specs=[pl.BlockSpec((tm, k), lambda i, j: (i, 0)),
                  pl.BlockSpec((k, tn), lambda i, j: (0, j)),
                  pl.BlockSpec((1, tn), lambda i, j: (0, j))],
        out_specs=pl.BlockSpec((tm, tn), lambda i, j: (i, j)),
        out_shape=jax.ShapeDtypeStruct((m, n), out_dtype),
        compiler_params=_params("parallel", "parallel"),
        name="in_proj",
    )(xb, w, b)


def _merge_kernel(ya_ref, wa_ref, yb_ref, wb_ref, ga_ref, gb_ref, o_ref):
    a = _dot(ya_ref[...], wa_ref[...])
    b = _dot(yb_ref[...], wb_ref[...])
    o_ref[...] = (_sigmoid(ga_ref[...]) * a + _sigmoid(gb_ref[...]) * b).astype(o_ref.dtype)


def _merge(ya, wa, yb, wb, z, off_ga, off_gb, tm, tn):
    m, ka = ya.shape
    kb = yb.shape[1]
    n = wa.shape[1]
    tm, tn = _pick(m, tm), _pick(n, tn)
    while off_ga % tn or off_gb % tn:
        tn //= 2
    oa, ob = off_ga // tn, off_gb // tn
    return pl.pallas_call(
        _merge_kernel,
        grid=(m // tm, n // tn),
        in_specs=[pl.BlockSpec((tm, ka), lambda i, j: (i, 0)),
                  pl.BlockSpec((ka, tn), lambda i, j: (0, j)),
                  pl.BlockSpec((tm, kb), lambda i, j: (i, 0)),
                  pl.BlockSpec((kb, tn), lambda i, j: (0, j)),
                  pl.BlockSpec((tm, tn), lambda i, j: (i, oa + j)),
                  pl.BlockSpec((tm, tn), lambda i, j: (i, ob + j))],
        out_specs=pl.BlockSpec((tm, tn), lambda i, j: (i, j)),
        out_shape=jax.ShapeDtypeStruct((m, n), BF16),
        compiler_params=_params("parallel", "parallel"),
        name="merge",
    )(ya, wa, yb, wb, z, z)


def _mm_resid_kernel(m_ref, w_ref, x_ref, o_ref, *, alpha):
    o_ref[...] = alpha * x_ref[...] + _dot(m_ref[...], w_ref[...])


def _out_proj_resid(mb, w, x, alpha, tm, tn):
    m, k = mb.shape
    n = w.shape[1]
    tm, tn = _pick(m, tm), _pick(n, tn)
    return pl.pallas_call(
        functools.partial(_mm_resid_kernel, alpha=alpha),
        grid=(m // tm, n // tn),
        in_specs=[pl.BlockSpec((tm, k), lambda i, j: (i, 0)),
                  pl.BlockSpec((k, tn), lambda i, j: (0, j)),
                  pl.BlockSpec((tm, tn), lambda i, j: (i, j))],
        out_specs=pl.BlockSpec((tm, tn), lambda i, j: (i, j)),
        out_shape=jax.ShapeDtypeStruct((m, n), F32),
        compiler_params=_params("parallel", "parallel"),
        name="out_proj",
    )(mb, w, x)


def _branch_a_kernel(u_ref, v_ref, ws_ref, bs_ref, g_ref, b_ref, o_ref, *, n_groups, chunk):
    tm, aw = u_ref.shape
    gd = aw // n_groups
    row = lax.broadcasted_iota(I32, (chunk, chunk), 0)
    col = lax.broadcasted_iota(I32, (chunk, chunk), 1)
    causal = row >= col
    for g in range(n_groups):
        ws = jnp.where(causal, ws_ref[g], 0.0).astype(BF16)
        cs = slice(g * gd, (g + 1) * gd)
        for c in range(tm // chunk):
            rs = slice(c * chunk, (c + 1) * chunk)
            vn = _layer_norm(_gelu(v_ref[rs, cs]), g_ref[:, cs], b_ref[:, cs])
            mixed = _dot(ws, vn.astype(BF16)) + bs_ref[:, g:g + 1]
            o_ref[rs, cs] = (_gelu(u_ref[rs, cs]) * mixed).astype(o_ref.dtype)


def _branch_a(z, a_ws, a_bs, a_ln_g, a_ln_b, aw, tm):
    t = z.shape[0]
    n_groups, chunk, _ = a_ws.shape
    tm = max(_pick(t, tm), chunk)
    return pl.pallas_call(
        functools.partial(_branch_a_kernel, n_groups=n_groups, chunk=chunk),
        grid=(t // tm,),
        in_specs=[pl.BlockSpec((tm, aw), lambda i: (i, 0)),
                  pl.BlockSpec((tm, aw), lambda i: (i, 1)),
                  pl.BlockSpec((n_groups, chunk, chunk), lambda i: (0, 0, 0)),
                  pl.BlockSpec((chunk, n_groups), lambda i: (0, 0)),
                  pl.BlockSpec((1, aw), lambda i: (0, 0)),
                  pl.BlockSpec((1, aw), lambda i: (0, 0))],
        out_specs=pl.BlockSpec((tm, aw), lambda i: (i, 0)),
        out_shape=jax.ShapeDtypeStruct((t, aw), BF16),
        compiler_params=_params("parallel"),
        name="branch_a",
    )(z, z, a_ws, a_bs.T, a_ln_g.reshape(1, aw), a_ln_b.reshape(1, aw))


def _gla_kernel(q_ref, k_ref, v_ref, r_ref, lr_ref, wlr_ref, blr_ref, gn_ref, o_ref, s_ref, *, heads, chunk):
    @pl.when(pl.program_id(0) == 0)
    def _():
        s_ref[...] = jnp.zeros_like(s_ref)

    tg, kw = q_ref.shape
    vw = v_ref.shape[1]
    dk, dv = kw // heads, vw // heads
    row = lax.broadcasted_iota(I32, (chunk, chunk), 0)
    col = lax.broadcasted_iota(I32, (chunk, chunk), 1)
    causal = row >= col
    tri = causal.astype(BF16)
    ones = jnp.ones((chunk, LANES), BF16)
    scale = dk ** -0.5

    def body(c, carry):
        rs = pl.ds(pl.multiple_of(c * chunk, chunk), chunk)
        gl = _dot(lr_ref[rs, :].astype(BF16), wlr_ref[...]) + blr_ref[...]
        la = (jnp.minimum(gl, 0.0) - jnp.log1p(jnp.exp(-jnp.abs(gl)))) * (1.0 / GLA_GATE_TAU)
        la_hi = la.astype(BF16)
        la_lo = (la - la_hi.astype(F32)).astype(BF16)
        cum = _dot(tri, la_hi) + _dot(tri, la_lo)
        tot = cum[chunk - 1:chunk, :]
        tot_col = _dot_t0(la_hi, ones) + _dot_t0(la_lo, ones)
        dec_col = jnp.exp(tot_col)
        q = q_ref[rs, :] * scale
        k = k_ref[rs, :]
        qt = (q * jnp.exp(cum)).astype(BF16)
        kt = (k * jnp.exp(-cum)).astype(BF16)
        kl = (k * jnp.exp(tot - cum)).astype(BF16)
        for h in range(heads):
            ks = slice(h * dk, (h + 1) * dk)
            vs = slice(h * dv, (h + 1) * dv)
            vh = v_ref[rs, vs].astype(BF16)
            sc = jnp.where(causal, _dot_t1(qt[:, ks], kt[:, ks]), 0.0).astype(BF16)
            state = s_ref[h]
            o = _dot(sc, vh) + _dot(qt[:, ks], state.astype(BF16))
            dec = jnp.concatenate([dec_col[ks, :]] * (dv // LANES), axis=1) if dv >= LANES else dec_col[ks, :dv]
            s_ref[h] = dec * state + _dot_t0(kl[:, ks], vh)
            on = o * lax.rsqrt(jnp.mean(o * o, axis=-1, keepdims=True) + LN_EPS) * gn_ref[:, vs]
            rr = r_ref[rs, vs]
            o_ref[rs, vs] = (on * (rr * _sigmoid(rr))).astype(o_ref.dtype)
        return carry

    lax.fori_loop(0, tg // chunk, body, 0)


def _gla(z, lr, wlr, blr, gn, kw, vw, off_q, heads, tg):
    t = z.shape[0]
    tg = max(_pick(t, tg), GLA_CHUNK)
    oq = off_q // kw
    ok = oq + 1
    ov = (off_q + 2 * kw) // vw
    orr = ov + 1
    return pl.pallas_call(
        functools.partial(_gla_kernel, heads=heads, chunk=GLA_CHUNK),
        grid=(t // tg,),
        in_specs=[pl.BlockSpec((tg, kw), lambda i: (i, oq)),
                  pl.BlockSpec((tg, kw), lambda i: (i, ok)),
                  pl.BlockSpec((tg, vw), lambda i: (i, ov)),
                  pl.BlockSpec((tg, vw), lambda i: (i, orr)),
                  pl.BlockSpec((tg, LANES), lambda i: (i, 0)),
                  pl.BlockSpec((LANES, kw), lambda i: (0, 0)),
                  pl.BlockSpec((1, kw), lambda i: (0, 0)),
                  pl.BlockSpec((1, vw), lambda i: (0, 0))],
        out_specs=pl.BlockSpec((tg, vw), lambda i: (i, 0)),
        out_shape=jax.ShapeDtypeStruct((t, vw), BF16),
        scratch_shapes=[pltpu.VMEM((heads, kw // heads, vw // heads), F32)],
        compiler_params=_params("arbitrary"),
        name="gla",
    )(z, z, z, z, lr, wlr, blr, gn)


def _ln_router_kernel(h_ref, g_ref, b_ref, wr_ref, br_ref, x1_ref, xp_ref, idx_ref, wgt_ref, rank_ref, cnt_ref,
                      base_ref, *, top_k):
    @pl.when(pl.program_id(0) == 0)
    def _():
        base_ref[...] = jnp.zeros_like(base_ref)

    tm, d = h_ref.shape
    half = d // 2
    x1 = _layer_norm(h_ref[...], g_ref[...], b_ref[...])
    x1_ref[...] = x1
    lo = pltpu.bitcast(x1[:, :half].astype(BF16).astype(F32), U32) >> 16
    hi = pltpu.bitcast(x1[:, half:].astype(BF16).astype(F32), U32) & jnp.uint32(HI_MASK)
    xp_ref[...] = lo | hi

    logits = _dot(x1.astype(BF16), wr_ref[...]) + br_ref[...]
    lane = lax.broadcasted_iota(I32, (tm, LANES), 1).astype(F32)
    member = jnp.zeros((tm, LANES), F32)
    vals, idxs = [], []
    cur = logits
    for _ in range(top_k):
        mx = jnp.max(cur, axis=-1, keepdims=True)
        ix = jnp.min(jnp.where(cur == mx, lane, float(LANES)), axis=-1, keepdims=True)
        sel = lane == ix
        vals.append(mx)
        idxs.append(ix)
        member = member + sel.astype(F32)
        cur = jnp.where(sel, -jnp.inf, cur)
    exps = [jnp.exp(v - vals[0]) for v in vals]
    denom = exps[0]
    for e in exps[1:]:
        denom = denom + e
    r_i = lax.broadcasted_iota(I32, (tm, tm), 0)
    c_i = lax.broadcasted_iota(I32, (tm, tm), 1)
    before = _dot((r_i > c_i).astype(BF16), member.astype(BF16)) + base_ref[...]
    idx_out = jnp.zeros((tm, LANES), F32)
    wgt_out = jnp.zeros((tm, LANES), F32)
    rank_out = jnp.zeros((tm, LANES), F32)
    for j in range(top_k):
        rk = jnp.sum(jnp.where(lane == idxs[j], before, 0.0), axis=-1, keepdims=True)
        at = lane == float(j)
        idx_out = jnp.where(at, idxs[j], idx_out)
        wgt_out = jnp.where(at, exps[j] / denom, wgt_out)
        rank_out = jnp.where(at, rk, rank_out)
    idx_ref[...] = idx_out.astype(I32)
    wgt_ref[...] = wgt_out
    rank_ref[...] = rank_out.astype(I32)
    base_ref[...] = base_ref[...] + jnp.sum(member, axis=0, keepdims=True)
    cnt_ref[...] = base_ref[...]


def _ln_router(h1, g, b, wr, br, tm):
    t, d = h1.shape
    tm = _pick(t, tm)
    row = lambda i: (i, 0)
    fixed = lambda i: (0, 0)
    return pl.pallas_call(
        functools.partial(_ln_router_kernel, top_k=TOP_K),
        grid=(t // tm,),
        in_specs=[pl.BlockSpec((tm, d), row),
                  pl.BlockSpec((1, d), fixed),
                  pl.BlockSpec((1, d), fixed),
                  pl.BlockSpec((d, LANES), fixed),
                  pl.BlockSpec((1, LANES), fixed)],
        out_specs=[pl.BlockSpec((tm, d), row),
                   pl.BlockSpec((tm, d // 2), row),
                   pl.BlockSpec((tm, LANES), row),
                   pl.BlockSpec((tm, LANES), row),
                   pl.BlockSpec((tm, LANES), row),
                   pl.BlockSpec((1, LANES), fixed)],
        out_shape=[jax.ShapeDtypeStruct((t, d), F32),
                   jax.ShapeDtypeStruct((t, d // 2), U32),
                   jax.ShapeDtypeStruct((t, LANES), I32),
                   jax.ShapeDtypeStruct((t, LANES), F32),
                   jax.ShapeDtypeStruct((t, LANES), I32),
                   jax.ShapeDtypeStruct((1, LANES), F32)],
        scratch_shapes=[pltpu.VMEM((1, LANES), F32)],
        compiler_params=_params("arbitrary"),
        name="ln_router",
    )(h1, g, b, wr, br)


def _dispatch_kernel(nb_ref, tok_ref, xp_hbm, xs_ref, buf, sem, *, tm_e):
    used = pl.program_id(0) < nb_ref[0]

    @pl.when(jnp.logical_not(used))
    def _():
        xs_ref[...] = jnp.zeros_like(xs_ref)

    @pl.when(used)
    def _():
        def row_copy(r):
            return pltpu.make_async_copy(xp_hbm.at[pl.ds(tok_ref[r], 1)], buf.at[pl.ds(r, 1)], sem)

        def issue(r, c):
            row_copy(r).start()
            return c

        def drain(r, c):
            row_copy(r).wait()
            return c

        lax.fori_loop(0, tm_e, issue, 0)
        lax.fori_loop(0, tm_e, drain, 0)
        w = buf[...]
        half = w.shape[1]
        xs_ref[:, :half] = pltpu.bitcast(w << 16, F32).astype(BF16)
        xs_ref[:, half:] = pltpu.bitcast(w & jnp.uint32(HI_MASK), F32).astype(BF16)


def _dispatch(tok_buf, nb_used, xp, tm_e):
    dh = xp.shape[1]
    n_rows = tok_buf.shape[0]
    grid_spec = pltpu.PrefetchScalarGridSpec(
        num_scalar_prefetch=1,
        grid=(n_rows // tm_e,),
        in_specs=[pl.BlockSpec((tm_e,), lambda b, nb: (b,), memory_space=pltpu.SMEM),
                  pl.BlockSpec(memory_space=pl.ANY)],
        out_specs=pl.BlockSpec((tm_e, 2 * dh), lambda b, nb: (b, 0)),
        scratch_shapes=[pltpu.VMEM((tm_e, dh), U32), pltpu.SemaphoreType.DMA(())],
    )
    return pl.pallas_call(
        functools.partial(_dispatch_kernel, tm_e=tm_e),
        grid_spec=grid_spec,
        out_shape=jax.ShapeDtypeStruct((n_rows, 2 * dh), BF16),
        compiler_params=_params("arbitrary"),
        name="dispatch",
    )(nb_used, tok_buf, xp)


def _expert_changed(be_ref, nb_ref):
    b = pl.program_id(1)
    fresh = jnp.logical_or(b == 0, be_ref[b] != be_ref[jnp.maximum(b - 1, 0)])
    return b < nb_ref[0], fresh


def _moe1_kernel(be_ref, nb_ref, xs_ref, wg_ref, wl_ref, bg_ref, bl_ref, h_ref, wg_s, wl_s):
    used, fresh = _expert_changed(be_ref, nb_ref)

    @pl.when(jnp.logical_not(used))
    def _():
        h_ref[...] = jnp.zeros_like(h_ref)

    @pl.when(jnp.logical_and(used, fresh))
    def _():
        wg_s[...] = wg_ref[...].astype(BF16)
        wl_s[...] = wl_ref[...].astype(BF16)

    @pl.when(used)
    def _():
        x = xs_ref[...]
        gate =jnp.minimum(_dot(x, wg_s[...]) + bg_ref[...], SWIGLU_LIMIT)
        lin = jnp.clip(_dot(x, wl_s[...]) + bl_ref[...], -SWIGLU_LIMIT, SWIGLU_LIMIT)
        h_ref[...] = (gate * _sigmoid(SWIGLU_ALPHA * gate) * (lin + 1.0)).astype(h_ref.dtype)


def _moe1(block_e, nb_used, xs, w_gu, b_gu, tm, tn):
    n_rows, dh = xs.shape
    n_exp, d, de2 = w_gu.shape
    de = de2 // 2
    tn = _pick(de, tn)
    nj = de // tn

    def blk(b, nb):
        return jnp.minimum(b, nb[0] - 1)

    grid_spec = pltpu.PrefetchScalarGridSpec(
        num_scalar_prefetch=2,
        grid=(nj, n_rows // tm),
        in_specs=[pl.BlockSpec((tm, dh), lambda j, b, be, nb: (blk(b, nb), 0)),
                  pl.BlockSpec((None, d, tn), lambda j, b, be, nb: (be[blk(b, nb)], 0, j)),
                  pl.BlockSpec((None, d, tn), lambda j, b, be, nb: (be[blk(b, nb)], 0, nj + j)),
                  pl.BlockSpec((None, 1, tn), lambda j, b, be, nb: (be[blk(b, nb)], 0, j)),
                  pl.BlockSpec((None, 1, tn), lambda j, b, be, nb: (be[blk(b, nb)], 0, nj + j))],
        out_specs=pl.BlockSpec((tm, tn), lambda j, b, be, nb: (b, j)),
        scratch_shapes=[pltpu.VMEM((d, tn), BF16), pltpu.VMEM((d, tn), BF16)],
    )
    return pl.pallas_call(
        _moe1_kernel,
        grid_spec=grid_spec,
        out_shape=jax.ShapeDtypeStruct((n_rows, de), BF16),
        compiler_params=_params("arbitrary", "arbitrary"),
        name="moe_up",
    )(block_e, nb_used, xs, w_gu, w_gu, b_gu, b_gu)


def _moe2_kernel(be_ref, nb_ref, h_ref, w_ref, b_ref, y_ref, w_s):
    used, fresh = _expert_changed(be_ref, nb_ref)

    @pl.when(jnp.logical_not(used))
    def _():
        y_ref[...] = jnp.zeros_like(y_ref)

    @pl.when(jnp.logical_and(used, fresh))
    def _():
        w_s[...] = w_ref[...].astype(BF16)

    @pl.when(used)
    def _():
        y_ref[...] = _dot(h_ref[...], w_s[...]) + b_ref[...]


def _moe2(block_e, nb_used, h, w_down, b_down, tm, tn):
    n_rows, de = h.shape
    n_exp, _, d = w_down.shape
    tn = _pick(d, tn)

    def blk(b, nb):
        return jnp.minimum(b, nb[0] - 1)

    grid_spec = pltpu.PrefetchScalarGridSpec(
        num_scalar_prefetch=2,
        grid=(d // tn, n_rows // tm),
        in_specs=[pl.BlockSpec((tm, de), lambda j, b, be, nb: (blk(b, nb), 0)),
                  pl.BlockSpec((None, de, tn), lambda j, b, be, nb: (be[blk(b, nb)], 0, j)),
                  pl.BlockSpec((None, 1, tn), lambda j, b, be, nb: (be[blk(b, nb)], 0, j))],
        out_specs=pl.BlockSpec((tm, tn), lambda j, b, be, nb: (b, j)),
        scratch_shapes=[pltpu.VMEM((de, tn), BF16)],
    )
    return pl.pallas_call(
        _moe2_kernel,
        grid_spec=grid_spec,
        out_shape=jax.ShapeDtypeStruct((n_rows, d), F32),
        compiler_params=_params("arbitrary", "arbitrary"),
        name="moe_down",
    )(block_e, nb_used, h, w_down, b_down)


def _combine_kernel(dest_ref, ys_hbm, x1_ref, wgt_ref, g_ref, b_ref, o_ref, buf, sem, *, tc, top_k, alpha):
    def row_copy(t, j):
        return pltpu.make_async_copy(ys_hbm.at[pl.ds(dest_ref[t * top_k + j], 1)], buf.at[j, pl.ds(t, 1)], sem)

    def issue(t, c):
        for j in range(top_k):
            row_copy(t, j).start()
        return c

    def drain(t, c):
        for j in range(top_k):
            row_copy(t, j).wait()
        return c

    lax.fori_loop(0, tc, issue, 0)
    lax.fori_loop(0, tc, drain, 0)
    y = wgt_ref[:, 0:1] * buf[0]
    for j in range(1, top_k):
        y = y + wgt_ref[:, j:j + 1] * buf[j]
    o_ref[...] = _layer_norm(alpha * x1_ref[...] + y, g_ref[...], b_ref[...])


def _combine(dest_flat, ys, x1, wgt, g, b, alpha, tc):
    t, d = x1.shape
    tc = _pick(t, tc)
    return pl.pallas_call(
        functools.partial(_combine_kernel, tc=tc, top_k=TOP_K, alpha=alpha),
        grid=(t // tc,),
        in_specs=[pl.BlockSpec((tc * TOP_K,), lambda i: (i,), memory_space=pltpu.SMEM),
                  pl.BlockSpec(memory_space=pl.ANY),
                  pl.BlockSpec((tc, d), lambda i: (i, 0)),
                  pl.BlockSpec((tc, LANES), lambda i: (i, 0)),
                  pl.BlockSpec((1, d), lambda i: (0, 0)),
                  pl.BlockSpec((1, d), lambda i: (0, 0))],
        out_specs=pl.BlockSpec((tc, d), lambda i: (i, 0)),
        out_shape=jax.ShapeDtypeStruct((t, d), F32),
        scratch_shapes=[pltpu.VMEM((TOP_K, tc, d), F32), pltpu.SemaphoreType.DMA(())],
        compiler_params=_params("arbitrary"),
        name="combine",
    )(dest_flat, ys, x1, wgt, g, b)


def _layer(x2d, w_in, b_in, a_ws, a_bs, a_ln_g, a_ln_b, gla_w_lr, gla_b_lr, gla_gn_g, w_br_a, w_br_b, w_o,
           ln1_g, ln1_b, w_router, b_router, w_gu, b_gu, w_down, b_down, ln2_g, ln2_b, alpha):
    t, d = x2d.shape
    n_groups, a_chunk, _ = a_ws.shape
    aw = a_ln_g.size
    rank, kw = gla_w_lr.shape
    heads, dv = gla_gn_g.shape
    vw = heads * dv
    n_exp = w_router.shape[1]
    de = w_down.shape[1]
    off_lr = 2 * aw + 2 * kw + 2 * vw
    off_q = 2 * aw

    w_mix = w_in[:, :off_lr].astype(BF16)
    w_gate = w_in[:, off_lr + rank:].astype(BF16)
    w_lr_in = jnp.pad(w_in[:, off_lr:off_lr + rank], ((0, 0), (0, LANES - rank))).astype(BF16)
    b_lr_in = jnp.pad(b_in[off_lr:off_lr + rank], (0, LANES - rank)).reshape(1, LANES)
    wlr = jnp.pad(gla_w_lr, ((0, LANES - rank), (0, 0))).astype(BF16)
    wr = jnp.pad(w_router, ((0, 0), (0, LANES - n_exp))).astype(BF16)
    br = jnp.pad(b_router, (0, LANES - n_exp), constant_values=-jnp.inf).reshape(1, LANES)

    xb = x2d.astype(BF16)
    z = _matmul_bias(xb, w_mix, b_in[:off_lr].reshape(1, -1), F32, 1024, 1024)
    zg = _matmul_bias(xb, w_gate, b_in[off_lr + rank:].reshape(1, -1), F32, 1024, 1024)
    lr = _matmul_bias(xb, w_lr_in, b_lr_in, F32, 2048, LANES)

    ya = _branch_a(z, a_ws, a_bs, a_ln_g, a_ln_b, aw, 2 * a_chunk)
    yb = _gla(z, lr, wlr, gla_b_lr.reshape(1, kw), gla_gn_g.reshape(1, vw), kw, vw, off_q, heads, 256)
    merged = _merge(ya, w_br_a.astype(BF16), yb, w_br_b.astype(BF16), zg, 0, d, 1024, 512)
    h1 = _out_proj_resid(merged, w_o.astype(BF16), x2d, alpha, 1024, 512)

    x1, xp, idx, wgt, rnk, cnt = _ln_router(h1, ln1_g.reshape(1, d), ln1_b.reshape(1, d), wr, br, 256)

    tm_e = 512 if t * TOP_K >= 512 * n_exp else 64
    n_assign = t * TOP_K
    n_blocks = -(-(n_assign + n_exp * (tm_e - 1)) // tm_e)
    n_rows = n_blocks * tm_e
    counts = cnt[0, :n_exp].astype(I32)
    padded = (counts + tm_e - 1) // tm_e * tm_e
    pend = jnp.cumsum(padded)
    pstart = pend - padded
    sel = idx[:, :TOP_K, None] == jnp.arange(n_exp, dtype=I32)[None, None, :]
    dest = jnp.sum(jnp.where(sel, pstart[None, None, :], 0), axis=-1) + rnk[:, :TOP_K]
    dest_flat = dest.reshape(-1).astype(I32)
    nb_used = (pend[-1] // tm_e).astype(I32).reshape(1)
    block_start = jnp.arange(n_blocks, dtype=I32) * tm_e
    block_e = jnp.minimum(jnp.sum((pend[None, :] <= block_start[:, None]).astype(I32), axis=1), n_exp - 1)

    tok_buf = jnp.zeros((n_rows,), I32).at[dest_flat].set(jnp.arange(n_assign, dtype=I32) // TOP_K,
                                                          unique_indices=True)
    xs = _dispatch(tok_buf, nb_used, xp, tm_e)
    hmid = _moe1(block_e, nb_used, xs, w_gu, b_gu.reshape(n_exp, 1, 2 * de), tm_e, 512)
    ys = _moe2(block_e, nb_used, hmid, w_down, b_down.reshape(n_exp, 1, d), tm_e, 2048)
    return _combine(dest_flat, ys, x1, wgt, ln2_g.reshape(1, d), ln2_b.reshape(1, d), alpha, 128)


def kernel(x, w_in, b_in, a_ws, a_bs, a_ln_g, a_ln_b, gla_w_lr, gla_b_lr, gla_gn_g, w_br_a, w_br_b, w_o, ln1_g, ln1_b, w_router, b_router, w_gu, b_gu, w_down, b_down, ln2_g, ln2_b):
    bsz, seq, d = x.shape
    depth = w_in.shape[0]
    alpha = (2 * depth) ** 0.25
    outs = []
    for bi in range(bsz):
        h = x[bi]
        for l in range(depth):
            h = _layer(h, w_in[l], b_in[l], a_ws[l], a_bs[l], a_ln_g[l], a_ln_b[l], gla_w_lr[l], gla_b_lr[l],
                       gla_gn_g[l], w_br_a[l], w_br_b[l], w_o[l], ln1_g[l], ln1_b[l], w_router[l], b_router[l],
                       w_gu[l], b_gu[l], w_down[l], b_down[l], ln2_g[l], ln2_b[l], alpha)
        outs.append(h)
    return jnp.stack(outs) if bsz > 1 else outs[0][None]
```

```python
import functools

import jax
import jax.numpy as jnp
from jax import lax
from jax.experimental import pallas as pl
from jax.experimental.pallas import tpu as pltpu

F32 = jnp.float32
BF16 = jnp.bfloat16
U32 = jnp.uint32
I32 = jnp.int32

LN_EPS = 1e-5
GLA_CHUNK = 64
GLA_GATE_TAU = 16.0
TOP_K = 4
SWIGLU_LIMIT = 7.0
SWIGLU_ALPHA = 1.702
LANES = 128
VMEM_LIMIT = 56 * 1024 * 1024
HI_MASK = 0xFFFF0000


def _pick(n, pref):
    t = min(pref, n)
    while n % t:
        t //= 2
    return t


def _dot(a, b):
    return jnp.dot(a, b, preferred_element_type=F32)


def _dot_t0(a, b):
    return lax.dot_general(a, b, (((0,), (0,)), ((), ())), preferred_element_type=F32)


def _dot_t1(a, b):
    return lax.dot_general(a, b, (((1,), (1,)), ((), ())), preferred_element_type=F32)


def _sigmoid(x):
    return 1.0 / (1.0 + jnp.exp(-x))


def _gelu(x):
    return 0.5 * x * (1.0 + lax.erf(x * (2.0 ** -0.5)))


def _layer_norm(x, g, b):
    mu = jnp.mean(x, axis=-1, keepdims=True)
    xc = x - mu
    var = jnp.mean(xc * xc, axis=-1, keepdims=True)
    return xc * lax.rsqrt(var + LN_EPS) * g + b


def _params(*sem):
    return pltpu.CompilerParams(dimension_semantics=sem, vmem_limit_bytes=VMEM_LIMIT)


def _mm_bias_kernel(x_ref, w_ref, b_ref, o_ref):
    o_ref[...] = (_dot(x_ref[...], w_ref[...]) + b_ref[...]).astype(o_ref.dtype)


def _matmul_bias(xb, w, b, out_dtype, tm, tn):
    m, k = xb.shape
    n = w.shape[1]
    tm, tn = _pick(m, tm), _pick(n, tn)
    return pl.pallas_call(
        _mm_bias_kernel,
        grid=(m // tm, n // tn),
        in_specs=[pl.BlockSpec((tm, k), lambda i, j: (i, 0)),
                  pl.BlockSpec((k, tn), lambda i, j: (0, j)),
                  pl.BlockSpec((1, tn), lambda i, j: (0, j))],
        out_specs=pl.BlockSpec((tm, tn), lambda i, j: (i, j)),
        out_shape=jax.ShapeDtypeStruct((m, n), out_dtype),
        compiler_params=_params("parallel", "parallel"),
        name="in_proj",
    )(xb, w, b)


def _merge_kernel(ya_ref, wa_ref, yb_ref, wb_ref, ga_ref, gb_ref, o_ref):
    a = _dot(ya_ref[...], wa_ref[...])
    b = _dot(yb_ref[...], wb_ref[...])
    o_ref[...] = (_sigmoid(ga_ref[...]) * a + _sigmoid(gb_ref[...]) * b).astype(o_ref.dtype)


def _merge(ya, wa, yb, wb, z, off_ga, off_gb, tm, tn):
    m, ka = ya.shape
    kb = yb.shape[1]
    n = wa.shape[1]
    tm, tn = _pick(m, tm), _pick(n, tn)
    while off_ga % tn or off_gb % tn:
        tn //= 2
    oa, ob = off_ga // tn, off_gb // tn
    return pl.pallas_call(
        _merge_kernel,
        grid=(m // tm, n // tn),
        in_specs=[pl.BlockSpec((tm, ka), lambda i, j: (i, 0)),
                  pl.BlockSpec((ka, tn), lambda i, j: (0, j)),
                  pl.BlockSpec((tm, kb), lambda i, j: (i, 0)),
                  pl.BlockSpec((kb, tn), lambda i, j: (0, j)),
                  pl.BlockSpec((tm, tn), lambda i, j: (i, oa + j)),
                  pl.BlockSpec((tm, tn), lambda i, j: (i, ob + j))],
        out_specs=pl.BlockSpec((tm, tn), lambda i, j: (i, j)),
        out_shape=jax.ShapeDtypeStruct((m, n), BF16),
        compiler_params=_params("parallel", "parallel"),
        name="merge",
    )(ya, wa, yb, wb, z, z)


def _mm_resid_kernel(m_ref, w_ref, x_ref, o_ref, *, alpha):
    o_ref[...] = alpha * x_ref[...] + _dot(m_ref[...], w_ref[...])


def _out_proj_resid(mb, w, x, alpha, tm, tn):
    m, k = mb.shape
    n = w.shape[1]
    tm, tn = _pick(m, tm), _pick(n, tn)
    return pl.pallas_call(
        functools.partial(_mm_resid_kernel, alpha=alpha),
        grid=(m // tm, n // tn),
        in_specs=[pl.BlockSpec((tm, k), lambda i, j: (i, 0)),
                  pl.BlockSpec((k, tn), lambda i, j: (0, j)),
                  pl.BlockSpec((tm, tn), lambda i, j: (i, j))],
        out_specs=pl.BlockSpec((tm, tn), lambda i, j: (i, j)),
        out_shape=jax.ShapeDtypeStruct((m, n), F32),
        compiler_params=_params("parallel", "parallel"),
        name="out_proj",
    )(mb, w, x)


def _branch_a_kernel(u_ref, v_ref, ws_ref, bs_ref, g_ref, b_ref, o_ref, *, n_groups, chunk):
    tm, aw = u_ref.shape
    gd = aw // n_groups
    row = lax.broadcasted_iota(I32, (chunk, chunk), 0)
    col = lax.broadcasted_iota(I32, (chunk, chunk), 1)
    causal = row >= col
    for g in range(n_groups):
        ws = jnp.where(causal, ws_ref[g], 0.0).astype(BF16)
        cs = slice(g * gd, (g + 1) * gd)
        for c in range(tm // chunk):
            rs = slice(c * chunk, (c + 1) * chunk)
            vn = _layer_norm(_gelu(v_ref[rs, cs]), g_ref[:, cs], b_ref[:, cs])
            mixed = _dot(ws, vn.astype(BF16)) + bs_ref[:, g:g + 1]
            o_ref[rs, cs] = (_gelu(u_ref[rs, cs]) * mixed).astype(o_ref.dtype)


def _branch_a(z, a_ws, a_bs, a_ln_g, a_ln_b, aw, tm):
    t = z.shape[0]
    n_groups, chunk, _ = a_ws.shape
    tm = max(_pick(t, tm), chunk)
    return pl.pallas_call(
        functools.partial(_branch_a_kernel, n_groups=n_groups, chunk=chunk),
        grid=(t // tm,),
        in_specs=[pl.BlockSpec((tm, aw), lambda i: (i, 0)),
                  pl.BlockSpec((tm, aw), lambda i: (i, 1)),
                  pl.BlockSpec((n_groups, chunk, chunk), lambda i: (0, 0, 0)),
                  pl.BlockSpec((chunk, n_groups), lambda i: (0, 0)),
                  pl.BlockSpec((1, aw), lambda i: (0, 0)),
                  pl.BlockSpec((1, aw), lambda i: (0, 0))],
        out_specs=pl.BlockSpec((tm, aw), lambda i: (i, 0)),
        out_shape=jax.ShapeDtypeStruct((t, aw), BF16),
        compiler_params=_params("parallel"),
        name="branch_a",
    )(z, z, a_ws, a_bs.T, a_ln_g.reshape(1, aw), a_ln_b.reshape(1, aw))


def _gla_kernel(q_ref, k_ref, v_ref, r_ref, lr_ref, wlr_ref, blr_ref, gn_ref, o_ref, s_ref, *, heads, chunk):
    @pl.when(pl.program_id(0) == 0)
    def _():
        s_ref[...] = jnp.zeros_like(s_ref)

    tg, kw = q_ref.shape
    vw = v_ref.shape[1]
    dk, dv = kw // heads, vw // heads
    row = lax.broadcasted_iota(I32, (chunk, chunk), 0)
    col = lax.broadcasted_iota(I32, (chunk, chunk), 1)
    causal = row >= col
    tri = causal.astype(BF16)
    ones = jnp.ones((chunk, LANES), BF16)
    scale = dk ** -0.5

    def body(c, carry):
        rs = pl.ds(pl.multiple_of(c * chunk, chunk), chunk)
        gl = _dot(lr_ref[rs, :].astype(BF16), wlr_ref[...]) + blr_ref[...]
        la = (jnp.minimum(gl, 0.0) - jnp.log1p(jnp.exp(-jnp.abs(gl)))) * (1.0 / GLA_GATE_TAU)
        la_hi = la.astype(BF16)
        la_lo = (la - la_hi.astype(F32)).astype(BF16)
        cum = _dot(tri, la_hi) + _dot(tri, la_lo)
        tot = cum[chunk - 1:chunk, :]
        tot_col = _dot_t0(la_hi, ones) + _dot_t0(la_lo, ones)
        dec_col = jnp.exp(tot_col)
        q = q_ref[rs, :] * scale
        k = k_ref[rs, :]
        qt = (q * jnp.exp(cum)).astype(BF16)
        kt = (k * jnp.exp(-cum)).astype(BF16)
        kl = (k * jnp.exp(tot - cum)).astype(BF16)
        for h in range(heads):
            ks = slice(h * dk, (h + 1) * dk)
            vs = slice(h * dv, (h + 1) * dv)
            vh = v_ref[rs, vs].astype(BF16)
            sc = jnp.where(causal, _dot_t1(qt[:, ks], kt[:, ks]), 0.0).astype(BF16)
            state = s_ref[h]
            o = _dot(sc, vh) + _dot(qt[:, ks], state.astype(BF16))
            dec = jnp.concatenate([dec_col[ks, :]] * (dv // LANES), axis=1) if dv >= LANES else dec_col[ks, :dv]
            s_ref[h] = dec * state + _dot_t0(kl[:, ks], vh)
            on = o * lax.rsqrt(jnp.mean(o * o, axis=-1, keepdims=True) + LN_EPS) * gn_ref[:, vs]
            rr = r_ref[rs, vs]
            o_ref[rs, vs] = (on * (rr * _sigmoid(rr))).astype(o_ref.dtype)
        return carry

    lax.fori_loop(0, tg // chunk, body, 0)


def _gla(z, lr, wlr, blr, gn, kw, vw, off_q, heads, tg):
    t = z.shape[0]
    tg = max(_pick(t, tg), GLA_CHUNK)
    oq = off_q // kw
    ok = oq + 1
    ov = (off_q + 2 * kw) // vw
    orr = ov + 1
    return pl.pallas_call(
        functools.partial(_gla_kernel, heads=heads, chunk=GLA_CHUNK),
        grid=(t // tg,),
        in_specs=[pl.BlockSpec((tg, kw), lambda i: (i, oq)),
                  pl.BlockSpec((tg, kw), lambda i: (i, ok)),
                  pl.BlockSpec((tg, vw), lambda i: (i, ov)),
                  pl.BlockSpec((tg, vw), lambda i: (i, orr)),
                  pl.BlockSpec((tg, LANES), lambda i: (i, 0)),
                  pl.BlockSpec((LANES, kw), lambda i: (0, 0)),
                  pl.BlockSpec((1, kw), lambda i: (0, 0)),
                  pl.BlockSpec((1, vw), lambda i: (0, 0))],
        out_specs=pl.BlockSpec((tg, vw), lambda i: (i, 0)),
        out_shape=jax.ShapeDtypeStruct((t, vw), BF16),
        scratch_shapes=[pltpu.VMEM((heads, kw // heads, vw // heads), F32)],
        compiler_params=_params("arbitrary"),
        name="gla",
    )(z, z, z, z, lr, wlr, blr, gn)


def _ln_router_kernel(h_ref, g_ref, b_ref, wr_ref, br_ref, x1_ref, xp_ref, idx_ref, wgt_ref, rank_ref, cnt_ref,
                      base_ref, *, top_k):
    @pl.when(pl.program_id(0) == 0)
    def _():
        base_ref[...] = jnp.zeros_like(base_ref)

    tm, d = h_ref.shape
    half = d // 2
    x1 = _layer_norm(h_ref[...], g_ref[...], b_ref[...])
    x1_ref[...] = x1
    lo = pltpu.bitcast(x1[:, :half].astype(BF16).astype(F32), U32) >> 16
    hi = pltpu.bitcast(x1[:, half:].astype(BF16).astype(F32), U32) & jnp.uint32(HI_MASK)
    xp_ref[...] = lo | hi

    logits = _dot(x1.astype(BF16), wr_ref[...]) + br_ref[...]
    lane = lax.broadcasted_iota(I32, (tm, LANES), 1).astype(F32)
    member = jnp.zeros((tm, LANES), F32)
    vals, idxs = [], []
    cur = logits
    for _ in range(top_k):
        mx = jnp.max(cur, axis=-1, keepdims=True)
        ix = jnp.min(jnp.where(cur == mx, lane, float(LANES)), axis=-1, keepdims=True)
        sel = lane == ix
        vals.append(mx)
        idxs.append(ix)
        member = member + sel.astype(F32)
        cur = jnp.where(sel, -jnp.inf, cur)
    exps = [jnp.exp(v - vals[0]) for v in vals]
    denom = exps[0]
    for e in exps[1:]:
        denom = denom + e
    r_i = lax.broadcasted_iota(I32, (tm, tm), 0)
    c_i = lax.broadcasted_iota(I32, (tm, tm), 1)
    before = _dot((r_i > c_i).astype(BF16), member.astype(BF16)) + base_ref[...]
    idx_out = jnp.zeros((tm, LANES), F32)
    wgt_out = jnp.zeros((tm, LANES), F32)
    rank_out = jnp.zeros((tm, LANES), F32)
    for j in range(top_k):
        rk = jnp.sum(jnp.where(lane == idxs[j], before, 0.0), axis=-1, keepdims=True)
        at = lane == float(j)
        idx_out = jnp.where(at, idxs[j], idx_out)
        wgt_out = jnp.where(at, exps[j] / denom, wgt_out)
        rank_out = jnp.where(at, rk, rank_out)
    idx_ref[...] = idx_out.astype(I32)
    wgt_ref[...] = wgt_out
    rank_ref[...] = rank_out.astype(I32)
    base_ref[...] = base_ref[...] + jnp.sum(member, axis=0, keepdims=True)
    cnt_ref[...] = base_ref[...]


def _ln_router(h1, g, b, wr, br, tm):
    t, d = h1.shape
    tm = _pick(t, tm)
    row = lambda i: (i, 0)
    fixed = lambda i: (0, 0)
    return pl.pallas_call(
        functools.partial(_ln_router_kernel, top_k=TOP_K),
        grid=(t // tm,),
        in_specs=[pl.BlockSpec((tm, d), row),
                  pl.BlockSpec((1, d), fixed),
                  pl.BlockSpec((1, d), fixed),
                  pl.BlockSpec((d, LANES), fixed),
                  pl.BlockSpec((1, LANES), fixed)],
        out_specs=[pl.BlockSpec((tm, d), row),
                   pl.BlockSpec((tm, d // 2), row),
                   pl.BlockSpec((tm, LANES), row),
                   pl.BlockSpec((tm, LANES), row),
                   pl.BlockSpec((tm, LANES), row),
                   pl.BlockSpec((1, LANES), fixed)],
        out_shape=[jax.ShapeDtypeStruct((t, d), F32),
                   jax.ShapeDtypeStruct((t, d // 2), U32),
                   jax.ShapeDtypeStruct((t, LANES), I32),
                   jax.ShapeDtypeStruct((t, LANES), F32),
                   jax.ShapeDtypeStruct((t, LANES), I32),
                   jax.ShapeDtypeStruct((1, LANES), F32)],
        scratch_shapes=[pltpu.VMEM((1, LANES), F32)],
        compiler_params=_params("arbitrary"),
        name="ln_router",
    )(h1, g, b, wr, br)


def _dispatch_kernel(nb_ref, tok_ref, tok_nxt_ref, xp_hbm, xs_ref, buf, sems, *, tm_e):
    b = pl.program_id(0)
    slot = b % 2
    used = b < nb_ref[0]

    def row_copy(idx_ref, s, r):
        return pltpu.make_async_copy(xp_hbm.at[pl.ds(idx_ref[r], 1)], buf.at[s, pl.ds(r, 1)], sems.at[s])

    def issue(idx_ref, s):
        def body(r, c):
            row_copy(idx_ref, s, r).start()
            return c
        lax.fori_loop(0, tm_e, body, 0)

    @pl.when(b == 0)
    def _():
        issue(tok_ref, slot)

    @pl.when(b + 1 < nb_ref[0])
    def _():
        issue(tok_nxt_ref, 1 - slot)

    @pl.when(jnp.logical_not(used))
    def _():
        xs_ref[...] = jnp.zeros_like(xs_ref)

    @pl.when(used)
    def _():
        def drain(r, c):
            row_copy(tok_ref, slot, r).wait()
            return c

        lax.fori_loop(0, tm_e, drain, 0)
        w = buf[slot]
        half = w.shape[1]
        xs_ref[:, :half] = pltpu.bitcast(w << 16, F32).astype(BF16)
        xs_ref[:, half:] = pltpu.bitcast(w & jnp.uint32(HI_MASK), F32).astype(BF16)


def _dispatch(tok_buf, nb_used, xp, tm_e):
    dh = xp.shape[1]
    n_rows = tok_buf.shape[0]
    last = n_rows // tm_e - 1
    grid_spec = pltpu.PrefetchScalarGridSpec(
        num_scalar_prefetch=1,
        grid=(n_rows // tm_e,),
        in_specs=[pl.BlockSpec((tm_e,), lambda b, nb: (b,), memory_space=pltpu.SMEM),
                  pl.BlockSpec((tm_e,), lambda b, nb: (jnp.minimum(b + 1, last),), memory_space=pltpu.SMEM),
                  pl.BlockSpec(memory_space=pl.ANY)],
        out_specs=pl.BlockSpec((tm_e, 2 * dh), lambda b, nb: (b, 0)),
        scratch_shapes=[pltpu.VMEM((2, tm_e, dh), U32), pltpu.SemaphoreType.DMA((2,))],
    )
    return pl.pallas_call(
        functools.partial(_dispatch_kernel, tm_e=tm_e),
        grid_spec=grid_spec,
        out_shape=jax.ShapeDtypeStruct((n_rows, 2 * dh), BF16),
        compiler_params=_params("arbitrary"),
        name="dispatch",
    )(nb_used, tok_buf, tok_buf, xp)


def _expert_changed(be_ref, nb_ref):
    b = pl.program_id(1)
    fresh = jnp.logical_or(b == 0, be_ref[b] != be_ref[jnp.maximum(b - 1, 0)])
    return b < nb_ref[0], fresh


def _stream_expert_weights(be_ref, nxt_ref, wrap_ref, used, fresh, tile_copies, stage, work):
    j, b = pl.program_id(0), pl.program_id(1)
    nj = pl.num_programs(0)

    @pl.when(jnp.logical_and(j == 0, b == 0))
    def _():
        for c in tile_copies(be_ref[0], 0):
            c.start()

    @pl.when(jnp.logical_and(used, fresh))
    def _():
        for c in tile_copies(be_ref[b], j):
            c.wait()
        for src, dst in zip(stage, work):
            dst[...] = src[...].astype(BF16)
        jn = j + wrap_ref[b]

        @pl.when(jn < nj)
        def _():
            for c in tile_copies(nxt_ref[b], jn):
                c.start()


def _moe1_kernel(be_ref, nb_ref, nxt_ref, wrap_ref, xs_ref, w_hbm, bg_ref, bl_ref, h_ref, sg, sl, wg_s, wl_s, sems,
                 *, tn):
    used, fresh = _expert_changed(be_ref, nb_ref)
    n_lin = pl.num_programs(0)

    def tile_copies(e, jt):
        return (pltpu.make_async_copy(w_hbm.at[e, :, pl.ds(pl.multiple_of(jt * tn, tn), tn)], sg, sems.at[0]),
                pltpu.make_async_copy(w_hbm.at[e, :, pl.ds(pl.multiple_of((n_lin + jt) * tn, tn), tn)], sl,
                                      sems.at[1]))

    _stream_expert_weights(be_ref, nxt_ref, wrap_ref, used, fresh, tile_copies, (sg, sl), (wg_s, wl_s))

    @pl.when(jnp.logical_not(used))
    def _():
        h_ref[...] = jnp.zeros_like(h_ref)

    @pl.when(used)
    def _():
        x = xs_ref[...]
        gate = jnp.minimum(_dot(x, wg_s[...]) + bg_ref[...], SWIGLU_LIMIT)
        lin = jnp.clip(_dot(x, wl_s[...]) + bl_ref[...], -SWIGLU_LIMIT, SWIGLU_LIMIT)
        h_ref[...] = (gate * _sigmoid(SWIGLU_ALPHA * gate) * (lin + 1.0)).astype(h_ref.dtype)


def _moe1(sched, xs, w_gu, b_gu, tm, tn):
    n_rows, dh = xs.shape
    n_exp, d, de2 = w_gu.shape
    de = de2 // 2
    tn = _pick(de, tn)
    nj = de // tn

    def blk(b, nb):
        return jnp.minimum(b, nb[0] - 1)

    grid_spec = pltpu.PrefetchScalarGridSpec(
        num_scalar_prefetch=4,
        grid=(nj, n_rows // tm),
        in_specs=[pl.BlockSpec((tm, dh), lambda j, b, be, nb, *_: (blk(b, nb), 0)),
                  pl.BlockSpec(memory_space=pl.ANY),
                  pl.BlockSpec((None, 1, tn), lambda j, b, be, nb, *_: (be[blk(b, nb)], 0, j)),
                  pl.BlockSpec((None, 1, tn), lambda j, b, be, nb, *_: (be[blk(b, nb)], 0, nj + j))],
        out_specs=pl.BlockSpec((tm, tn), lambda j, b, *_: (b, j)),
        scratch_shapes=[pltpu.VMEM((d, tn), F32), pltpu.VMEM((d, tn), F32),
                        pltpu.VMEM((d, tn), BF16), pltpu.VMEM((d, tn), BF16),
                        pltpu.SemaphoreType.DMA((2,))],
    )
    return pl.pallas_call(
        functools.partial(_moe1_kernel, tn=tn),
        grid_spec=grid_spec,
        out_shape=jax.ShapeDtypeStruct((n_rows, de), BF16),
        compiler_params=_params("arbitrary", "arbitrary"),
        name="moe_up",
    )(*sched, xs, w_gu, b_gu, b_gu)


def _moe2_kernel(be_ref, nb_ref, nxt_ref, wrap_ref, h_ref, w_hbm, b_ref, y_ref, stage, w_s, sems, *, tn):
    used, fresh = _expert_changed(be_ref, nb_ref)

    def tile_copies(e, jt):
        return (pltpu.make_async_copy(w_hbm.at[e, :, pl.ds(pl.multiple_of(jt * tn, tn), tn)], stage, sems.at[0]),)

    _stream_expert_weights(be_ref, nxt_ref, wrap_ref, used, fresh, tile_copies, (stage,), (w_s,))

    @pl.when(jnp.logical_not(used))
    def _():
        y_ref[...] = jnp.zeros_like(y_ref)

    @pl.when(used)
    def _():
        y_ref[...] = _dot(h_ref[...], w_s[...]) + b_ref[...]


def _moe2(sched, h, w_down, b_down, tm, tn):
    n_rows, de = h.shape
    n_exp, _, d = w_down.shape
    tn = _pick(d, tn)

    def blk(b, nb):
        return jnp.minimum(b, nb[0] - 1)

    grid_spec = pltpu.PrefetchScalarGridSpec(
        num_scalar_prefetch=4,
        grid=(d // tn, n_rows // tm),
        in_specs=[pl.BlockSpec((tm, de), lambda j, b, be, nb, *_: (blk(b, nb), 0)),
                  pl.BlockSpec(memory_space=pl.ANY),
                  pl.BlockSpec((None, 1, tn), lambda j, b, be, nb, *_: (be[blk(b, nb)], 0, j))],
        out_specs=pl.BlockSpec((tm, tn), lambda j, b, *_: (b, j)),
        scratch_shapes=[pltpu.VMEM((de, tn), F32), pltpu.VMEM((de, tn), BF16), pltpu.SemaphoreType.DMA((1,))],
    )
    return pl.pallas_call(
        functools.partial(_moe2_kernel, tn=tn),
        grid_spec=grid_spec,
        out_shape=jax.ShapeDtypeStruct((n_rows, d), F32),
        compiler_params=_params("arbitrary", "arbitrary"),
        name="moe_down",
    )(*sched, h, w_down, b_down)


def _combine_kernel(dest_ref, dest_nxt_ref, ys_hbm, x1_ref, wgt_ref, g_ref, b_ref, o_ref, buf, sems, *, tc, top_k,
                    alpha):
    i = pl.program_id(0)
    slot = i % 2

    def row_copy(idx_ref, s, t, j):
        return pltpu.make_async_copy(ys_hbm.at[pl.ds(idx_ref[t * top_k + j], 1)], buf.at[s, j, pl.ds(t, 1)],
                                     sems.at[s])

    def issue(idx_ref, s):
        def body(t, c):
            for j in range(top_k):
                row_copy(idx_ref, s, t, j).start()
            return c
        lax.fori_loop(0, tc, body, 0)

    @pl.when(i == 0)
    def _():
        issue(dest_ref, slot)

    @pl.when(i + 1 < pl.num_programs(0))
    def _():
        issue(dest_nxt_ref, 1 - slot)

    def drain(t, c):
        for j in range(top_k):
            row_copy(dest_ref, slot, t, j).wait()
        return c

    lax.fori_loop(0, tc, drain, 0)
    y = wgt_ref[:, 0:1] * buf[slot, 0]
    for j in range(1, top_k):
        y = y + wgt_ref[:, j:j + 1] * buf[slot, j]
    o_ref[...] = _layer_norm(alpha * x1_ref[...] + y, g_ref[...], b_ref[...])


def _combine(dest_flat, ys, x1, wgt, g, b, alpha, tc):
    t, d = x1.shape
    tc = _pick(t, tc)
    last = t // tc - 1
    return pl.pallas_call(
        functools.partial(_combine_kernel, tc=tc, top_k=TOP_K, alpha=alpha),
        grid=(t // tc,),
        in_specs=[pl.BlockSpec((tc * TOP_K,), lambda i: (i,), memory_space=pltpu.SMEM),
                  pl.BlockSpec((tc * TOP_K,), lambda i: (jnp.minimum(i + 1, last),), memory_space=pltpu.SMEM),
                  pl.BlockSpec(memory_space=pl.ANY),
                  pl.BlockSpec((tc, d), lambda i: (i, 0)),
                  pl.BlockSpec((tc, LANES), lambda i: (i, 0)),
                  pl.BlockSpec((1, d), lambda i: (0, 0)),
                  pl.BlockSpec((1, d), lambda i: (0, 0))],
        out_specs=pl.BlockSpec((tc, d), lambda i: (i, 0)),
        out_shape=jax.ShapeDtypeStruct((t, d), F32),
        scratch_shapes=[pltpu.VMEM((2, TOP_K, tc, d), F32), pltpu.SemaphoreType.DMA((2,))],
        compiler_params=_params("arbitrary"),
        name="combine",
    )(dest_flat, dest_flat, ys, x1, wgt, g, b)


def _layer(x2d, w_in, b_in, a_ws, a_bs, a_ln_g, a_ln_b, gla_w_lr, gla_b_lr, gla_gn_g, w_br_a, w_br_b, w_o,
           ln1_g, ln1_b, w_router, b_router, w_gu, b_gu, w_down, b_down, ln2_g, ln2_b, alpha):
    t, d = x2d.shape
    n_groups, a_chunk, _ = a_ws.shape
    aw = a_ln_g.size
    rank, kw = gla_w_lr.shape
    heads, dv = gla_gn_g.shape
    vw = heads * dv
    n_exp = w_router.shape[1]
    de = w_down.shape[1]
    off_lr = 2 * aw + 2 * kw + 2 * vw
    off_q = 2 * aw

    w_mix = w_in[:, :off_lr].astype(BF16)
    w_gate = w_in[:, off_lr + rank:].astype(BF16)
    w_lr_in = jnp.pad(w_in[:, off_lr:off_lr + rank], ((0, 0), (0, LANES - rank))).astype(BF16)
    b_lr_in = jnp.pad(b_in[off_lr:off_lr + rank], (0, LANES - rank)).reshape(1, LANES)
    wlr = jnp.pad(gla_w_lr, ((0, LANES - rank), (0, 0))).astype(BF16)
    wr = jnp.pad(w_router, ((0, 0), (0, LANES - n_exp))).astype(BF16)
    br = jnp.pad(b_router, (0, LANES - n_exp), constant_values=-jnp.inf).reshape(1, LANES)

    xb = x2d.astype(BF16)
    z = _matmul_bias(xb, w_mix, b_in[:off_lr].reshape(1, -1), F32, 1024, 1024)
    zg = _matmul_bias(xb, w_gate, b_in[off_lr + rank:].reshape(1, -1), F32, 1024, 1024)
    lr = _matmul_bias(xb, w_lr_in, b_lr_in, F32, 2048, LANES)

    ya = _branch_a(z, a_ws, a_bs, a_ln_g, a_ln_b, aw, 2 * a_chunk)
    yb = _gla(z, lr, wlr, gla_b_lr.reshape(1, kw), gla_gn_g.reshape(1, vw), kw, vw, off_q, heads, 256)
    merged = _merge(ya, w_br_a.astype(BF16), yb, w_br_b.astype(BF16), zg, 0, d, 1024, 512)
    h1 = _out_proj_resid(merged, w_o.astype(BF16), x2d, alpha, 1024, 512)

    x1, xp, idx, wgt, rnk, cnt = _ln_router(h1, ln1_g.reshape(1, d), ln1_b.reshape(1, d), wr, br, 256)

    tm_e = 512 if t * TOP_K >= 512 * n_exp else 64
    n_assign = t * TOP_K
    n_blocks = -(-(n_assign + n_exp * (tm_e - 1)) // tm_e)
    n_rows = n_blocks * tm_e
    counts = cnt[0, :n_exp].astype(I32)
    padded = (counts + tm_e - 1) // tm_e * tm_e
    pend = jnp.cumsum(padded)
    pstart = pend - padded
    sel = idx[:, :TOP_K, None] == jnp.arange(n_exp, dtype=I32)[None, None, :]
    dest = jnp.sum(jnp.where(sel, pstart[None, None, :], 0), axis=-1) + rnk[:, :TOP_K]
    dest_flat = dest.reshape(-1).astype(I32)
    nb_used = (pend[-1] // tm_e).astype(I32).reshape(1)
    block_start = jnp.arange(n_blocks, dtype=I32) * tm_e
    block_e = jnp.minimum(jnp.sum((pend[None, :] <= block_start[:, None]).astype(I32), axis=1), n_exp - 1)

    tok_buf = jnp.zeros((n_rows,), I32).at[dest_flat].set(jnp.arange(n_assign, dtype=I32) // TOP_K,
                                                          unique_indices=True)
    xs = _dispatch(tok_buf, nb_used, xp, tm_e)
    experts = jnp.arange(n_exp, dtype=I32)
    later = jnp.where((padded > 0)[None, :] & (experts[None, :] > block_e[:, None]), experts[None, :], n_exp)
    nxt = jnp.min(later, axis=1)
    wrap = (nxt == n_exp).astype(I32)
    nxt_e = jnp.where(nxt == n_exp, block_e[0], nxt).astype(I32)
    sched = (block_e, nb_used, nxt_e, wrap)
    hmid = _moe1(sched, xs, w_gu, b_gu.reshape(n_exp, 1, 2 * de), tm_e, 512)
    ys = _moe2(sched, hmid, w_down, b_down.reshape(n_exp, 1, d), tm_e, 2048)
    return _combine(dest_flat, ys, x1, wgt, ln2_g.reshape(1, d), ln2_b.reshape(1, d), alpha, 128)


def kernel(x, w_in, b_in, a_ws, a_bs, a_ln_g, a_ln_b, gla_w_lr, gla_b_lr, gla_gn_g, w_br_a, w_br_b, w_o, ln1_g, ln1_b, w_router, b_router, w_gu, b_gu, w_down, b_down, ln2_g, ln2_b):
    bsz, seq, d = x.shape
    depth = w_in.shape[0]
    alpha = (2 * depth) ** 0.25
    outs = []
    for bi in range(bsz):
        h = x[bi]
        for l in range(depth):
            h = _layer(h, w_in[l], b_in[l], a_ws[l], a_bs[l], a_ln_g[l], a_ln_b[l], gla_w_lr[l], gla_b_lr[l],
                       gla_gn_g[l], w_br_a[l], w_br_b[l], w_o[l], ln1_g[l], ln1_b[l], w_router[l], b_router[l],
                       w_gu[l], b_gu[l], w_down[l], b_down[l], ln2_g[l], ln2_b[l], alpha)
        outs.append(h)
    return jnp.stack(outs) if bsz > 1 else outs[0][None]
```

```python
import functools

import jax
import jax.numpy as jnp
from jax import lax
from jax.experimental import pallas as pl
from jax.experimental.pallas import tpu as pltpu

F32 = jnp.float32
BF16 = jnp.bfloat16
U32 = jnp.uint32
I32 = jnp.int32

LN_EPS = 1e-5
GLA_CHUNK = 64
GLA_GATE_TAU = 16.0
TOP_K = 4
SWIGLU_LIMIT = 7.0
SWIGLU_ALPHA = 1.702
LANES = 128
VMEM_LIMIT = 56 * 1024 * 1024
HI_MASK = 0xFFFF0000
DISPATCH_STRIPES = 8


def _pick(n, pref):
    t = min(pref, n)
    while n % t:
        t //= 2
    return t


def _dot(a, b):
    return jnp.dot(a, b, preferred_element_type=F32)


def _dot_t0(a, b):
    return lax.dot_general(a, b, (((0,), (0,)), ((), ())), preferred_element_type=F32)


def _dot_t1(a, b):
    return lax.dot_general(a, b, (((1,), (1,)), ((), ())), preferred_element_type=F32)


def _sigmoid(x):
    return 1.0 / (1.0 + jnp.exp(-x))


def _gelu(x):
    return 0.5 * x * (1.0 + lax.erf(x * (2.0 ** -0.5)))


def _layer_norm(x, g, b):
    mu = jnp.mean(x, axis=-1, keepdims=True)
    xc = x - mu
    var = jnp.mean(xc * xc, axis=-1, keepdims=True)
    return xc * lax.rsqrt(var + LN_EPS) * g + b


def _params(*sem):
    return pltpu.CompilerParams(dimension_semantics=sem, vmem_limit_bytes=VMEM_LIMIT)


def _mm_bias_kernel(x_ref, w_ref, b_ref, o_ref):
    o_ref[...] = (_dot(x_ref[...], w_ref[...]) + b_ref[...]).astype(o_ref.dtype)


def _matmul_bias(xb, w, b, out_dtype, tm, tn):
    m, k = xb.shape
    n = b.shape[1]
    tm, tn = _pick(m, tm), _pick(n, tn)
    return pl.pallas_call(
        _mm_bias_kernel,
        grid=(m // tm, n // tn),
        in_specs=[pl.BlockSpec((tm, k), lambda i, j: (i, 0)),
                  pl.BlockSpec((k, tn), lambda i, j: (0, j)),
                  pl.BlockSpec((1, tn), lambda i, j: (0, j))],
        out_specs=pl.BlockSpec((tm, tn), lambda i, j: (i, j)),
        out_shape=jax.ShapeDtypeStruct((m, n), out_dtype),
        compiler_params=_params("parallel", "parallel"),
        name="in_proj",
    )(xb, w, b)


def _merge_kernel(ya_ref, wa_ref, yb_ref, wb_ref, ga_ref, gb_ref, o_ref):
    a = _dot(ya_ref[...], wa_ref[...])
    b = _dot(yb_ref[...], wb_ref[...])
    o_ref[...] = (_sigmoid(ga_ref[...]) * a + _sigmoid(gb_ref[...]) * b).astype(o_ref.dtype)


def _merge(ya, wa, yb, wb, z, off_ga, off_gb, tm, tn):
    m, ka = ya.shape
    kb = yb.shape[1]
    n = wa.shape[1]
    tm, tn = _pick(m, tm), _pick(n, tn)
    while off_ga % tn or off_gb % tn:
        tn //= 2
    oa, ob = off_ga // tn, off_gb // tn
    return pl.pallas_call(
        _merge_kernel,
        grid=(m // tm, n // tn),
        in_specs=[pl.BlockSpec((tm, ka), lambda i, j: (i, 0)),
                  pl.BlockSpec((ka, tn), lambda i, j: (0, j)),
                  pl.BlockSpec((tm, kb), lambda i, j: (i, 0)),
                  pl.BlockSpec((kb, tn), lambda i, j: (0, j)),
                  pl.BlockSpec((tm, tn), lambda i, j: (i, oa + j)),
                  pl.BlockSpec((tm, tn), lambda i, j: (i, ob + j))],
        out_specs=pl.BlockSpec((tm, tn), lambda i, j: (i, j)),
        out_shape=jax.ShapeDtypeStruct((m, n), BF16),
        compiler_params=_params("parallel", "parallel"),
        name="merge",
    )(ya, wa, yb, wb, z, z)


def _mm_resid_kernel(m_ref, w_ref, x_ref, o_ref, *, alpha):
    o_ref[...] = alpha * x_ref[...] + _dot(m_ref[...], w_ref[...])


def _out_proj_resid(mb, w, x, alpha, tm, tn):
    m, k = mb.shape
    n = w.shape[1]
    tm, tn = _pick(m, tm), _pick(n, tn)
    return pl.pallas_call(
        functools.partial(_mm_resid_kernel, alpha=alpha),
        grid=(m // tm, n // tn),
        in_specs=[pl.BlockSpec((tm, k), lambda i, j: (i, 0)),
                  pl.BlockSpec((k, tn), lambda i, j: (0, j)),
                  pl.BlockSpec((tm, tn), lambda i, j: (i, j))],
        out_specs=pl.BlockSpec((tm, tn), lambda i, j: (i, j)),
        out_shape=jax.ShapeDtypeStruct((m, n), F32),
        compiler_params=_params("parallel", "parallel"),
        name="out_proj",
    )(mb, w, x)


def _branch_a_kernel(u_ref, v_ref, ws_ref, bs_ref, g_ref, b_ref, o_ref, *, n_groups, chunk):
    tm, aw = u_ref.shape
    gd = aw // n_groups
    row = lax.broadcasted_iota(I32, (chunk, chunk), 0)
    col = lax.broadcasted_iota(I32, (chunk, chunk), 1)
    causal = row >= col
    for g in range(n_groups):
        ws = jnp.where(causal, ws_ref[g], 0.0).astype(BF16)
        cs = slice(g * gd, (g + 1) * gd)
        for c in range(tm // chunk):
            rs = slice(c * chunk, (c + 1) * chunk)
            vn = _layer_norm(_gelu(v_ref[rs, cs]), g_ref[:, cs], b_ref[:, cs])
            mixed = _dot(ws, vn.astype(BF16)) + bs_ref[:, g:g + 1]
            o_ref[rs, cs] = (_gelu(u_ref[rs, cs]) * mixed).astype(o_ref.dtype)


def _branch_a(z, a_ws, a_bs, a_ln_g, a_ln_b, aw, tm):
    t = z.shape[0]
    n_groups, chunk, _ = a_ws.shape
    tm = max(_pick(t, tm), chunk)
    return pl.pallas_call(
        functools.partial(_branch_a_kernel, n_groups=n_groups, chunk=chunk),
        grid=(t // tm,),
        in_specs=[pl.BlockSpec((tm, aw), lambda i: (i, 0)),
                  pl.BlockSpec((tm, aw), lambda i: (i, 1)),
                  pl.BlockSpec((n_groups, chunk, chunk), lambda i: (0, 0, 0)),
                  pl.BlockSpec((chunk, n_groups), lambda i: (0, 0)),
                  pl.BlockSpec((1, aw), lambda i: (0, 0)),
                  pl.BlockSpec((1, aw), lambda i: (0, 0))],
        out_specs=pl.BlockSpec((tm, aw), lambda i: (i, 0)),
        out_shape=jax.ShapeDtypeStruct((t, aw), BF16),
        compiler_params=_params("parallel"),
        name="branch_a",
    )(z, z, a_ws, a_bs.T, a_ln_g.reshape(1, aw), a_ln_b.reshape(1, aw))


def _gla_kernel(q_ref, k_ref, v_ref, r_ref, lr_ref, wlr_ref, blr_ref, gn_ref, o_ref, s_ref, *, heads, chunk):
    @pl.when(pl.program_id(0) == 0)
    def _():
        s_ref[...] = jnp.zeros_like(s_ref)

    tg, kw = q_ref.shape
    vw = v_ref.shape[1]
    dk, dv = kw // heads, vw // heads
    row = lax.broadcasted_iota(I32, (chunk, chunk), 0)
    col = lax.broadcasted_iota(I32, (chunk, chunk), 1)
    causal = row >= col
    tri = causal.astype(BF16)
    ones = jnp.ones((chunk, LANES), BF16)
    scale = dk ** -0.5

    def body(c, carry):
        rs = pl.ds(pl.multiple_of(c * chunk, chunk), chunk)
        gl = _dot(lr_ref[rs, :].astype(BF16), wlr_ref[...]) + blr_ref[...]
        la = (jnp.minimum(gl, 0.0) - jnp.log1p(jnp.exp(-jnp.abs(gl)))) * (1.0 / GLA_GATE_TAU)
        la_hi = la.astype(BF16)
        la_lo = (la - la_hi.astype(F32)).astype(BF16)
        cum = _dot(tri, la_hi) + _dot(tri, la_lo)
        tot = cum[chunk - 1:chunk, :]
        tot_col = _dot_t0(la_hi, ones) + _dot_t0(la_lo, ones)
        dec_col = jnp.exp(tot_col)
        q = q_ref[rs, :] * scale
        k = k_ref[rs, :]
        qt = (q * jnp.exp(cum)).astype(BF16)
        kt = (k * jnp.exp(-cum)).astype(BF16)
        kl = (k * jnp.exp(tot - cum)).astype(BF16)
        for h in range(heads):
            ks = slice(h * dk, (h + 1) * dk)
            vs = slice(h * dv, (h + 1) * dv)
            vh = v_ref[rs, vs].astype(BF16)
            sc = jnp.where(causal, _dot_t1(qt[:, ks], kt[:, ks]), 0.0).astype(BF16)
            state = s_ref[h]
            o = _dot(sc, vh) + _dot(qt[:, ks], state.astype(BF16))
            dec = jnp.concatenate([dec_col[ks, :]] * (dv // LANES), axis=1) if dv >= LANES else dec_col[ks, :dv]
            s_ref[h] = dec * state + _dot_t0(kl[:, ks], vh)
            on = o * lax.rsqrt(jnp.mean(o * o, axis=-1, keepdims=True) + LN_EPS) * gn_ref[:, vs]
            rr = r_ref[rs, vs]
            o_ref[rs, vs] = (on * (rr * _sigmoid(rr))).astype(o_ref.dtype)
        return carry

    lax.fori_loop(0, tg // chunk, body, 0)


def _gla(z, lr, wlr, blr, gn, kw, vw, off_q, heads, tg):
    t = z.shape[0]
    tg = max(_pick(t, tg), GLA_CHUNK)
    oq = off_q // kw
    ok = oq + 1
    ov = (off_q + 2 * kw) // vw
    orr = ov + 1
    return pl.pallas_call(
        functools.partial(_gla_kernel, heads=heads, chunk=GLA_CHUNK),
        grid=(t // tg,),
        in_specs=[pl.BlockSpec((tg, kw), lambda i: (i, oq)),
                  pl.BlockSpec((tg, kw), lambda i: (i, ok)),
                  pl.BlockSpec((tg, vw), lambda i: (i, ov)),
                  pl.BlockSpec((tg, vw), lambda i: (i, orr)),
                  pl.BlockSpec((tg, LANES), lambda i: (i, 0)),
                  pl.BlockSpec((LANES, kw), lambda i: (0, 0)),
                  pl.BlockSpec((1, kw), lambda i: (0, 0)),
                  pl.BlockSpec((1, vw), lambda i: (0, 0))],
        out_specs=pl.BlockSpec((tg, vw), lambda i: (i, 0)),
        out_shape=jax.ShapeDtypeStruct((t, vw), BF16),
        scratch_shapes=[pltpu.VMEM((heads, kw // heads, vw // heads), F32)],
        compiler_params=_params("arbitrary"),
        name="gla",
    )(z, z, z, z, lr, wlr, blr, gn)


def _ln_router_kernel(h_ref, g_ref, b_ref, wr_ref, br_ref, x1_ref, xp_ref, idx_ref, wgt_ref, rank_ref, cnt_ref,
                      base_ref, *, top_k):
    @pl.when(pl.program_id(0) == 0)
    def _():
        base_ref[...] = jnp.zeros_like(base_ref)

    tm, d = h_ref.shape
    half = d // 2
    x1 = _layer_norm(h_ref[...], g_ref[...], b_ref[...])
    x1_ref[...] = x1
    lo = pltpu.bitcast(x1[:, :half].astype(BF16).astype(F32), U32) >> 16
    hi = pltpu.bitcast(x1[:, half:].astype(BF16).astype(F32), U32) & jnp.uint32(HI_MASK)
    xp_ref[...] = lo | hi

    logits = _dot(x1.astype(BF16), wr_ref[...]) + br_ref[...]
    lane = lax.broadcasted_iota(I32, (tm, LANES), 1).astype(F32)
    member = jnp.zeros((tm, LANES), F32)
    vals, idxs = [], []
    cur = logits
    for _ in range(top_k):
        mx = jnp.max(cur, axis=-1, keepdims=True)
        ix = jnp.min(jnp.where(cur == mx, lane, float(LANES)), axis=-1, keepdims=True)
        sel = lane == ix
        vals.append(mx)
        idxs.append(ix)
        member = member + sel.astype(F32)
        cur = jnp.where(sel, -jnp.inf, cur)
    exps = [jnp.exp(v - vals[0]) for v in vals]
    denom = exps[0]
    for e in exps[1:]:
        denom = denom + e
    r_i = lax.broadcasted_iota(I32, (tm, tm), 0)
    c_i = lax.broadcasted_iota(I32, (tm, tm), 1)
    before = _dot((r_i > c_i).astype(BF16), member.astype(BF16)) + base_ref[...]
    idx_out = jnp.zeros((tm, LANES), F32)
    wgt_out = jnp.zeros((tm, LANES), F32)
    rank_out = jnp.zeros((tm, LANES), F32)
    for j in range(top_k):
        rk = jnp.sum(jnp.where(lane == idxs[j], before, 0.0), axis=-1, keepdims=True)
        at = lane == float(j)
        idx_out = jnp.where(at, idxs[j], idx_out)
        wgt_out = jnp.where(at, exps[j] / denom, wgt_out)
        rank_out = jnp.where(at, rk, rank_out)
    idx_ref[...] = idx_out.astype(I32)
    wgt_ref[...] = wgt_out
    rank_ref[...] = rank_out.astype(I32)
    base_ref[...] = base_ref[...] + jnp.sum(member, axis=0, keepdims=True)
    cnt_ref[...] = base_ref[...]


def _ln_router(h1, g, b, wr, br, tm):
    t, d = h1.shape
    tm = _pick(t, tm)
    row = lambda i: (i, 0)
    fixed = lambda i: (0, 0)
    return pl.pallas_call(
        functools.partial(_ln_router_kernel, top_k=TOP_K),
        grid=(t // tm,),
        in_specs=[pl.BlockSpec((tm, d), row),
                  pl.BlockSpec((1, d), fixed),
                  pl.BlockSpec((1, d), fixed),
                  pl.BlockSpec((d, LANES), fixed),
                  pl.BlockSpec((1, LANES), fixed)],
        out_specs=[pl.BlockSpec((tm, d), row),
                   pl.BlockSpec((tm, d // 2), row),
                   pl.BlockSpec((tm, LANES), row),
                   pl.BlockSpec((tm, LANES), row),
                   pl.BlockSpec((tm, LANES), row),
                   pl.BlockSpec((1, LANES), fixed)],
        out_shape=[jax.ShapeDtypeStruct((t, d), F32),
                   jax.ShapeDtypeStruct((t, d // 2), U32),
                   jax.ShapeDtypeStruct((t, LANES), I32),
                   jax.ShapeDtypeStruct((t, LANES), F32),
                   jax.ShapeDtypeStruct((t, LANES), I32),
                   jax.ShapeDtypeStruct((1, LANES), F32)],
        scratch_shapes=[pltpu.VMEM((1, LANES), F32)],
        compiler_params=_params("arbitrary"),
        name="ln_router",
    )(h1, g, b, wr, br)


def _dispatch_kernel(nb_ref, tok_ref, tok_nxt_ref, xp_hbm, xs_ref, buf, sems, *, tm_e):
    b = pl.program_id(0)
    slot = b % 2
    used = b < nb_ref[0]
    stripe = tm_e // DISPATCH_STRIPES

    def row_copy(idx_ref, s, u, k):
        return pltpu.make_async_copy(xp_hbm.at[pl.ds(idx_ref[u * stripe + k], 1)], buf.at[s, u, pl.ds(k, 1)],
                                     sems.at[s])

    def issue(idx_ref, s):
        def body(k, c):
            for u in range(DISPATCH_STRIPES):
                row_copy(idx_ref, s, u, k).start()
            return c
        lax.fori_loop(0, stripe, body, 0)

    @pl.when(b == 0)
    def _():
        issue(tok_ref, slot)

    @pl.when(b + 1 < nb_ref[0])
    def _():
        issue(tok_nxt_ref, 1 - slot)

    @pl.when(jnp.logical_not(used))
    def _():
        xs_ref[...] = jnp.zeros_like(xs_ref)

    @pl.when(used)
    def _():
        def drain(k, c):
            for u in range(DISPATCH_STRIPES):
                row_copy(tok_ref, slot, u, k).wait()
            return c

        lax.fori_loop(0, stripe, drain, 0)
        w = buf[slot].reshape(tm_e, buf.shape[-1])
        half = w.shape[1]
        xs_ref[:, :half] = pltpu.bitcast(w << 16, F32).astype(BF16)
        xs_ref[:, half:] = pltpu.bitcast(w & jnp.uint32(HI_MASK), F32).astype(BF16)


def _dispatch(tok_buf, nb_used, xp, tm_e):
    dh = xp.shape[1]
    n_rows = tok_buf.shape[0]
    last = n_rows // tm_e - 1
    grid_spec = pltpu.PrefetchScalarGridSpec(
        num_scalar_prefetch=1,
        grid=(n_rows // tm_e,),
        in_specs=[pl.BlockSpec((tm_e,), lambda b, nb: (b,), memory_space=pltpu.SMEM),
                  pl.BlockSpec((tm_e,), lambda b, nb: (jnp.minimum(b + 1, last),), memory_space=pltpu.SMEM),
                  pl.BlockSpec(memory_space=pl.ANY)],
        out_specs=pl.BlockSpec((tm_e, 2 * dh), lambda b, nb: (b, 0)),
        scratch_shapes=[pltpu.VMEM((2, DISPATCH_STRIPES, tm_e // DISPATCH_STRIPES, dh), U32),
                        pltpu.SemaphoreType.DMA((2,))],
    )
    return pl.pallas_call(
        functools.partial(_dispatch_kernel, tm_e=tm_e),
        grid_spec=grid_spec,
        out_shape=jax.ShapeDtypeStruct((n_rows, 2 * dh), BF16),
        compiler_params=_params("arbitrary"),
        name="dispatch",
    )(nb_used, tok_buf, tok_buf, xp)


def _expert_changed(be_ref, nb_ref):
    b = pl.program_id(1)
    fresh = jnp.logical_or(b == 0, be_ref[b] != be_ref[jnp.maximum(b - 1, 0)])
    return b < nb_ref[0], fresh


def _stream_expert_weights(be_ref, nxt_ref, wrap_ref, used, fresh, tile_copies, stage, work):
    j, b = pl.program_id(0), pl.program_id(1)
    nj = pl.num_programs(0)

    @pl.when(jnp.logical_and(j == 0, b == 0))
    def _():
        for c in tile_copies(be_ref[0], 0):
            c.start()

    @pl.when(jnp.logical_and(used, fresh))
    def _():
        for c in tile_copies(be_ref[b], j):
            c.wait()
        for src, dst in zip(stage, work):
            dst[...] = src[...].astype(BF16)
        jn = j + wrap_ref[b]

        @pl.when(jn < nj)
        def _():
            for c in tile_copies(nxt_ref[b], jn):
                c.start()


def _moe1_kernel(be_ref, nb_ref, nxt_ref, wrap_ref, xs_ref, w_hbm, bg_ref, bl_ref, h_ref, sg, sl, wg_s, wl_s, sems,
                 *, tn):
    used, fresh = _expert_changed(be_ref, nb_ref)
    n_lin = pl.num_programs(0)

    def tile_copies(e, jt):
        return (pltpu.make_async_copy(w_hbm.at[e, :, pl.ds(pl.multiple_of(jt * tn, tn), tn)], sg, sems.at[0]),
                pltpu.make_async_copy(w_hbm.at[e, :, pl.ds(pl.multiple_of((n_lin + jt) * tn, tn), tn)], sl,
                                      sems.at[1]))

    _stream_expert_weights(be_ref, nxt_ref, wrap_ref, used, fresh, tile_copies, (sg, sl), (wg_s, wl_s))

    @pl.when(jnp.logical_not(used))
    def _():
        h_ref[...] = jnp.zeros_like(h_ref)

    @pl.when(used)
    def _():
        x = xs_ref[...]
        gate = jnp.minimum(_dot(x, wg_s[...]) + bg_ref[...], SWIGLU_LIMIT)
        lin = jnp.clip(_dot(x, wl_s[...]) + bl_ref[...], -SWIGLU_LIMIT, SWIGLU_LIMIT)
        h_ref[...] = (gate * _sigmoid(SWIGLU_ALPHA * gate) * (lin + 1.0)).astype(h_ref.dtype)


def _moe1(sched, xs, w_gu, b_gu, tm, tn):
    n_rows, dh = xs.shape
    n_exp, d, de2 = w_gu.shape
    de = de2 // 2
    tn = _pick(de, tn)
    nj = de // tn

    def blk(b, nb):
        return jnp.minimum(b, nb[0] - 1)

    grid_spec = pltpu.PrefetchScalarGridSpec(
        num_scalar_prefetch=4,
        grid=(nj, n_rows // tm),
        in_specs=[pl.BlockSpec((tm, dh), lambda j, b, be, nb, *_: (blk(b, nb), 0)),
                  pl.BlockSpec(memory_space=pl.ANY),
                  pl.BlockSpec((None, 1, tn), lambda j, b, be, nb, *_: (be[blk(b, nb)], 0, j)),
                  pl.BlockSpec((None, 1, tn), lambda j, b, be, nb, *_: (be[blk(b, nb)], 0, nj + j))],
        out_specs=pl.BlockSpec((tm, tn), lambda j, b, *_: (b, j)),
        scratch_shapes=[pltpu.VMEM((d, tn), F32), pltpu.VMEM((d, tn), F32),
                        pltpu.VMEM((d, tn), BF16), pltpu.VMEM((d, tn), BF16),
                        pltpu.SemaphoreType.DMA((2,))],
    )
    return pl.pallas_call(
        functools.partial(_moe1_kernel, tn=tn),
        grid_spec=grid_spec,
        out_shape=jax.ShapeDtypeStruct((n_rows, de), BF16),
        compiler_params=_params("arbitrary", "arbitrary"),
        name="moe_up",
    )(*sched, xs, w_gu, b_gu, b_gu)


def _moe2_kernel(be_ref, nb_ref, nxt_ref, wrap_ref, h_ref, w_hbm, b_ref, y_ref, stage, w_s, sems, *, tn):
    used, fresh = _expert_changed(be_ref, nb_ref)

    def tile_copies(e, jt):
        return (pltpu.make_async_copy(w_hbm.at[e, :, pl.ds(pl.multiple_of(jt * tn, tn), tn)], stage, sems.at[0]),)

    _stream_expert_weights(be_ref, nxt_ref, wrap_ref, used, fresh, tile_copies, (stage,), (w_s,))

    @pl.when(jnp.logical_not(used))
    def _():
        y_ref[...] = jnp.zeros_like(y_ref)

    @pl.when(used)
    def _():
        y_ref[...] = _dot(h_ref[...], w_s[...]) + b_ref[...]


def _moe2(sched, h, w_down, b_down, tm, tn):
    n_rows, de = h.shape
    n_exp, _, d = w_down.shape
    tn = _pick(d, tn)

    def blk(b, nb):
        return jnp.minimum(b, nb[0] - 1)

    grid_spec = pltpu.PrefetchScalarGridSpec(
        num_scalar_prefetch=4,
        grid=(d // tn, n_rows // tm),
        in_specs=[pl.BlockSpec((tm, de), lambda j, b, be, nb, *_: (blk(b, nb), 0)),
                  pl.BlockSpec(memory_space=pl.ANY),
                  pl.BlockSpec((None, 1, tn), lambda j, b, be, nb, *_: (be[blk(b, nb)], 0, j))],
        out_specs=pl.BlockSpec((tm, tn), lambda j, b, *_: (b, j)),
        scratch_shapes=[pltpu.VMEM((de, tn), F32), pltpu.VMEM((de, tn), BF16), pltpu.SemaphoreType.DMA((1,))],
    )
    return pl.pallas_call(
        functools.partial(_moe2_kernel, tn=tn),
        grid_spec=grid_spec,
        out_shape=jax.ShapeDtypeStruct((n_rows, d), F32),
        compiler_params=_params("arbitrary", "arbitrary"),
        name="moe_down",
    )(*sched, h, w_down, b_down)


def _combine_kernel(dest_ref, dest_nxt_ref, ys_hbm, x1_ref, wgt_ref, g_ref, b_ref, o_ref, buf, sems, *, tc, top_k,
                    alpha):
    i = pl.program_id(0)
    slot = i % 2

    def row_copy(idx_ref, s, t, j):
        return pltpu.make_async_copy(ys_hbm.at[pl.ds(idx_ref[t * top_k + j], 1)], buf.at[s, j, pl.ds(t, 1)],
                                     sems.at[s])

    def issue(idx_ref, s):
        def body(t, c):
            for j in range(top_k):
                row_copy(idx_ref, s, t, j).start()
            return c
        lax.fori_loop(0, tc, body, 0, unroll=2)

    @pl.when(i == 0)
    def _():
        issue(dest_ref, slot)

    @pl.when(i + 1 < pl.num_programs(0))
    def _():
        issue(dest_nxt_ref, 1 - slot)

    def drain(t, c):
        for j in range(top_k):
            row_copy(dest_ref, slot, t, j).wait()
        return c

    lax.fori_loop(0, tc, drain, 0, unroll=2)
    y = wgt_ref[:, 0:1] * buf[slot, 0]
    for j in range(1, top_k):
        y = y + wgt_ref[:, j:j + 1] * buf[slot, j]
    o_ref[...] = _layer_norm(alpha * x1_ref[...] + y, g_ref[...], b_ref[...])


def _combine(dest_flat, ys, x1, wgt, g, b, alpha, tc):
    t, d = x1.shape
    tc = _pick(t, tc)
    last = t // tc - 1
    return pl.pallas_call(
        functools.partial(_combine_kernel, tc=tc, top_k=TOP_K, alpha=alpha),
        grid=(t // tc,),
        in_specs=[pl.BlockSpec((tc * TOP_K,), lambda i: (i,), memory_space=pltpu.SMEM),
                  pl.BlockSpec((tc * TOP_K,), lambda i: (jnp.minimum(i + 1, last),), memory_space=pltpu.SMEM),
                  pl.BlockSpec(memory_space=pl.ANY),
                  pl.BlockSpec((tc, d), lambda i: (i, 0)),
                  pl.BlockSpec((tc, LANES), lambda i: (i, 0)),
                  pl.BlockSpec((1, d), lambda i: (0, 0)),
                  pl.BlockSpec((1, d), lambda i: (0, 0))],
        out_specs=pl.BlockSpec((tc, d), lambda i: (i, 0)),
        out_shape=jax.ShapeDtypeStruct((t, d), F32),
        scratch_shapes=[pltpu.VMEM((2, TOP_K, tc, d), F32), pltpu.SemaphoreType.DMA((2,))],
        compiler_params=_params("arbitrary"),
        name="combine",
    )(dest_flat, dest_flat, ys, x1, wgt, g, b)


def _layer(x2d, w_in, b_in, a_ws, a_bs, a_ln_g, a_ln_b, gla_w_lr, gla_b_lr, gla_gn_g, w_br_a, w_br_b, w_o,
           ln1_g, ln1_b, w_router, b_router, w_gu, b_gu, w_down, b_down, ln2_g, ln2_b, alpha):
    t, d = x2d.shape
    n_groups, a_chunk, _ = a_ws.shape
    aw = a_ln_g.size
    rank, kw = gla_w_lr.shape
    heads, dv = gla_gn_g.shape
    vw = heads * dv
    n_exp = w_router.shape[1]
    de = w_down.shape[1]
    off_lr = 2 * aw + 2 * kw + 2 * vw
    off_q = 2 * aw

    w_mix = w_in.astype(BF16)
    w_gate = w_mix[:, off_lr + rank:]
    w_lr_in = jnp.pad(w_in[:, off_lr:off_lr + rank], ((0, 0), (0, LANES - rank))).astype(BF16)
    b_lr_in = jnp.pad(b_in[off_lr:off_lr + rank], (0, LANES - rank)).reshape(1, LANES)
    wlr = jnp.pad(gla_w_lr, ((0, LANES - rank), (0, 0))).astype(BF16)
    wr = jnp.pad(w_router, ((0, 0), (0, LANES - n_exp))).astype(BF16)
    br = jnp.pad(b_router, (0, LANES - n_exp), constant_values=-jnp.inf).reshape(1, LANES)

    xb = x2d.astype(BF16)
    z = _matmul_bias(xb, w_mix, b_in[:off_lr].reshape(1, -1), F32, 1024, 1024)
    zg = _matmul_bias(xb, w_gate, b_in[off_lr + rank:].reshape(1, -1), F32, 1024, 1024)
    lr = _matmul_bias(xb, w_lr_in, b_lr_in, F32, 2048, LANES)

    ya = _branch_a(z, a_ws, a_bs, a_ln_g, a_ln_b, aw, 2 * a_chunk)
    yb = _gla(z, lr, wlr, gla_b_lr.reshape(1, kw), gla_gn_g.reshape(1, vw), kw, vw, off_q, heads, 256)
    merged = _merge(ya, w_br_a.astype(BF16), yb, w_br_b.astype(BF16), zg, 0, d, 1024, 512)
    h1 = _out_proj_resid(merged, w_o.astype(BF16), x2d, alpha, 1024, 512)

    x1, xp, idx, wgt, rnk, cnt = _ln_router(h1, ln1_g.reshape(1, d), ln1_b.reshape(1, d), wr, br, 256)

    tm_e = 512 if t * TOP_K >= 512 * n_exp else 64
    n_assign = t * TOP_K
    n_blocks = -(-(n_assign + n_exp * (tm_e - 1)) // tm_e)
    n_rows = n_blocks * tm_e
    counts = cnt[0, :n_exp].astype(I32)
    padded = (counts + tm_e - 1) // tm_e * tm_e
    pend = jnp.cumsum(padded)
    pstart = pend - padded
    sel = idx[:, :TOP_K, None] == jnp.arange(n_exp, dtype=I32)[None, None, :]
    dest = jnp.sum(jnp.where(sel, pstart[None, None, :], 0), axis=-1) + rnk[:, :TOP_K]
    dest_flat = dest.reshape(-1).astype(I32)
    nb_used = (pend[-1] // tm_e).astype(I32).reshape(1)
    block_start = jnp.arange(n_blocks, dtype=I32) * tm_e
    block_e = jnp.minimum(jnp.sum((pend[None, :] <= block_start[:, None]).astype(I32), axis=1), n_exp - 1)

    tok_buf = jnp.zeros((n_rows,), I32).at[dest_flat].set(jnp.arange(n_assign, dtype=I32) // TOP_K,
                                                          unique_indices=True)
    xs = _dispatch(tok_buf, nb_used, xp, tm_e)
    experts = jnp.arange(n_exp, dtype=I32)
    later = jnp.where((padded > 0)[None, :] & (experts[None, :] > block_e[:, None]), experts[None, :], n_exp)
    nxt = jnp.min(later, axis=1)
    wrap = (nxt == n_exp).astype(I32)
    nxt_e = jnp.where(nxt == n_exp, block_e[0], nxt).astype(I32)
    sched = (block_e, nb_used, nxt_e, wrap)
    hmid = _moe1(sched, xs, w_gu, b_gu.reshape(n_exp, 1, 2 * de), tm_e, 512)
    ys = _moe2(sched, hmid, w_down, b_down.reshape(n_exp, 1, d), tm_e, 2048)
    return _combine(dest_flat, ys, x1, wgt, ln2_g.reshape(1, d), ln2_b.reshape(1, d), alpha, 128)


def kernel(x, w_in, b_in, a_ws, a_bs, a_ln_g, a_ln_b, gla_w_lr, gla_b_lr, gla_gn_g, w_br_a, w_br_b, w_o, ln1_g, ln1_b, w_router, b_router, w_gu, b_gu, w_down, b_down, ln2_g, ln2_b):
    bsz, seq, d = x.shape
    depth = w_in.shape[0]
    alpha = (2 * depth) ** 0.25
    outs = []
    for bi in range(bsz):
        h = x[bi]
        for l in range(depth):
            h = _layer(h, w_in[l], b_in[l], a_ws[l], a_bs[l], a_ln_g[l], a_ln_b[l], gla_w_lr[l], gla_b_lr[l],
                       gla_gn_g[l], w_br_a[l], w_br_b[l], w_o[l], ln1_g[l], ln1_b[l], w_router[l], b_router[l],
                       w_gu[l], b_gu[l], w_down[l], b_down[l], ln2_g[l], ln2_b[l], alpha)
        outs.append(h)
    return jnp.stack(outs) if bsz > 1 else outs[0][None]
```

```python
import functools

import jax
import jax.numpy as jnp
from jax import lax
from jax.experimental import pallas as pl
from jax.experimental.pallas import tpu as pltpu

F32 = jnp.float32
BF16 = jnp.bfloat16
U32 = jnp.uint32
I32 = jnp.int32

LN_EPS = 1e-5
GLA_CHUNK = 64
GLA_GATE_TAU = 16.0
TOP_K = 4
SWIGLU_LIMIT = 7.0
SWIGLU_ALPHA = 1.702
LANES = 128
VMEM_LIMIT = 56 * 1024 * 1024
HI_MASK = 0xFFFF0000
DISPATCH_STRIPES = 8

def _pick(n, pref):
    t = min(pref, n)
    while n % t:
        t //= 2
    return t


def _dot(a, b):
    return jnp.dot(a, b, preferred_element_type=F32)


def _dot_t0(a, b):
    return lax.dot_general(a, b, (((0,), (0,)), ((), ())), preferred_element_type=F32)


def _dot_t1(a, b):
    return lax.dot_general(a, b, (((1,), (1,)), ((), ())), preferred_element_type=F32)


def _sigmoid(x):
    return 1.0 / (1.0 + jnp.exp(-x))


def _gelu(x):
    return 0.5 * x * (1.0 + lax.erf(x * (2.0 ** -0.5)))


def _layer_norm(x, g, b):
    mu = jnp.mean(x, axis=-1, keepdims=True)
    xc = x - mu
    var = jnp.mean(xc * xc, axis=-1, keepdims=True)
    return xc * lax.rsqrt(var + LN_EPS) * g + b


def _params(*sem):
    return pltpu.CompilerParams(dimension_semantics=sem, vmem_limit_bytes=VMEM_LIMIT)


def _mm_bias_kernel(x_ref, w_ref, b_ref, o_ref):
    o_ref[...] = (_dot(x_ref[...], w_ref[...]) + b_ref[...]).astype(o_ref.dtype)


def _matmul_bias(xb, w, b, out_dtype, tm, tn):
    m, k = xb.shape
    n = b.shape[1]
    tm, tn = _pick(m, tm), _pick(n, tn)
    return pl.pallas_call(
        _mm_bias_kernel,
        grid=(m // tm, n // tn),
        in_specs=[pl.BlockSpec((tm, k), lambda i, j: (i, 0)),
                  pl.BlockSpec((k, tn), lambda i, j: (0, j)),
                  pl.BlockSpec((1, tn), lambda i, j: (0, j))],
        out_specs=pl.BlockSpec((tm, tn), lambda i, j: (i, j)),
        out_shape=jax.ShapeDtypeStruct((m, n), out_dtype),
        compiler_params=_params("parallel", "parallel"),
        name="in_proj",
    )(xb, w, b)


def _merge_kernel(ya_ref, wa_ref, yb_ref, wb_ref, ga_ref, gb_ref, o_ref):
    a = _dot(ya_ref[...], wa_ref[...])
    b = _dot(yb_ref[...], wb_ref[...])
    o_ref[...] = (_sigmoid(ga_ref[...]) * a + _sigmoid(gb_ref[...]) * b).astype(o_ref.dtype)


def _merge(ya, wa, yb, wb, z, off_ga, off_gb, tm, tn):
    m, ka = ya.shape
    kb = yb.shape[1]
    n = wa.shape[1]
    tm, tn = _pick(m, tm), _pick(n, tn)
    while off_ga % tn or off_gb % tn:
        tn //= 2
    oa, ob = off_ga // tn, off_gb // tn
    return pl.pallas_call(
        _merge_kernel,
        grid=(m // tm, n // tn),
        in_specs=[pl.BlockSpec((tm, ka), lambda i, j: (i, 0)),
                  pl.BlockSpec((ka, tn), lambda i, j: (0, j)),
                  pl.BlockSpec((tm, kb), lambda i, j: (i, 0)),
                  pl.BlockSpec((kb, tn), lambda i, j: (0, j)),
                  pl.BlockSpec((tm, tn), lambda i, j: (i, oa + j)),
                  pl.BlockSpec((tm, tn), lambda i, j: (i, ob + j))],
        out_specs=pl.BlockSpec((tm, tn), lambda i, j: (i, j)),
        out_shape=jax.ShapeDtypeStruct((m, n), BF16),
        compiler_params=_params("parallel", "parallel"),
        name="merge",
    )(ya, wa, yb, wb, z, z)


def _mm_resid_kernel(m_ref, w_ref, x_ref, o_ref, *, alpha):
    o_ref[...] = alpha * x_ref[...] + _dot(m_ref[...], w_ref[...])


def _out_proj_resid(mb, w, x, alpha, tm, tn):
    m, k = mb.shape
    n = w.shape[1]
    tm, tn = _pick(m, tm), _pick(n, tn)
    return pl.pallas_call(
        functools.partial(_mm_resid_kernel, alpha=alpha),
        grid=(m // tm, n // tn),
        in_specs=[pl.BlockSpec((tm, k), lambda i, j: (i, 0)),
                  pl.BlockSpec((k, tn), lambda i, j: (0, j)),
                  pl.BlockSpec((tm, tn), lambda i, j: (i, j))],
        out_specs=pl.BlockSpec((tm, tn), lambda i, j: (i, j)),
        out_shape=jax.ShapeDtypeStruct((m, n), F32),
        compiler_params=_params("parallel", "parallel"),
        name="out_proj",
    )(mb, w, x)


def _branch_a_kernel(u_ref, v_ref, ws_ref, bs_ref, g_ref, b_ref, o_ref, *, n_groups, chunk):
    tm, aw = u_ref.shape
    gd = aw // n_groups
    row = lax.broadcasted_iota(I32, (chunk, chunk), 0)
    col = lax.broadcasted_iota(I32, (chunk, chunk), 1)
    causal = row >= col
    for g in range(n_groups):
        ws = jnp.where(causal, ws_ref[g], 0.0).astype(BF16)
        cs = slice(g * gd, (g + 1) * gd)
        for c in range(tm // chunk):
            rs = slice(c * chunk, (c + 1) * chunk)
            vn = _layer_norm(_gelu(v_ref[rs, cs]), g_ref[:, cs], b_ref[:, cs])
            mixed = _dot(ws, vn.astype(BF16)) + bs_ref[:, g:g + 1]
            o_ref[rs, cs] = (_gelu(u_ref[rs, cs]) * mixed).astype(o_ref.dtype)


def _branch_a(z, a_ws, a_bs, a_ln_g, a_ln_b, aw, tm):
    t = z.shape[0]
    n_groups, chunk, _ = a_ws.shape
    tm = max(_pick(t, tm), chunk)
    return pl.pallas_call(
        functools.partial(_branch_a_kernel, n_groups=n_groups, chunk=chunk),
        grid=(t // tm,),
        in_specs=[pl.BlockSpec((tm, aw), lambda i: (i, 0)),
                  pl.BlockSpec((tm, aw), lambda i: (i, 1)),
                  pl.BlockSpec((n_groups, chunk, chunk), lambda i: (0, 0, 0)),
                  pl.BlockSpec((chunk, n_groups), lambda i: (0, 0)),
                  pl.BlockSpec((1, aw), lambda i: (0, 0)),
                  pl.BlockSpec((1, aw), lambda i: (0, 0))],
        out_specs=pl.BlockSpec((tm, aw), lambda i: (i, 0)),
        out_shape=jax.ShapeDtypeStruct((t, aw), BF16),
        compiler_params=_params("parallel"),
        name="branch_a",
    )(z, z, a_ws, a_bs.T, a_ln_g.reshape(1, aw), a_ln_b.reshape(1, aw))


def _gla_kernel(q_ref, k_ref, v_ref, r_ref, lr_ref, wlr_ref, blr_ref, gn_ref, o_ref, s_ref, *, heads, chunk):
    @pl.when(pl.program_id(0) == 0)
    def _():
        s_ref[...] = jnp.zeros_like(s_ref)

    tg, kw = q_ref.shape
    vw = v_ref.shape[1]
    dk, dv = kw // heads, vw // heads
    row = lax.broadcasted_iota(I32, (chunk, chunk), 0)
    col = lax.broadcasted_iota(I32, (chunk, chunk), 1)
    causal = row >= col
    tri = causal.astype(BF16)
    ones = jnp.ones((chunk, LANES), BF16)
    scale = dk ** -0.5

    def body(c, carry):
        rs = pl.ds(pl.multiple_of(c * chunk, chunk), chunk)
        gl = _dot(lr_ref[rs, :].astype(BF16), wlr_ref[...]) + blr_ref[...]
        la = (jnp.minimum(gl, 0.0) - jnp.log1p(jnp.exp(-jnp.abs(gl)))) * (1.0 / GLA_GATE_TAU)
        la_hi = la.astype(BF16)
        la_lo = (la - la_hi.astype(F32)).astype(BF16)
        cum = _dot(tri, la_hi) + _dot(tri, la_lo)
        tot = cum[chunk - 1:chunk, :]
        tot_col = _dot_t0(la_hi, ones) + _dot_t0(la_lo, ones)
        dec_col = jnp.exp(tot_col)
        q = q_ref[rs, :] * scale
        k = k_ref[rs, :]
        qt = (q * jnp.exp(cum)).astype(BF16)
        kt = (k * jnp.exp(-cum)).astype(BF16)
        kl = (k * jnp.exp(tot - cum)).astype(BF16)
        for h in range(heads):
            ks = slice(h * dk, (h + 1) * dk)
            vs = slice(h * dv, (h + 1) * dv)
            vh = v_ref[rs, vs].astype(BF16)
            sc = jnp.where(causal, _dot_t1(qt[:, ks], kt[:, ks]), 0.0).astype(BF16)
            state = s_ref[h]
            o = _dot(sc, vh) + _dot(qt[:, ks], state.astype(BF16))
            dec = jnp.concatenate([dec_col[ks, :]] * (dv // LANES), axis=1) if dv >= LANES else dec_col[ks, :dv]
            s_ref[h] = dec * state + _dot_t0(kl[:, ks], vh)
            on = o * lax.rsqrt(jnp.mean(o * o, axis=-1, keepdims=True) + LN_EPS) * gn_ref[:, vs]
            rr = r_ref[rs, vs]
            o_ref[rs, vs] = (on * (rr * _sigmoid(rr))).astype(o_ref.dtype)
        return carry

    lax.fori_loop(0, tg // chunk, body, 0)


def _gla(z, lr, wlr, blr, gn, kw, vw, off_q, heads, tg):
    t = z.shape[0]
    tg = max(_pick(t, tg), GLA_CHUNK)
    oq = off_q // kw
    ok = oq + 1
    ov = (off_q + 2 * kw) // vw
    orr = ov + 1
    return pl.pallas_call(
        functools.partial(_gla_kernel, heads=heads, chunk=GLA_CHUNK),
        grid=(t // tg,),
        in_specs=[pl.BlockSpec((tg, kw), lambda i: (i, oq)),
                  pl.BlockSpec((tg, kw), lambda i: (i, ok)),
                  pl.BlockSpec((tg, vw), lambda i: (i, ov)),
                  pl.BlockSpec((tg, vw), lambda i: (i, orr)),
                  pl.BlockSpec((tg, LANES), lambda i: (i, 0)),
                  pl.BlockSpec((LANES, kw), lambda i: (0, 0)),
                  pl.BlockSpec((1, kw), lambda i: (0, 0)),
                  pl.BlockSpec((1, vw), lambda i: (0, 0))],
        out_specs=pl.BlockSpec((tg, vw), lambda i: (i, 0)),
        out_shape=jax.ShapeDtypeStruct((t, vw), BF16),
        scratch_shapes=[pltpu.VMEM((heads, kw // heads, vw // heads), F32)],
        compiler_params=_params("arbitrary"),
        name="gla",
    )(z, z, z, z, lr, wlr, blr, gn)


def _ln_router_kernel(h_ref, g_ref, b_ref, wr_ref, br_ref, x1_ref, xp_ref, idx_ref, wgt_ref, rank_ref, cnt_ref,
                      base_ref, *, top_k):
    @pl.when(pl.program_id(0) == 0)
    def _():
        base_ref[...] = jnp.zeros_like(base_ref)

    tm, d = h_ref.shape
    half = d // 2
    x1 = _layer_norm(h_ref[...], g_ref[...], b_ref[...])
    x1_ref[...] = x1
    lo = pltpu.bitcast(x1[:, :half].astype(BF16).astype(F32), U32) >> 16
    hi = pltpu.bitcast(x1[:, half:].astype(BF16).astype(F32), U32) & jnp.uint32(HI_MASK)
    xp_ref[...] = lo | hi

    logits = _dot(x1.astype(BF16), wr_ref[...]) + br_ref[...]
    lane = lax.broadcasted_iota(I32, (tm, LANES), 1).astype(F32)
    member = jnp.zeros((tm, LANES), F32)
    vals, idxs = [], []
    cur = logits
    for _ in range(top_k):
        mx = jnp.max(cur, axis=-1, keepdims=True)
        ix = jnp.min(jnp.where(cur == mx, lane, float(LANES)), axis=-1, keepdims=True)
        sel = lane == ix
        vals.append(mx)
        idxs.append(ix)
        member = member + sel.astype(F32)
        cur = jnp.where(sel, -jnp.inf, cur)
    exps = [jnp.exp(v - vals[0]) for v in vals]
    denom = exps[0]
    for e in exps[1:]:
        denom = denom + e
    r_i = lax.broadcasted_iota(I32, (tm, tm), 0)
    c_i = lax.broadcasted_iota(I32, (tm, tm), 1)
    before = _dot((r_i > c_i).astype(BF16), member.astype(BF16)) + base_ref[...]
    idx_out = jnp.zeros((tm, LANES), F32)
    wgt_out = jnp.zeros((tm, LANES), F32)
    rank_out = jnp.zeros((tm, LANES), F32)
    for j in range(top_k):
        rk = jnp.sum(jnp.where(lane == idxs[j], before, 0.0), axis=-1, keepdims=True)
        at = lane == float(j)
        idx_out = jnp.where(at, idxs[j], idx_out)
        wgt_out = jnp.where(at, exps[j] / denom, wgt_out)
        rank_out = jnp.where(at, rk, rank_out)
    idx_ref[...] = idx_out.astype(I32)
    wgt_ref[...] = wgt_out
    rank_ref[...] = rank_out.astype(I32)
    base_ref[...] = base_ref[...] + jnp.sum(member, axis=0, keepdims=True)
    cnt_ref[...] = base_ref[...]


def _ln_router(h1, g, b, wr, br, tm):
    t, d = h1.shape
    tm = _pick(t, tm)
    row = lambda i: (i, 0)
    fixed = lambda i: (0, 0)
    return pl.pallas_call(
        functools.partial(_ln_router_kernel, top_k=TOP_K),
        grid=(t // tm,),
        in_specs=[pl.BlockSpec((tm, d), row),
                  pl.BlockSpec((1, d), fixed),
                  pl.BlockSpec((1, d), fixed),
                  pl.BlockSpec((d, LANES), fixed),
                  pl.BlockSpec((1, LANES), fixed)],
        out_specs=[pl.BlockSpec((tm, d), row),
                   pl.BlockSpec((tm, d // 2), row),
                   pl.BlockSpec((tm, LANES), row),
                   pl.BlockSpec((tm, LANES), row),
                   pl.BlockSpec((tm, LANES), row),
                   pl.BlockSpec((1, LANES), fixed)],
        out_shape=[jax.ShapeDtypeStruct((t, d), F32),
                   jax.ShapeDtypeStruct((t, d // 2), U32),
                   jax.ShapeDtypeStruct((t, LANES), I32),
                   jax.ShapeDtypeStruct((t, LANES), F32),
                   jax.ShapeDtypeStruct((t, LANES), I32),
                   jax.ShapeDtypeStruct((1, LANES), F32)],
        scratch_shapes=[pltpu.VMEM((1, LANES), F32)],
        compiler_params=_params("arbitrary"),
        name="ln_router",
    )(h1, g, b, wr, br)


def _dispatch_kernel(nb_ref, tok_ref, tok_nxt_ref, xp_hbm, xs_ref, buf, sems, *, tm_e):
    b = pl.program_id(0)
    slot = b % 2
    used = b < nb_ref[0]
    stripe = tm_e // DISPATCH_STRIPES

    def row_copy(idx_ref, s, u, k):
        return pltpu.make_async_copy(xp_hbm.at[pl.ds(idx_ref[u * stripe + k], 1)], buf.at[s, u, pl.ds(k, 1)],
                                     sems.at[s])

    def issue(idx_ref, s):
        def body(k, c):
            for u in range(DISPATCH_STRIPES):
                row_copy(idx_ref, s, u, k).start()
            return c
        lax.fori_loop(0, stripe, body, 0)

    @pl.when(b == 0)
    def _():
        issue(tok_ref, slot)

    @pl.when(b + 1 < nb_ref[0])
    def _():
        issue(tok_nxt_ref, 1 - slot)

    @pl.when(jnp.logical_not(used))
    def _():
        xs_ref[...] = jnp.zeros_like(xs_ref)

    @pl.when(used)
    def _():
        def drain(k, c):
            for u in range(DISPATCH_STRIPES):
                row_copy(tok_ref, slot, u, k).wait()
            return c

        lax.fori_loop(0, stripe, drain, 0)
        w = buf[slot].reshape(tm_e, buf.shape[-1])
        half = w.shape[1]
        xs_ref[:, :half] = pltpu.bitcast(w << 16, F32).astype(BF16)
        xs_ref[:, half:] = pltpu.bitcast(w & jnp.uint32(HI_MASK), F32).astype(BF16)


def _dispatch(tok_buf, nb_used, xp, tm_e):
    dh = xp.shape[1]
    n_rows = tok_buf.shape[0]
    last = n_rows // tm_e - 1
    grid_spec = pltpu.PrefetchScalarGridSpec(
        num_scalar_prefetch=1,
        grid=(n_rows // tm_e,),
        in_specs=[pl.BlockSpec((tm_e,), lambda b, nb: (b,), memory_space=pltpu.SMEM),
                  pl.BlockSpec((tm_e,), lambda b, nb: (jnp.minimum(b + 1, last),), memory_space=pltpu.SMEM),
                  pl.BlockSpec(memory_space=pl.ANY)],
        out_specs=pl.BlockSpec((tm_e, 2 * dh), lambda b, nb: (b, 0)),
        scratch_shapes=[pltpu.VMEM((2, DISPATCH_STRIPES, tm_e // DISPATCH_STRIPES, dh), U32),
                        pltpu.SemaphoreType.DMA((2,))],
    )
    return pl.pallas_call(
        functools.partial(_dispatch_kernel, tm_e=tm_e),
        grid_spec=grid_spec,
        out_shape=jax.ShapeDtypeStruct((n_rows, 2 * dh), BF16),
        compiler_params=_params("arbitrary"),
        name="dispatch",
    )(nb_used, tok_buf, tok_buf, xp)


def _expert_changed(be_ref, nb_ref):
    b = pl.program_id(1)
    fresh = jnp.logical_or(b == 0, be_ref[b] != be_ref[jnp.maximum(b - 1, 0)])
    return b < nb_ref[0], fresh


def _stream_expert_weights(be_ref, nxt_ref, wrap_ref, used, fresh, tile_copies, stage, work):
    j, b = pl.program_id(0), pl.program_id(1)
    nj = pl.num_programs(0)

    @pl.when(jnp.logical_and(j == 0, b == 0))
    def _():
        for c in tile_copies(be_ref[0], 0):
            c.start()

    @pl.when(jnp.logical_and(used, fresh))
    def _():
        for c in tile_copies(be_ref[b], j):
            c.wait()
        for src, dst in zip(stage, work):
            dst[...] = src[...].astype(BF16)
        jn = j + wrap_ref[b]

        @pl.when(jn < nj)
        def _():
            for c in tile_copies(nxt_ref[b], jn):
                c.start()


def _for_valid_rows(used, bv_ref, out_ref, compute):
    tm = out_ref.shape[0]
    half = tm // 2
    few = bv_ref[pl.program_id(1)] <= half

    @pl.when(jnp.logical_and(used, jnp.logical_not(few)))
    def _():
        compute(tm)

    @pl.when(jnp.logical_and(used, few))
    def _():
        compute(half)
        out_ref[half:, :] = jnp.zeros((tm - half, out_ref.shape[1]), out_ref.dtype)


def _moe1_kernel(be_ref, nb_ref, nxt_ref, wrap_ref, bv_ref, xs_ref, w_hbm, bg_ref, bl_ref, h_ref, sg, sl, wg_s, wl_s,
                 sems, *, tn):
    used, fresh = _expert_changed(be_ref, nb_ref)
    n_lin = pl.num_programs(0)

    def tile_copies(e, jt):
        return (pltpu.make_async_copy(w_hbm.at[e, :, pl.ds(pl.multiple_of(jt * tn, tn), tn)], sg, sems.at[0]),
                pltpu.make_async_copy(w_hbm.at[e, :, pl.ds(pl.multiple_of((n_lin + jt) * tn, tn), tn)], sl,
                                      sems.at[1]))

    _stream_expert_weights(be_ref, nxt_ref, wrap_ref, used, fresh, tile_copies, (sg, sl), (wg_s, wl_s))

    @pl.when(jnp.logical_not(used))
    def _():
        h_ref[...] = jnp.zeros_like(h_ref)

    def compute(rows):
        x = xs_ref[:rows, :]
        gate = jnp.minimum(_dot(x, wg_s[...]) + bg_ref[...], SWIGLU_LIMIT)
        lin = jnp.clip(_dot(x, wl_s[...]) + bl_ref[...], -SWIGLU_LIMIT, SWIGLU_LIMIT)
        h_ref[:rows, :] = (gate * _sigmoid(SWIGLU_ALPHA * gate) * (lin + 1.0)).astype(h_ref.dtype)

    _for_valid_rows(used, bv_ref, h_ref, compute)


def _moe1(sched, xs, w_gu, b_gu, tm, tn):
    n_rows, dh = xs.shape
    n_exp, d, de2 = w_gu.shape
    de = de2 // 2
    tn = _pick(de, tn)
    nj = de // tn

    def blk(b, nb):
        return jnp.minimum(b, nb[0] - 1)

    grid_spec = pltpu.PrefetchScalarGridSpec(
        num_scalar_prefetch=5,
        grid=(nj, n_rows // tm),
        in_specs=[pl.BlockSpec((tm, dh), lambda j, b, be, nb, *_: (blk(b, nb), 0)),
                  pl.BlockSpec(memory_space=pl.ANY),
                  pl.BlockSpec((None, 1, tn), lambda j, b, be, nb, *_: (be[blk(b, nb)], 0, j)),
                  pl.BlockSpec((None, 1, tn), lambda j, b, be, nb, *_: (be[blk(b, nb)], 0, nj + j))],
        out_specs=pl.BlockSpec((tm, tn), lambda j, b, *_: (b, j)),
        scratch_shapes=[pltpu.VMEM((d, tn), F32), pltpu.VMEM((d, tn), F32),
                        pltpu.VMEM((d, tn), BF16), pltpu.VMEM((d, tn), BF16),
                        pltpu.SemaphoreType.DMA((2,))],
    )
    return pl.pallas_call(
        functools.partial(_moe1_kernel, tn=tn),
        grid_spec=grid_spec,
        out_shape=jax.ShapeDtypeStruct((n_rows, de), BF16),
        compiler_params=_params("arbitrary", "arbitrary"),
        name="moe_up",
    )(*sched, xs, w_gu, b_gu, b_gu)


def _moe2_kernel(be_ref, nb_ref, nxt_ref, wrap_ref, bv_ref, h_ref, w_hbm, b_ref, y_ref, stage, w_s, sems, *, tn):
    used, fresh = _expert_changed(be_ref, nb_ref)

    def tile_copies(e, jt):
        return (pltpu.make_async_copy(w_hbm.at[e, :, pl.ds(pl.multiple_of(jt * tn, tn), tn)], stage, sems.at[0]),)

    _stream_expert_weights(be_ref, nxt_ref, wrap_ref, used, fresh, tile_copies, (stage,), (w_s,))

    @pl.when(jnp.logical_not(used))
    def _():
        y_ref[...] = jnp.zeros_like(y_ref)

    def compute(rows):
        y_ref[:rows, :] = _dot(h_ref[:rows, :], w_s[...]) + b_ref[...]

    _for_valid_rows(used, bv_ref, y_ref, compute)


def _moe2(sched, h, w_down, b_down, tm, tn):
    n_rows, de = h.shape
    n_exp, _, d = w_down.shape
    tn = _pick(d, tn)

    def blk(b, nb):
        return jnp.minimum(b, nb[0] - 1)

    grid_spec = pltpu.PrefetchScalarGridSpec(
        num_scalar_prefetch=5,
        grid=(d // tn, n_rows // tm),
        in_specs=[pl.BlockSpec((tm, de), lambda j, b, be, nb, *_: (blk(b, nb), 0)),
                  pl.BlockSpec(memory_space=pl.ANY),
                  pl.BlockSpec((None, 1, tn), lambda j, b, be, nb, *_: (be[blk(b, nb)], 0, j))],
        out_specs=pl.BlockSpec((tm, tn), lambda j, b, *_: (b, j)),
        scratch_shapes=[pltpu.VMEM((de, tn), F32), pltpu.VMEM((de, tn), BF16), pltpu.SemaphoreType.DMA((1,))],
    )
    return pl.pallas_call(
        functools.partial(_moe2_kernel, tn=tn),
        grid_spec=grid_spec,
        out_shape=jax.ShapeDtypeStruct((n_rows, d), F32),
        compiler_params=_params("arbitrary", "arbitrary"),
        name="moe_down",
    )(*sched, h, w_down, b_down)


def _combine_kernel(dest_ref, dest_nxt_ref, ys_hbm, x1_ref, wgt_ref, g_ref, b_ref, o_ref, buf, sems, *, tc, top_k,
                    alpha):
    i = pl.program_id(0)
    slot = i % 2

    def row_copy(idx_ref, s, t, j):
        return pltpu.make_async_copy(ys_hbm.at[pl.ds(idx_ref[t * top_k + j], 1)], buf.at[s, j, pl.ds(t, 1)],
                                     sems.at[s])

    def issue(idx_ref, s):
        def body(t, c):
            for j in range(top_k):
                row_copy(idx_ref, s, t, j).start()
            return c
        lax.fori_loop(0, tc, body, 0, unroll=2)

    @pl.when(i == 0)
    def _():
        issue(dest_ref, slot)

    @pl.when(i + 1 < pl.num_programs(0))
    def _():
        issue(dest_nxt_ref, 1 - slot)

    def drain(t, c):
        for j in range(top_k):
            row_copy(dest_ref, slot, t, j).wait()
        return c

    lax.fori_loop(0, tc, drain, 0, unroll=2)
    y = wgt_ref[:, 0:1] * buf[slot, 0]
    for j in range(1, top_k):
        y = y + wgt_ref[:, j:j + 1] * buf[slot, j]
    o_ref[...] = _layer_norm(alpha * x1_ref[...] + y, g_ref[...], b_ref[...])


def _combine(dest_flat, ys, x1, wgt, g, b, alpha, tc):
    t, d = x1.shape
    tc = _pick(t, tc)
    last = t // tc - 1
    return pl.pallas_call(
        functools.partial(_combine_kernel, tc=tc, top_k=TOP_K, alpha=alpha),
        grid=(t // tc,),
        in_specs=[pl.BlockSpec((tc * TOP_K,), lambda i: (i,), memory_space=pltpu.SMEM),
                  pl.BlockSpec((tc * TOP_K,), lambda i: (jnp.minimum(i + 1, last),), memory_space=pltpu.SMEM),
                  pl.BlockSpec(memory_space=pl.ANY),
                  pl.BlockSpec((tc, d), lambda i: (i, 0)),
                  pl.BlockSpec((tc, LANES), lambda i: (i, 0)),
                  pl.BlockSpec((1, d), lambda i: (0, 0)),
                  pl.BlockSpec((1, d), lambda i: (0, 0))],
        out_specs=pl.BlockSpec((tc, d), lambda i: (i, 0)),
        out_shape=jax.ShapeDtypeStruct((t, d), F32),
        scratch_shapes=[pltpu.VMEM((2, TOP_K, tc, d), F32), pltpu.SemaphoreType.DMA((2,))],
        compiler_params=_params("arbitrary"),
        name="combine",
    )(dest_flat, dest_flat, ys, x1, wgt, g, b)


def _layer(x2d, w_in, b_in, a_ws, a_bs, a_ln_g, a_ln_b, gla_w_lr, gla_b_lr, gla_gn_g, w_br_a, w_br_b, w_o,
           ln1_g, ln1_b, w_router, b_router, w_gu, b_gu, w_down, b_down, ln2_g, ln2_b, alpha):
    t, d = x2d.shape
    n_groups, a_chunk, _ = a_ws.shape
    aw = a_ln_g.size
    rank, kw = gla_w_lr.shape
    heads, dv = gla_gn_g.shape
    vw = heads * dv
    n_exp = w_router.shape[1]
    de = w_down.shape[1]
    off_lr = 2 * aw + 2 * kw + 2 * vw
    off_q = 2 * aw

    w_mix = w_in.astype(BF16)
    w_gate = w_mix[:, off_lr + rank:]
    w_lr_in = jnp.pad(w_in[:, off_lr:off_lr + rank], ((0, 0), (0, LANES - rank))).astype(BF16)
    b_lr_in = jnp.pad(b_in[off_lr:off_lr + rank], (0, LANES - rank)).reshape(1, LANES)
    wlr = jnp.pad(gla_w_lr, ((0, LANES - rank), (0, 0))).astype(BF16)
    wr = jnp.pad(w_router, ((0, 0), (0, LANES - n_exp))).astype(BF16)
    br = jnp.pad(b_router, (0, LANES - n_exp), constant_values=-jnp.inf).reshape(1, LANES)

    xb = x2d.astype(BF16)
    z = _matmul_bias(xb, w_mix, b_in[:off_lr].reshape(1, -1), F32, 1024, 1024)
    zg = _matmul_bias(xb, w_gate, b_in[off_lr + rank:].reshape(1, -1), F32, 1024, 1024)
    lr = _matmul_bias(xb, w_lr_in, b_lr_in, F32, 2048, LANES)

    ya = _branch_a(z, a_ws, a_bs, a_ln_g, a_ln_b, aw, 2 * a_chunk)
    yb = _gla(z, lr, wlr, gla_b_lr.reshape(1, kw), gla_gn_g.reshape(1, vw), kw, vw, off_q, heads, 256)
    merged = _merge(ya, w_br_a.astype(BF16), yb, w_br_b.astype(BF16), zg, 0, d, 1024, 512)
    h1 = _out_proj_resid(merged, w_o.astype(BF16), x2d, alpha, 1024, 512)

    x1, xp, idx, wgt, rnk, cnt = _ln_router(h1, ln1_g.reshape(1, d), ln1_b.reshape(1, d), wr, br, 256)

    tm_e = 512 if t * TOP_K >= 512 * n_exp else 64
    n_assign = t * TOP_K
    n_blocks = -(-(n_assign + n_exp * (tm_e - 1)) // tm_e)
    n_rows = n_blocks * tm_e
    counts = cnt[0, :n_exp].astype(I32)
    padded = (counts + tm_e - 1) // tm_e * tm_e
    pend = jnp.cumsum(padded)
    pstart = pend - padded
    sel = idx[:, :TOP_K, None] == jnp.arange(n_exp, dtype=I32)[None, None, :]
    dest = jnp.sum(jnp.where(sel, pstart[None, None, :], 0), axis=-1) + rnk[:, :TOP_K]
    dest_flat = dest.reshape(-1).astype(I32)
    nb_used = (pend[-1] // tm_e).astype(I32).reshape(1)
    block_start = jnp.arange(n_blocks, dtype=I32) * tm_e
    block_e = jnp.minimum(jnp.sum((pend[None, :] <= block_start[:, None]).astype(I32), axis=1), n_exp - 1)

    tok_buf = jnp.zeros((n_rows,), I32).at[dest_flat].set(jnp.arange(n_assign, dtype=I32) // TOP_K,
                                                          unique_indices=True)
    xs = _dispatch(tok_buf, nb_used, xp, tm_e)
    experts = jnp.arange(n_exp, dtype=I32)
    later = jnp.where((padded > 0)[None, :] & (experts[None, :] > block_e[:, None]), experts[None, :], n_exp)
    nxt = jnp.min(later, axis=1)
    wrap = (nxt == n_exp).astype(I32)
    nxt_e = jnp.where(nxt == n_exp, block_e[0], nxt).astype(I32)
    mine = experts[None, :] == block_e[:, None]
    group_end = jnp.sum(jnp.where(mine, (pstart + counts)[None, :], 0), axis=1)
    bvalid = jnp.clip(group_end - block_start, 0, tm_e).astype(I32)
    sched = (block_e, nb_used, nxt_e, wrap, bvalid)
    hmid = _moe1(sched, xs, w_gu, b_gu.reshape(n_exp, 1, 2 * de), tm_e, 768)
    ys = _moe2(sched, hmid, w_down, b_down.reshape(n_exp, 1, d), tm_e, 2048)
    return _combine(dest_flat, ys, x1, wgt, ln2_g.reshape(1, d), ln2_b.reshape(1, d), alpha, 128)


def kernel(x, w_in, b_in, a_ws, a_bs, a_ln_g, a_ln_b, gla_w_lr, gla_b_lr, gla_gn_g, w_br_a, w_br_b, w_o, ln1_g, ln1_b, w_router, b_router, w_gu, b_gu, w_down, b_down, ln2_g, ln2_b):
    bsz, seq, d = x.shape
    depth = w_in.shape[0]
    alpha = (2 * depth) ** 0.25
    outs = []
    for bi in range(bsz):
        h = x[bi]
        for l in range(depth):
            h = _layer(h, w_in[l], b_in[l], a_ws[l], a_bs[l], a_ln_g[l], a_ln_b[l], gla_w_lr[l], gla_b_lr[l],
                       gla_gn_g[l], w_br_a[l], w_br_b[l], w_o[l], ln1_g[l], ln1_b[l], w_router[l], b_router[l],
                       w_gu[l], b_gu[l], w_down[l], b_down[l], ln2_g[l], ln2_b[l], alpha)
        outs.append(h)
    return jnp.stack(outs) if bsz > 1 else outs[0][None]
```

```python
import functools

import jax
import jax.numpy as jnp
from jax import lax
from jax.experimental import pallas as pl
from jax.experimental.pallas import tpu as pltpu

F32 = jnp.float32
BF16 = jnp.bfloat16
U32 = jnp.uint32
I32 = jnp.int32

LN_EPS = 1e-5
GLA_CHUNK = 64
GLA_GATE_TAU = 16.0
TOP_K = 4
SWIGLU_LIMIT = 7.0
SWIGLU_ALPHA = 1.702
LANES = 128
VMEM_LIMIT = 56 * 1024 * 1024
HI_MASK = 0xFFFF0000
DISPATCH_STRIPES = 8

def _pick(n, pref):
    t = min(pref, n)
    while n % t:
        t //= 2
    return t


def _dot(a, b):
    return jnp.dot(a, b, preferred_element_type=F32)


def _dot_t0(a, b):
    return lax.dot_general(a, b, (((0,), (0,)), ((), ())), preferred_element_type=F32)


def _dot_t1(a, b):
    return lax.dot_general(a, b, (((1,), (1,)), ((), ())), preferred_element_type=F32)


def _sigmoid(x):
    return 1.0 / (1.0 + jnp.exp(-x))


def _gelu(x):
    return 0.5 * x * (1.0 + lax.erf(x * (2.0 ** -0.5)))


def _layer_norm(x, g, b):
    mu = jnp.mean(x, axis=-1, keepdims=True)
    xc = x - mu
    var = jnp.mean(xc * xc, axis=-1, keepdims=True)
    return xc * lax.rsqrt(var + LN_EPS) * g + b


def _params(*sem):
    return pltpu.CompilerParams(dimension_semantics=sem, vmem_limit_bytes=VMEM_LIMIT)


def _mm_bias_kernel(x_ref, w_ref, b_ref, o_ref):
    o_ref[...] = (_dot(x_ref[...], w_ref[...]) + b_ref[...]).astype(o_ref.dtype)


def _matmul_bias(xb, w, b, out_dtype, tm, tn):
    m, k = xb.shape
    n = b.shape[1]
    tm, tn = _pick(m, tm), _pick(n, tn)
    return pl.pallas_call(
        _mm_bias_kernel,
        grid=(m // tm, n // tn),
        in_specs=[pl.BlockSpec((tm, k), lambda i, j: (i, 0)),
                  pl.BlockSpec((k, tn), lambda i, j: (0, j)),
                  pl.BlockSpec((1, tn), lambda i, j: (0, j))],
        out_specs=pl.BlockSpec((tm, tn), lambda i, j: (i, j)),
        out_shape=jax.ShapeDtypeStruct((m, n), out_dtype),
        compiler_params=_params("parallel", "parallel"),
        name="in_proj",
    )(xb, w, b)


def _mm_stream_kernel(x_ref, w_hbm, b_ref, o_ref, stage, work, sem, *, tn):
    j, i = pl.program_id(0), pl.program_id(1)

    def w_copy(jt):
        return pltpu.make_async_copy(w_hbm.at[:, pl.ds(pl.multiple_of(jt * tn, tn), tn)], stage, sem.at[0])

    @pl.when(jnp.logical_and(j == 0, i == 0))
    def _():
        w_copy(0).start()

    @pl.when(i == 0)
    def _():
        w_copy(j).wait()
        work[...] = stage[...].astype(BF16)

        @pl.when(j + 1 < pl.num_programs(0))
        def _():
            w_copy(j + 1).start()

    o_ref[...] = (_dot(x_ref[...], work[...]) + b_ref[...]).astype(o_ref.dtype)


def _matmul_bias_stream(xb, w, b, out_dtype, tm, tn):
    m, k = xb.shape
    n = b.shape[1]
    tm, tn = _pick(m, tm), _pick(n, tn)
    return pl.pallas_call(
        functools.partial(_mm_stream_kernel, tn=tn),
        grid=(n // tn, m // tm),
        in_specs=[pl.BlockSpec((tm, k), lambda j, i: (i, 0)),
                  pl.BlockSpec(memory_space=pl.ANY),
                  pl.BlockSpec((1, tn), lambda j, i: (0, j))],
        out_specs=pl.BlockSpec((tm, tn), lambda j, i: (i, j)),
        out_shape=jax.ShapeDtypeStruct((m, n), out_dtype),
        scratch_shapes=[pltpu.VMEM((k, tn), F32), pltpu.VMEM((k, tn), BF16), pltpu.SemaphoreType.DMA((1,))],
        compiler_params=_params("arbitrary", "arbitrary"),
        name="in_proj",
    )(xb, w, b)


def _merge_kernel(ya_ref, wa_ref, yb_ref, wb_ref, ga_ref, gb_ref, o_ref):
    a = _dot(ya_ref[...], wa_ref[...])
    b = _dot(yb_ref[...], wb_ref[...])
    o_ref[...] = (_sigmoid(ga_ref[...]) * a + _sigmoid(gb_ref[...]) * b).astype(o_ref.dtype)


def _merge(ya, wa, yb, wb, z, off_ga, off_gb, tm, tn):
    m, ka = ya.shape
    kb = yb.shape[1]
    n = wa.shape[1]
    tm, tn = _pick(m, tm), _pick(n, tn)
    while off_ga % tn or off_gb % tn:
        tn //= 2
    oa, ob = off_ga // tn, off_gb // tn
    return pl.pallas_call(
        _merge_kernel,
        grid=(m // tm, n // tn),
        in_specs=[pl.BlockSpec((tm, ka), lambda i, j: (i, 0)),
                  pl.BlockSpec((ka, tn), lambda i, j: (0, j)),
                  pl.BlockSpec((tm, kb), lambda i, j: (i, 0)),
                  pl.BlockSpec((kb, tn), lambda i, j: (0, j)),
                  pl.BlockSpec((tm, tn), lambda i, j: (i, oa + j)),
                  pl.BlockSpec((tm, tn), lambda i, j: (i, ob + j))],
        out_specs=pl.BlockSpec((tm, tn), lambda i, j: (i, j)),
        out_shape=jax.ShapeDtypeStruct((m, n), BF16),
        compiler_params=_params("parallel", "parallel"),
        name="merge",
    )(ya, wa, yb, wb, z, z)


def _mm_resid_kernel(m_ref, w_ref, x_ref, o_ref, *, alpha):
    o_ref[...] = alpha * x_ref[...] + _dot(m_ref[...], w_ref[...])


def _out_proj_resid(mb, w, x, alpha, tm, tn):
    m, k = mb.shape
    n = w.shape[1]
    tm, tn = _pick(m, tm), _pick(n, tn)
    return pl.pallas_call(
        functools.partial(_mm_resid_kernel, alpha=alpha),
        grid=(m // tm, n // tn),
        in_specs=[pl.BlockSpec((tm, k), lambda i, j: (i, 0)),
                  pl.BlockSpec((k, tn), lambda i, j: (0, j)),
                  pl.BlockSpec((tm, tn), lambda i, j: (i, j))],
        out_specs=pl.BlockSpec((tm, tn), lambda i, j: (i, j)),
        out_shape=jax.ShapeDtypeStruct((m, n), F32),
        compiler_params=_params("parallel", "parallel"),
        name="out_proj",
    )(mb, w, x)


def _branch_a_kernel(u_ref, v_ref, ws_ref, bs_ref, g_ref, b_ref, o_ref, *, n_groups, chunk):
    tm, aw = u_ref.shape
    gd = aw // n_groups
    row = lax.broadcasted_iota(I32, (chunk, chunk), 0)
    col = lax.broadcasted_iota(I32, (chunk, chunk), 1)
    causal = row >= col
    for g in range(n_groups):
        ws = jnp.where(causal, ws_ref[g], 0.0).astype(BF16)
        cs = slice(g * gd, (g + 1) * gd)
        for c in range(tm // chunk):
            rs = slice(c * chunk, (c + 1) * chunk)
            vn = _layer_norm(_gelu(v_ref[rs, cs]), g_ref[:, cs], b_ref[:, cs])
            mixed = _dot(ws, vn.astype(BF16)) + bs_ref[:, g:g + 1]
            o_ref[rs, cs] = (_gelu(u_ref[rs, cs]) * mixed).astype(o_ref.dtype)


def _branch_a(z, a_ws, a_bs, a_ln_g, a_ln_b, aw, tm):
    t = z.shape[0]
    n_groups, chunk, _ = a_ws.shape
    tm = max(_pick(t, tm), chunk)
    return pl.pallas_call(
        functools.partial(_branch_a_kernel, n_groups=n_groups, chunk=chunk),
        grid=(t // tm,),
        in_specs=[pl.BlockSpec((tm, aw), lambda i: (i, 0)),
                  pl.BlockSpec((tm, aw), lambda i: (i, 1)),
                  pl.BlockSpec((n_groups, chunk, chunk), lambda i: (0, 0, 0)),
                  pl.BlockSpec((chunk, n_groups), lambda i: (0, 0)),
                  pl.BlockSpec((1, aw), lambda i: (0, 0)),
                  pl.BlockSpec((1, aw), lambda i: (0, 0))],
        out_specs=pl.BlockSpec((tm, aw), lambda i: (i, 0)),
        out_shape=jax.ShapeDtypeStruct((t, aw), BF16),
        compiler_params=_params("parallel"),
        name="branch_a",
    )(z, z, a_ws, a_bs.T, a_ln_g.reshape(1, aw), a_ln_b.reshape(1, aw))


def _gla_kernel(q_ref, k_ref, v_ref, r_ref, lr_ref, wlr_ref, blr_ref, gn_ref, o_ref, s_ref, *, heads, chunk):
    @pl.when(pl.program_id(0) == 0)
    def _():
        s_ref[...] = jnp.zeros_like(s_ref)

    tg, kw = q_ref.shape
    vw = v_ref.shape[1]
    dk, dv = kw // heads, vw // heads
    row = lax.broadcasted_iota(I32, (chunk, chunk), 0)
    col = lax.broadcasted_iota(I32, (chunk, chunk), 1)
    causal = row >= col
    tri = causal.astype(BF16)
    ones = jnp.ones((chunk, LANES), BF16)
    scale = dk ** -0.5

    def body(c, carry):
        rs = pl.ds(pl.multiple_of(c * chunk, chunk), chunk)
        gl = _dot(lr_ref[rs, :].astype(BF16), wlr_ref[...]) + blr_ref[...]
        la = (jnp.minimum(gl, 0.0) - jnp.log1p(jnp.exp(-jnp.abs(gl)))) * (1.0 / GLA_GATE_TAU)
        la_hi = la.astype(BF16)
        la_lo = (la - la_hi.astype(F32)).astype(BF16)
        cum = _dot(tri, la_hi) + _dot(tri, la_lo)
        tot = cum[chunk - 1:chunk, :]
        tot_col = _dot_t0(la_hi, ones) + _dot_t0(la_lo, ones)
        dec_col = jnp.exp(tot_col)
        q = q_ref[rs, :] * scale
        k = k_ref[rs, :]
        qt = (q * jnp.exp(cum)).astype(BF16)
        kt = (k * jnp.exp(-cum)).astype(BF16)
        kl = (k * jnp.exp(tot - cum)).astype(BF16)
        for h in range(heads):
            ks = slice(h * dk, (h + 1) * dk)
            vs = slice(h * dv, (h + 1) * dv)
            vh = v_ref[rs, vs].astype(BF16)
            sc = jnp.where(causal, _dot_t1(qt[:, ks], kt[:, ks]), 0.0).astype(BF16)
            state = s_ref[h]
            o = _dot(sc, vh) + _dot(qt[:, ks], state.astype(BF16))
            dec = jnp.concatenate([dec_col[ks, :]] * (dv // LANES), axis=1) if dv >= LANES else dec_col[ks, :dv]
            s_ref[h] = dec * state + _dot_t0(kl[:, ks], vh)
            on = o * lax.rsqrt(jnp.mean(o * o, axis=-1, keepdims=True) + LN_EPS) * gn_ref[:, vs]
            rr = r_ref[rs, vs]
            o_ref[rs, vs] = (on * (rr * _sigmoid(rr))).astype(o_ref.dtype)
        return carry

    lax.fori_loop(0, tg // chunk, body, 0)


def _gla(z, lr, wlr, blr, gn, kw, vw, off_q, heads, tg):
    t = z.shape[0]
    tg = max(_pick(t, tg), GLA_CHUNK)
    oq = off_q // kw
    ok = oq + 1
    ov = (off_q + 2 * kw) // vw
    orr = ov + 1
    return pl.pallas_call(
        functools.partial(_gla_kernel, heads=heads, chunk=GLA_CHUNK),
        grid=(t // tg,),
        in_specs=[pl.BlockSpec((tg, kw), lambda i: (i, oq)),
                  pl.BlockSpec((tg, kw), lambda i: (i, ok)),
                  pl.BlockSpec((tg, vw), lambda i: (i, ov)),
                  pl.BlockSpec((tg, vw), lambda i: (i, orr)),
                  pl.BlockSpec((tg, LANES), lambda i: (i, 0)),
                  pl.BlockSpec((LANES, kw), lambda i: (0, 0)),
                  pl.BlockSpec((1, kw), lambda i: (0, 0)),
                  pl.BlockSpec((1, vw), lambda i: (0, 0))],
        out_specs=pl.BlockSpec((tg, vw), lambda i: (i, 0)),
        out_shape=jax.ShapeDtypeStruct((t, vw), BF16),
        scratch_shapes=[pltpu.VMEM((heads, kw // heads, vw // heads), F32)],
        compiler_params=_params("arbitrary"),
        name="gla",
    )(z, z, z, z, lr, wlr, blr, gn)


def _ln_router_kernel(h_ref, g_ref, b_ref, wr_ref, br_ref, x1_ref, xp_ref, idx_ref, wgt_ref, rank_ref, cnt_ref,
                      base_ref, *, top_k):
    @pl.when(pl.program_id(0) == 0)
    def _():
        base_ref[...] = jnp.zeros_like(base_ref)

    tm, d = h_ref.shape
    half = d // 2
    x1 = _layer_norm(h_ref[...], g_ref[...], b_ref[...])
    x1_ref[...] = x1
    lo = pltpu.bitcast(x1[:, :half].astype(BF16).astype(F32), U32) >> 16
    hi = pltpu.bitcast(x1[:, half:].astype(BF16).astype(F32), U32) & jnp.uint32(HI_MASK)
    xp_ref[...] = lo | hi

    logits = _dot(x1.astype(BF16), wr_ref[...]) + br_ref[...]
    lane = lax.broadcasted_iota(I32, (tm, LANES), 1).astype(F32)
    member = jnp.zeros((tm, LANES), F32)
    vals, idxs = [], []
    cur = logits
    for _ in range(top_k):
        mx = jnp.max(cur, axis=-1, keepdims=True)
        ix = jnp.min(jnp.where(cur == mx, lane, float(LANES)), axis=-1, keepdims=True)
        sel = lane == ix
        vals.append(mx)
        idxs.append(ix)
        member = member + sel.astype(F32)
        cur = jnp.where(sel, -jnp.inf, cur)
    exps = [jnp.exp(v - vals[0]) for v in vals]
    denom = exps[0]
    for e in exps[1:]:
        denom = denom + e
    r_i = lax.broadcasted_iota(I32, (tm, tm), 0)
    c_i = lax.broadcasted_iota(I32, (tm, tm), 1)
    before = _dot((r_i > c_i).astype(BF16), member.astype(BF16)) + base_ref[...]
    idx_out = jnp.zeros((tm, LANES), F32)
    wgt_out = jnp.zeros((tm, LANES), F32)
    rank_out = jnp.zeros((tm, LANES), F32)
    for j in range(top_k):
        rk = jnp.sum(jnp.where(lane == idxs[j], before, 0.0), axis=-1, keepdims=True)
        at = lane == float(j)
        idx_out = jnp.where(at, idxs[j], idx_out)
        wgt_out = jnp.where(at, exps[j] / denom, wgt_out)
        rank_out = jnp.where(at, rk, rank_out)
    idx_ref[...] = idx_out.astype(I32)
    wgt_ref[...] = wgt_out
    rank_ref[...] = rank_out.astype(I32)
    base_ref[...] = base_ref[...] + jnp.sum(member, axis=0, keepdims=True)
    cnt_ref[...] = base_ref[...]


def _ln_router(h1, g, b, wr, br, tm):
    t, d = h1.shape
    tm = _pick(t, tm)
    row = lambda i: (i, 0)
    fixed = lambda i: (0, 0)
    return pl.pallas_call(
        functools.partial(_ln_router_kernel, top_k=TOP_K),
        grid=(t // tm,),
        in_specs=[pl.BlockSpec((tm, d), row),
                  pl.BlockSpec((1, d), fixed),
                  pl.BlockSpec((1, d), fixed),
                  pl.BlockSpec((d, LANES), fixed),
                  pl.BlockSpec((1, LANES), fixed)],
        out_specs=[pl.BlockSpec((tm, d), row),
                   pl.BlockSpec((tm, d // 2), row),
                   pl.BlockSpec((tm, LANES), row),
                   pl.BlockSpec((tm, LANES), row),
                   pl.BlockSpec((tm, LANES), row),
                   pl.BlockSpec((1, LANES), fixed)],
        out_shape=[jax.ShapeDtypeStruct((t, d), F32),
                   jax.ShapeDtypeStruct((t, d // 2), U32),
                   jax.ShapeDtypeStruct((t, LANES), I32),
                   jax.ShapeDtypeStruct((t, LANES), F32),
                   jax.ShapeDtypeStruct((t, LANES), I32),
                   jax.ShapeDtypeStruct((1, LANES), F32)],
        scratch_shapes=[pltpu.VMEM((1, LANES), F32)],
        compiler_params=_params("arbitrary"),
        name="ln_router",
    )(h1, g, b, wr, br)


def _dispatch_kernel(nb_ref, tok_ref, tok_nxt_ref, xp_hbm, xs_ref, buf, sems, *, tm_e):
    b = pl.program_id(0)
    slot = b % 2
    used = b < nb_ref[0]
    stripe = tm_e // DISPATCH_STRIPES

    def row_copy(idx_ref, s, u, k):
        return pltpu.make_async_copy(xp_hbm.at[pl.ds(idx_ref[u * stripe + k], 1)], buf.at[s, u, pl.ds(k, 1)],
                                     sems.at[s])

    def issue(idx_ref, s):
        def body(k, c):
            for u in range(DISPATCH_STRIPES):
                row_copy(idx_ref, s, u, k).start()
            return c
        lax.fori_loop(0, stripe, body, 0)

    @pl.when(b == 0)
    def _():
        issue(tok_ref, slot)

    @pl.when(b + 1 < nb_ref[0])
    def _():
        issue(tok_nxt_ref, 1 - slot)

    @pl.when(jnp.logical_not(used))
    def _():
        xs_ref[...] = jnp.zeros_like(xs_ref)

    @pl.when(used)
    def _():
        def drain(k, c):
            for u in range(DISPATCH_STRIPES):
                row_copy(tok_ref, slot, u, k).wait()
            return c

        lax.fori_loop(0, stripe, drain, 0)
        w = buf[slot].reshape(tm_e, buf.shape[-1])
        half = w.shape[1]
        xs_ref[:, :half] = pltpu.bitcast(w << 16, F32).astype(BF16)
        xs_ref[:, half:] = pltpu.bitcast(w & jnp.uint32(HI_MASK), F32).astype(BF16)


def _dispatch(tok_buf, nb_used, xp, tm_e):
    dh = xp.shape[1]
    n_rows = tok_buf.shape[0]
    last = n_rows // tm_e - 1
    grid_spec = pltpu.PrefetchScalarGridSpec(
        num_scalar_prefetch=1,
        grid=(n_rows // tm_e,),
        in_specs=[pl.BlockSpec((tm_e,), lambda b, nb: (b,), memory_space=pltpu.SMEM),
                  pl.BlockSpec((tm_e,), lambda b, nb: (jnp.minimum(b + 1, last),), memory_space=pltpu.SMEM),
                  pl.BlockSpec(memory_space=pl.ANY)],
        out_specs=pl.BlockSpec((tm_e, 2 * dh), lambda b, nb: (b, 0)),
        scratch_shapes=[pltpu.VMEM((2, DISPATCH_STRIPES, tm_e // DISPATCH_STRIPES, dh), U32),
                        pltpu.SemaphoreType.DMA((2,))],
    )
    return pl.pallas_call(
        functools.partial(_dispatch_kernel, tm_e=tm_e),
        grid_spec=grid_spec,
        out_shape=jax.ShapeDtypeStruct((n_rows, 2 * dh), BF16),
        compiler_params=_params("arbitrary"),
        name="dispatch",
    )(nb_used, tok_buf, tok_buf, xp)


def _expert_changed(be_ref, nb_ref):
    b = pl.program_id(1)
    fresh = jnp.logical_or(b == 0, be_ref[b] != be_ref[jnp.maximum(b - 1, 0)])
    return b < nb_ref[0], fresh


def _stream_expert_weights(be_ref, nxt_ref, wrap_ref, used, fresh, tile_copies, stage, work):
    j, b = pl.program_id(0), pl.program_id(1)
    nj = pl.num_programs(0)

    @pl.when(jnp.logical_and(j == 0, b == 0))
    def _():
        for c in tile_copies(be_ref[0], 0):
            c.start()

    @pl.when(jnp.logical_and(used, fresh))
    def _():
        for c in tile_copies(be_ref[b], j):
            c.wait()
        for src, dst in zip(stage, work):
            dst[...] = src[...].astype(BF16)
        jn = j + wrap_ref[b]

        @pl.when(jn < nj)
        def _():
            for c in tile_copies(nxt_ref[b], jn):
                c.start()


def _for_valid_rows(used, bv_ref, out_ref, compute):
    tm = out_ref.shape[0]
    half = tm // 2
    few = bv_ref[pl.program_id(1)] <= half

    @pl.when(jnp.logical_and(used, jnp.logical_not(few)))
    def _():
        compute(tm)

    @pl.when(jnp.logical_and(used, few))
    def _():
        compute(half)
        out_ref[half:, :] = jnp.zeros((tm - half, out_ref.shape[1]), out_ref.dtype)


def _moe1_kernel(be_ref, nb_ref, nxt_ref, wrap_ref, bv_ref, xs_ref, w_hbm, bg_ref, bl_ref, h_ref, sg, sl, wg_s, wl_s,
                 sems, *, tn):
    used, fresh = _expert_changed(be_ref, nb_ref)
    n_lin = pl.num_programs(0)

    def tile_copies(e, jt):
        return (pltpu.make_async_copy(w_hbm.at[e, :, pl.ds(pl.multiple_of(jt * tn, tn), tn)], sg, sems.at[0]),
                pltpu.make_async_copy(w_hbm.at[e, :, pl.ds(pl.multiple_of((n_lin + jt) * tn, tn), tn)], sl,
                                      sems.at[1]))

    _stream_expert_weights(be_ref, nxt_ref, wrap_ref, used, fresh, tile_copies, (sg, sl), (wg_s, wl_s))

    @pl.when(jnp.logical_not(used))
    def _():
        h_ref[...] = jnp.zeros_like(h_ref)

    def compute(rows):
        x = xs_ref[:rows, :]
        gate = jnp.minimum(_dot(x, wg_s[...]) + bg_ref[...], SWIGLU_LIMIT)
        lin = jnp.clip(_dot(x, wl_s[...]) + bl_ref[...], -SWIGLU_LIMIT, SWIGLU_LIMIT)
        h_ref[:rows, :] = (gate * _sigmoid(SWIGLU_ALPHA * gate) * (lin + 1.0)).astype(h_ref.dtype)

    _for_valid_rows(used, bv_ref, h_ref, compute)


def _moe1(sched, xs, w_gu, b_gu, tm, tn):
    n_rows, dh = xs.shape
    n_exp, d, de2 = w_gu.shape
    de = de2 // 2
    tn = _pick(de, tn)
    nj = de // tn

    def blk(b, nb):
        return jnp.minimum(b, nb[0] - 1)

    grid_spec = pltpu.PrefetchScalarGridSpec(
        num_scalar_prefetch=5,
        grid=(nj, n_rows // tm),
        in_specs=[pl.BlockSpec((tm, dh), lambda j, b, be, nb, *_: (blk(b, nb), 0)),
                  pl.BlockSpec(memory_space=pl.ANY),
                  pl.BlockSpec((None, 1, tn), lambda j, b, be, nb, *_: (be[blk(b, nb)], 0, j)),
                  pl.BlockSpec((None, 1, tn), lambda j, b, be, nb, *_: (be[blk(b, nb)], 0, nj + j))],
        out_specs=pl.BlockSpec((tm, tn), lambda j, b, *_: (b, j)),
        scratch_shapes=[pltpu.VMEM((d, tn), F32), pltpu.VMEM((d, tn), F32),
                        pltpu.VMEM((d, tn), BF16), pltpu.VMEM((d, tn), BF16),
                        pltpu.SemaphoreType.DMA((2,))],
    )
    return pl.pallas_call(
        functools.partial(_moe1_kernel, tn=tn),
        grid_spec=grid_spec,
        out_shape=jax.ShapeDtypeStruct((n_rows, de), BF16),
        compiler_params=_params("arbitrary", "arbitrary"),
        name="moe_up",
    )(*sched, xs, w_gu, b_gu, b_gu)


def _moe2_kernel(be_ref, nb_ref, nxt_ref, wrap_ref, bv_ref, h_ref, w_hbm, b_ref, y_ref, stage, w_s, sems, *, tn):
    used, fresh = _expert_changed(be_ref, nb_ref)

    def tile_copies(e, jt):
        return (pltpu.make_async_copy(w_hbm.at[e, :, pl.ds(pl.multiple_of(jt * tn, tn), tn)], stage, sems.at[0]),)

    _stream_expert_weights(be_ref, nxt_ref, wrap_ref, used, fresh, tile_copies, (stage,), (w_s,))

    @pl.when(jnp.logical_not(used))
    def _():
        y_ref[...] = jnp.zeros_like(y_ref)

    def compute(rows):
        y_ref[:rows, :] = _dot(h_ref[:rows, :], w_s[...]) + b_ref[...]

    _for_valid_rows(used, bv_ref, y_ref, compute)


def _moe2(sched, h, w_down, b_down, tm, tn):
    n_rows, de = h.shape
    n_exp, _, d = w_down.shape
    tn = _pick(d, tn)

    def blk(b, nb):
        return jnp.minimum(b, nb[0] - 1)

    grid_spec = pltpu.PrefetchScalarGridSpec(
        num_scalar_prefetch=5,
        grid=(d // tn, n_rows // tm),
        in_specs=[pl.BlockSpec((tm, de), lambda j, b, be, nb, *_: (blk(b, nb), 0)),
                  pl.BlockSpec(memory_space=pl.ANY),
                  pl.BlockSpec((None, 1, tn), lambda j, b, be, nb, *_: (be[blk(b, nb)], 0, j))],
        out_specs=pl.BlockSpec((tm, tn), lambda j, b, *_: (b, j)),
        scratch_shapes=[pltpu.VMEM((de, tn), F32), pltpu.VMEM((de, tn), BF16), pltpu.SemaphoreType.DMA((1,))],
    )
    return pl.pallas_call(
        functools.partial(_moe2_kernel, tn=tn),
        grid_spec=grid_spec,
        out_shape=jax.ShapeDtypeStruct((n_rows, d), F32),
        compiler_params=_params("arbitrary", "arbitrary"),
        name="moe_down",
    )(*sched, h, w_down, b_down)


def _combine_kernel(dest_ref, dest_nxt_ref, ys_hbm, x1_ref, wgt_ref, g_ref, b_ref, o_ref, buf, sems, *, tc, top_k,
                    alpha):
    i = pl.program_id(0)
    slot = i % 2

    def row_copy(idx_ref, s, t, j):
        return pltpu.make_async_copy(ys_hbm.at[pl.ds(idx_ref[t * top_k + j], 1)], buf.at[s, j, pl.ds(t, 1)],
                                     sems.at[s])

    def issue(idx_ref, s):
        def body(t, c):
            for j in range(top_k):
                row_copy(idx_ref, s, t, j).start()
            return c
        lax.fori_loop(0, tc, body, 0, unroll=2)

    @pl.when(i == 0)
    def _():
        issue(dest_ref, slot)

    @pl.when(i + 1 < pl.num_programs(0))
    def _():
        issue(dest_nxt_ref, 1 - slot)

    def drain(t, c):
        for j in range(top_k):
            row_copy(dest_ref, slot, t, j).wait()
        return c

    lax.fori_loop(0, tc, drain, 0, unroll=2)
    y = wgt_ref[:, 0:1] * buf[slot, 0]
    for j in range(1, top_k):
        y = y + wgt_ref[:, j:j + 1] * buf[slot, j]
    o_ref[...] = _layer_norm(alpha * x1_ref[...] + y, g_ref[...], b_ref[...])


def _combine(dest_flat, ys, x1, wgt, g, b, alpha, tc):
    t, d = x1.shape
    tc = _pick(t, tc)
    last = t // tc - 1
    return pl.pallas_call(
        functools.partial(_combine_kernel, tc=tc, top_k=TOP_K, alpha=alpha),
        grid=(t // tc,),
        in_specs=[pl.BlockSpec((tc * TOP_K,), lambda i: (i,), memory_space=pltpu.SMEM),
                  pl.BlockSpec((tc * TOP_K,), lambda i: (jnp.minimum(i + 1, last),), memory_space=pltpu.SMEM),
                  pl.BlockSpec(memory_space=pl.ANY),
                  pl.BlockSpec((tc, d), lambda i: (i, 0)),
                  pl.BlockSpec((tc, LANES), lambda i: (i, 0)),
                  pl.BlockSpec((1, d), lambda i: (0, 0)),
                  pl.BlockSpec((1, d), lambda i: (0, 0))],
        out_specs=pl.BlockSpec((tc, d), lambda i: (i, 0)),
        out_shape=jax.ShapeDtypeStruct((t, d), F32),
        scratch_shapes=[pltpu.VMEM((2, TOP_K, tc, d), F32), pltpu.SemaphoreType.DMA((2,))],
        compiler_params=_params("arbitrary"),
        name="combine",
    )(dest_flat, dest_flat, ys, x1, wgt, g, b)


def _layer(x2d, w_in, b_in, a_ws, a_bs, a_ln_g, a_ln_b, gla_w_lr, gla_b_lr, gla_gn_g, w_br_a, w_br_b, w_o,
           ln1_g, ln1_b, w_router, b_router, w_gu, b_gu, w_down, b_down, ln2_g, ln2_b, alpha):
    t, d = x2d.shape
    n_groups, a_chunk, _ = a_ws.shape
    aw = a_ln_g.size
    rank, kw = gla_w_lr.shape
    heads, dv = gla_gn_g.shape
    vw = heads * dv
    n_exp = w_router.shape[1]
    de = w_down.shape[1]
    off_lr = 2 * aw + 2 * kw + 2 * vw
    off_q = 2 * aw

    w_gate = w_in[:, off_lr + rank:]
    w_lr_in = jnp.pad(w_in[:, off_lr:off_lr + rank], ((0, 0), (0, LANES - rank))).astype(BF16)
    b_lr_in = jnp.pad(b_in[off_lr:off_lr + rank], (0, LANES - rank)).reshape(1, LANES)
    wlr = jnp.pad(gla_w_lr, ((0, LANES - rank), (0, 0))).astype(BF16)
    wr = jnp.pad(w_router, ((0, 0), (0, LANES - n_exp))).astype(BF16)
    br = jnp.pad(b_router, (0, LANES - n_exp), constant_values=-jnp.inf).reshape(1, LANES)

    xb = x2d.astype(BF16)
    z = _matmul_bias_stream(xb, w_in, b_in[:off_lr].reshape(1, -1), F32, 1024, 1024)
    zg = _matmul_bias_stream(xb, w_gate, b_in[off_lr + rank:].reshape(1, -1), F32, 1024, 1024)
    lr = _matmul_bias(xb, w_lr_in, b_lr_in, F32, 2048, LANES)

    ya = _branch_a(z, a_ws, a_bs, a_ln_g, a_ln_b, aw, 2 * a_chunk)
    yb = _gla(z, lr, wlr, gla_b_lr.reshape(1, kw), gla_gn_g.reshape(1, vw), kw, vw, off_q, heads, 256)
    merged = _merge(ya, w_br_a.astype(BF16), yb, w_br_b.astype(BF16), zg, 0, d, 1024, 512)
    h1 = _out_proj_resid(merged, w_o.astype(BF16), x2d, alpha, 1024, 512)

    x1, xp, idx, wgt, rnk, cnt = _ln_router(h1, ln1_g.reshape(1, d), ln1_b.reshape(1, d), wr, br, 256)

    tm_e = 512 if t * TOP_K >= 512 * n_exp else 64
    n_assign = t * TOP_K
    n_blocks = -(-(n_assign + n_exp * (tm_e - 1)) // tm_e)
    n_rows = n_blocks * tm_e
    counts = cnt[0, :n_exp].astype(I32)
    padded = (counts + tm_e - 1) // tm_e * tm_e
    pend = jnp.cumsum(padded)
    pstart = pend - padded
    sel = idx[:, :TOP_K, None] == jnp.arange(n_exp, dtype=I32)[None, None, :]
    dest = jnp.sum(jnp.where(sel, pstart[None, None, :], 0), axis=-1) + rnk[:, :TOP_K]
    dest_flat = dest.reshape(-1).astype(I32)
    nb_used = (pend[-1] // tm_e).astype(I32).reshape(1)
    block_start = jnp.arange(n_blocks, dtype=I32) * tm_e
    block_e = jnp.minimum(jnp.sum((pend[None, :] <= block_start[:, None]).astype(I32), axis=1), n_exp - 1)

    tok_buf = jnp.zeros((n_rows,), I32).at[dest_flat].set(jnp.arange(n_assign, dtype=I32) // TOP_K,
                                                          unique_indices=True)
    xs = _dispatch(tok_buf, nb_used, xp, tm_e)
    experts = jnp.arange(n_exp, dtype=I32)
    later = jnp.where((padded > 0)[None, :] & (experts[None, :] > block_e[:, None]), experts[None, :], n_exp)
    nxt = jnp.min(later, axis=1)
    wrap = (nxt == n_exp).astype(I32)
    nxt_e = jnp.where(nxt == n_exp, block_e[0], nxt).astype(I32)
    mine = experts[None, :] == block_e[:, None]
    group_end = jnp.sum(jnp.where(mine, (pstart + counts)[None, :], 0), axis=1)
    bvalid = jnp.clip(group_end - block_start, 0, tm_e).astype(I32)
    sched = (block_e, nb_used, nxt_e, wrap, bvalid)
    hmid = _moe1(sched, xs, w_gu, b_gu.reshape(n_exp, 1, 2 * de), tm_e, 768)
    ys = _moe2(sched, hmid, w_down, b_down.reshape(n_exp, 1, d), tm_e, 2048)
    return _combine(dest_flat, ys, x1, wgt, ln2_g.reshape(1, d), ln2_b.reshape(1, d), alpha, 128)


def kernel(x, w_in, b_in, a_ws, a_bs, a_ln_g, a_ln_b, gla_w_lr, gla_b_lr, gla_gn_g, w_br_a, w_br_b, w_o, ln1_g, ln1_b, w_router, b_router, w_gu, b_gu, w_down, b_down, ln2_g, ln2_b):
    bsz, seq, d = x.shape
    depth = w_in.shape[0]
    alpha = (2 * depth) ** 0.25
    outs = []
    for bi in range(bsz):
        h = x[bi]
        for l in range(depth):
            h = _layer(h, w_in[l], b_in[l], a_ws[l], a_bs[l], a_ln_g[l], a_ln_b[l], gla_w_lr[l], gla_b_lr[l],
                       gla_gn_g[l], w_br_a[l], w_br_b[l], w_o[l], ln1_g[l], ln1_b[l], w_router[l], b_router[l],
                       w_gu[l], b_gu[l], w_down[l], b_down[l], ln2_g[l], ln2_b[l], alpha)
        outs.append(h)
    return jnp.stack(outs) if bsz > 1 else outs[0][None]
```

```python
import functools

import jax
import jax.numpy as jnp
from jax import lax
from jax.experimental import pallas as pl
from jax.experimental.pallas import tpu as pltpu

F32 = jnp.float32
BF16 = jnp.bfloat16
U32 = jnp.uint32
I32 = jnp.int32

LN_EPS = 1e-5
GLA_CHUNK = 64
GLA_GATE_TAU = 16.0
TOP_K = 4
SWIGLU_LIMIT = 7.0
SWIGLU_ALPHA = 1.702
LANES = 128
VMEM_LIMIT = 56 * 1024 * 1024
HI_MASK = 0xFFFF0000
DISPATCH_STRIPES = 8

def _pick(n, pref):
    t = min(pref, n)
    while n % t:
        t //= 2
    return t


def _dot(a, b):
    return jnp.dot(a, b, preferred_element_type=F32)


def _dot_t0(a, b):
    return lax.dot_general(a, b, (((0,), (0,)), ((), ())), preferred_element_type=F32)


def _dot_t1(a, b):
    return lax.dot_general(a, b, (((1,), (1,)), ((), ())), preferred_element_type=F32)


def _sigmoid(x):
    return 1.0 / (1.0 + jnp.exp(-x))


def _gelu(x):
    return 0.5 * x * (1.0 + lax.erf(x * (2.0 ** -0.5)))


def _layer_norm(x, g, b):
    mu = jnp.mean(x, axis=-1, keepdims=True)
    xc = x - mu
    var = jnp.mean(xc * xc, axis=-1, keepdims=True)
    return xc * lax.rsqrt(var + LN_EPS) * g + b


def _params(*sem):
    return pltpu.CompilerParams(dimension_semantics=sem, vmem_limit_bytes=VMEM_LIMIT)


def _mm_bias_kernel(x_ref, w_ref, b_ref, o_ref):
    o_ref[...] = (_dot(x_ref[...], w_ref[...]) + b_ref[...]).astype(o_ref.dtype)


def _matmul_bias(xb, w, b, out_dtype, tm, tn):
    m, k = xb.shape
    n = b.shape[1]
    tm, tn = _pick(m, tm), _pick(n, tn)
    return pl.pallas_call(
        _mm_bias_kernel,
        grid=(m // tm, n // tn),
        in_specs=[pl.BlockSpec((tm, k), lambda i, j: (i, 0)),
                  pl.BlockSpec((k, tn), lambda i, j: (0, j)),
                  pl.BlockSpec((1, tn), lambda i, j: (0, j))],
        out_specs=pl.BlockSpec((tm, tn), lambda i, j: (i, j)),
        out_shape=jax.ShapeDtypeStruct((m, n), out_dtype),
        compiler_params=_params("parallel", "parallel"),
        name="in_proj",
    )(xb, w, b)


def _mm_stream_kernel(x_ref, wt_hbm, b_ref, o_ref, stage, work, sem, *, tn, row0):
    j, i = pl.program_id(0), pl.program_id(1)

    def w_copy(jt):
        return pltpu.make_async_copy(wt_hbm.at[pl.ds(pl.multiple_of(row0 + jt * tn, 16), tn), :], stage, sem.at[0])

    @pl.when(jnp.logical_and(j == 0, i == 0))
    def _():
        w_copy(0).start()

    @pl.when(i == 0)
    def _():
        w_copy(j).wait()
        work[...] = stage[...].astype(BF16)

        @pl.when(j + 1 < pl.num_programs(0))
        def _():
            w_copy(j + 1).start()

    o_ref[...] = (_dot_t1(x_ref[...], work[...]) + b_ref[...]).astype(o_ref.dtype)


def _matmul_bias_stream(xb, wt, b, row0, out_dtype, tm, tn):
    m, k = xb.shape
    n = b.shape[1]
    tm, tn = _pick(m, tm), _pick(n, tn)
    assert row0 % 16 == 0 and tn % 16 == 0
    return pl.pallas_call(
        functools.partial(_mm_stream_kernel, tn=tn, row0=row0),
        grid=(n // tn, m // tm),
        in_specs=[pl.BlockSpec((tm, k), lambda j, i: (i, 0)),
                  pl.BlockSpec(memory_space=pl.ANY),
                  pl.BlockSpec((1, tn), lambda j, i: (0, j))],
        out_specs=pl.BlockSpec((tm, tn), lambda j, i: (i, j)),
        out_shape=jax.ShapeDtypeStruct((m, n), out_dtype),
        scratch_shapes=[pltpu.VMEM((tn, k), F32), pltpu.VMEM((tn, k), BF16), pltpu.SemaphoreType.DMA((1,))],
        compiler_params=_params("arbitrary", "arbitrary"),
        name="in_proj",
    )(xb, wt, b)


def _merge_kernel(ya_ref, wa_ref, yb_ref, wb_ref, ga_ref, gb_ref, o_ref):
    a = _dot(ya_ref[...], wa_ref[...])
    b = _dot(yb_ref[...], wb_ref[...])
    o_ref[...] = (_sigmoid(ga_ref[...]) * a + _sigmoid(gb_ref[...]) * b).astype(o_ref.dtype)


def _merge(ya, wa, yb, wb, z, off_ga, off_gb, tm, tn):
    m, ka = ya.shape
    kb = yb.shape[1]
    n = wa.shape[1]
    tm, tn = _pick(m, tm), _pick(n, tn)
    while off_ga % tn or off_gb % tn:
        tn //= 2
    oa, ob = off_ga // tn, off_gb // tn
    return pl.pallas_call(
        _merge_kernel,
        grid=(m // tm, n // tn),
        in_specs=[pl.BlockSpec((tm, ka), lambda i, j: (i, 0)),
                  pl.BlockSpec((ka, tn), lambda i, j: (0, j)),
                  pl.BlockSpec((tm, kb), lambda i, j: (i, 0)),
                  pl.BlockSpec((kb, tn), lambda i, j: (0, j)),
                  pl.BlockSpec((tm, tn), lambda i, j: (i, oa + j)),
                  pl.BlockSpec((tm, tn), lambda i, j: (i, ob + j))],
        out_specs=pl.BlockSpec((tm, tn), lambda i, j: (i, j)),
        out_shape=jax.ShapeDtypeStruct((m, n), BF16),
        compiler_params=_params("parallel", "parallel"),
        name="merge",
    )(ya, wa, yb, wb, z, z)


def _mm_resid_kernel(m_ref, w_ref, x_ref, o_ref, *, alpha):
    o_ref[...] = alpha * x_ref[...] + _dot(m_ref[...], w_ref[...])


def _out_proj_resid(mb, w, x, alpha, tm, tn):
    m, k = mb.shape
    n = w.shape[1]
    tm, tn = _pick(m, tm), _pick(n, tn)
    return pl.pallas_call(
        functools.partial(_mm_resid_kernel, alpha=alpha),
        grid=(m // tm, n // tn),
        in_specs=[pl.BlockSpec((tm, k), lambda i, j: (i, 0)),
                  pl.BlockSpec((k, tn), lambda i, j: (0, j)),
                  pl.BlockSpec((tm, tn), lambda i, j: (i, j))],
        out_specs=pl.BlockSpec((tm, tn), lambda i, j: (i, j)),
        out_shape=jax.ShapeDtypeStruct((m, n), F32),
        compiler_params=_params("parallel", "parallel"),
        name="out_proj",
    )(mb, w, x)


def _branch_a_kernel(u_ref, v_ref, ws_ref, bs_ref, g_ref, b_ref, o_ref, *, n_groups, chunk):
    tm, aw = u_ref.shape
    gd = aw // n_groups
    row = lax.broadcasted_iota(I32, (chunk, chunk), 0)
    col = lax.broadcasted_iota(I32, (chunk, chunk), 1)
    causal = row >= col
    for g in range(n_groups):
        ws = jnp.where(causal, ws_ref[g], 0.0).astype(BF16)
        cs = slice(g * gd, (g + 1) * gd)
        for c in range(tm // chunk):
            rs = slice(c * chunk, (c + 1) * chunk)
            vn = _layer_norm(_gelu(v_ref[rs, cs]), g_ref[:, cs], b_ref[:, cs])
            mixed = _dot(ws, vn.astype(BF16)) + bs_ref[:, g:g + 1]
            o_ref[rs, cs] = (_gelu(u_ref[rs, cs]) * mixed).astype(o_ref.dtype)


def _branch_a(z, a_ws, a_bs, a_ln_g, a_ln_b, aw, tm):
    t = z.shape[0]
    n_groups, chunk, _ = a_ws.shape
    tm = max(_pick(t, tm), chunk)
    return pl.pallas_call(
        functools.partial(_branch_a_kernel, n_groups=n_groups, chunk=chunk),
        grid=(t // tm,),
        in_specs=[pl.BlockSpec((tm, aw), lambda i: (i, 0)),
                  pl.BlockSpec((tm, aw), lambda i: (i, 1)),
                  pl.BlockSpec((n_groups, chunk, chunk), lambda i: (0, 0, 0)),
                  pl.BlockSpec((chunk, n_groups), lambda i: (0, 0)),
                  pl.BlockSpec((1, aw), lambda i: (0, 0)),
                  pl.BlockSpec((1, aw), lambda i: (0, 0))],
        out_specs=pl.BlockSpec((tm, aw), lambda i: (i, 0)),
        out_shape=jax.ShapeDtypeStruct((t, aw), BF16),
        compiler_params=_params("parallel"),
        name="branch_a",
    )(z, z, a_ws, a_bs.T, a_ln_g.reshape(1, aw), a_ln_b.reshape(1, aw))


def _gla_kernel(q_ref, k_ref, v_ref, r_ref, lr_ref, wlr_ref, blr_ref, gn_ref, o_ref, s_ref, *, heads, chunk):
    @pl.when(pl.program_id(0) == 0)
    def _():
        s_ref[...] = jnp.zeros_like(s_ref)

    tg, kw = q_ref.shape
    vw = v_ref.shape[1]
    dk, dv = kw // heads, vw // heads
    row = lax.broadcasted_iota(I32, (chunk, chunk), 0)
    col = lax.broadcasted_iota(I32, (chunk, chunk), 1)
    causal = row >= col
    tri = causal.astype(BF16)
    ones = jnp.ones((chunk, LANES), BF16)
    scale = dk ** -0.5

    def body(c, carry):
        rs = pl.ds(pl.multiple_of(c * chunk, chunk), chunk)
        gl = _dot(lr_ref[rs, :].astype(BF16), wlr_ref[...]) + blr_ref[...]
        la = (jnp.minimum(gl, 0.0) - jnp.log1p(jnp.exp(-jnp.abs(gl)))) * (1.0 / GLA_GATE_TAU)
        la_hi = la.astype(BF16)
        la_lo = (la - la_hi.astype(F32)).astype(BF16)
        cum = _dot(tri, la_hi) + _dot(tri, la_lo)
        tot = cum[chunk - 1:chunk, :]
        tot_col = _dot_t0(la_hi, ones) + _dot_t0(la_lo, ones)
        dec_col = jnp.exp(tot_col)
        q = q_ref[rs, :] * scale
        k = k_ref[rs, :]
        qt = (q * jnp.exp(cum)).astype(BF16)
        kt = (k * jnp.exp(-cum)).astype(BF16)
        kl = (k * jnp.exp(tot - cum)).astype(BF16)
        for h in range(heads):
            ks = slice(h * dk, (h + 1) * dk)
            vs = slice(h * dv, (h + 1) * dv)
            vh = v_ref[rs, vs].astype(BF16)
            sc = jnp.where(causal, _dot_t1(qt[:, ks], kt[:, ks]), 0.0).astype(BF16)
            state = s_ref[h]
            o = _dot(sc, vh) + _dot(qt[:, ks], state.astype(BF16))
            dec = jnp.concatenate([dec_col[ks, :]] * (dv // LANES), axis=1) if dv >= LANES else dec_col[ks, :dv]
            s_ref[h] = dec * state + _dot_t0(kl[:, ks], vh)
            on = o * lax.rsqrt(jnp.mean(o * o, axis=-1, keepdims=True) + LN_EPS) * gn_ref[:, vs]
            rr = r_ref[rs, vs]
            o_ref[rs, vs] = (on * (rr * _sigmoid(rr))).astype(o_ref.dtype)
        return carry

    lax.fori_loop(0, tg // chunk, body, 0)


def _gla(z, lr, wlr, blr, gn, kw, vw, off_q, heads, tg):
    t = z.shape[0]
    tg = max(_pick(t, tg), GLA_CHUNK)
    oq = off_q // kw
    ok = oq + 1
    ov = (off_q + 2 * kw) // vw
    orr = ov + 1
    return pl.pallas_call(
        functools.partial(_gla_kernel, heads=heads, chunk=GLA_CHUNK),
        grid=(t // tg,),
        in_specs=[pl.BlockSpec((tg, kw), lambda i: (i, oq)),
                  pl.BlockSpec((tg, kw), lambda i: (i, ok)),
                  pl.BlockSpec((tg, vw), lambda i: (i, ov)),
                  pl.BlockSpec((tg, vw), lambda i: (i, orr)),
                  pl.BlockSpec((tg, LANES), lambda i: (i, 0)),
                  pl.BlockSpec((LANES, kw), lambda i: (0, 0)),
                  pl.BlockSpec((1, kw), lambda i: (0, 0)),
                  pl.BlockSpec((1, vw), lambda i: (0, 0))],
        out_specs=pl.BlockSpec((tg, vw), lambda i: (i, 0)),
        out_shape=jax.ShapeDtypeStruct((t, vw), BF16),
        scratch_shapes=[pltpu.VMEM((heads, kw // heads, vw // heads), F32)],
        compiler_params=_params("arbitrary"),
        name="gla",
    )(z, z, z, z, lr, wlr, blr, gn)


def _ln_router_kernel(h_ref, g_ref, b_ref, wr_ref, br_ref, x1_ref, xp_ref, idx_ref, wgt_ref, rank_ref, cnt_ref,
                      base_ref, *, top_k):
    @pl.when(pl.program_id(0) == 0)
    def _():
        base_ref[...] = jnp.zeros_like(base_ref)

    tm, d = h_ref.shape
    half = d // 2
    x1 = _layer_norm(h_ref[...], g_ref[...], b_ref[...])
    x1_ref[...] = x1
    lo = pltpu.bitcast(x1[:, :half].astype(BF16).astype(F32), U32) >> 16
    hi = pltpu.bitcast(x1[:, half:].astype(BF16).astype(F32), U32) & jnp.uint32(HI_MASK)
    xp_ref[...] = lo | hi

    logits = _dot(x1.astype(BF16), wr_ref[...]) + br_ref[...]
    lane = lax.broadcasted_iota(I32, (tm, LANES), 1).astype(F32)
    member = jnp.zeros((tm, LANES), F32)
    vals, idxs = [], []
    cur = logits
    for _ in range(top_k):
        mx = jnp.max(cur, axis=-1, keepdims=True)
        ix = jnp.min(jnp.where(cur == mx, lane, float(LANES)), axis=-1, keepdims=True)
        sel = lane == ix
        vals.append(mx)
        idxs.append(ix)
        member = member + sel.astype(F32)
        cur = jnp.where(sel, -jnp.inf, cur)
    exps = [jnp.exp(v - vals[0]) for v in vals]
    denom = exps[0]
    for e in exps[1:]:
        denom = denom + e
    r_i = lax.broadcasted_iota(I32, (tm, tm), 0)
    c_i = lax.broadcasted_iota(I32, (tm, tm), 1)
    before = _dot((r_i > c_i).astype(BF16), member.astype(BF16)) + base_ref[...]
    idx_out = jnp.zeros((tm, LANES), F32)
    wgt_out = jnp.zeros((tm, LANES), F32)
    rank_out = jnp.zeros((tm, LANES), F32)
    for j in range(top_k):
        rk = jnp.sum(jnp.where(lane == idxs[j], before, 0.0), axis=-1, keepdims=True)
        at = lane == float(j)
        idx_out = jnp.where(at, idxs[j], idx_out)
        wgt_out = jnp.where(at, exps[j] / denom, wgt_out)
        rank_out = jnp.where(at, rk, rank_out)
    idx_ref[...] = idx_out.astype(I32)
    wgt_ref[...] = wgt_out
    rank_ref[...] = rank_out.astype(I32)
    base_ref[...] = base_ref[...] + jnp.sum(member, axis=0, keepdims=True)
    cnt_ref[...] = base_ref[...]


def _ln_router(h1, g, b, wr, br, tm):
    t, d = h1.shape
    tm = _pick(t, tm)
    row = lambda i: (i, 0)
    fixed = lambda i: (0, 0)
    return pl.pallas_call(
        functools.partial(_ln_router_kernel, top_k=TOP_K),
        grid=(t // tm,),
        in_specs=[pl.BlockSpec((tm, d), row),
                  pl.BlockSpec((1, d), fixed),
                  pl.BlockSpec((1, d), fixed),
                  pl.BlockSpec((d, LANES), fixed),
                  pl.BlockSpec((1, LANES), fixed)],
        out_specs=[pl.BlockSpec((tm, d), row),
                   pl.BlockSpec((tm, d // 2), row),
                   pl.BlockSpec((tm, LANES), row),
                   pl.BlockSpec((tm, LANES), row),
                   pl.BlockSpec((tm, LANES), row),
                   pl.BlockSpec((1, LANES), fixed)],
        out_shape=[jax.ShapeDtypeStruct((t, d), F32),
                   jax.ShapeDtypeStruct((t, d // 2), U32),
                   jax.ShapeDtypeStruct((t, LANES), I32),
                   jax.ShapeDtypeStruct((t, LANES), F32),
                   jax.ShapeDtypeStruct((t, LANES), I32),
                   jax.ShapeDtypeStruct((1, LANES), F32)],
        scratch_shapes=[pltpu.VMEM((1, LANES), F32)],
        compiler_params=_params("arbitrary"),
        name="ln_router",
    )(h1, g, b, wr, br)


def _dispatch_kernel(nb_ref, tok_ref, tok_nxt_ref, xp_hbm, xs_ref, buf, sems, *, tm_e):
    b = pl.program_id(0)
    slot = b % 2
    used = b < nb_ref[0]
    stripe = tm_e // DISPATCH_STRIPES

    def row_copy(idx_ref, s, u, k):
        return pltpu.make_async_copy(xp_hbm.at[pl.ds(idx_ref[u * stripe + k], 1)], buf.at[s, u, pl.ds(k, 1)],
                                     sems.at[s])

    def issue(idx_ref, s):
        def body(k, c):
            for u in range(DISPATCH_STRIPES):
                row_copy(idx_ref, s, u, k).start()
            return c
        lax.fori_loop(0, stripe, body, 0)

    @pl.when(b == 0)
    def _():
        issue(tok_ref, slot)

    @pl.when(b + 1 < nb_ref[0])
    def _():
        issue(tok_nxt_ref, 1 - slot)

    @pl.when(jnp.logical_not(used))
    def _():
        xs_ref[...] = jnp.zeros_like(xs_ref)

    @pl.when(used)
    def _():
        def drain(k, c):
            for u in range(DISPATCH_STRIPES):
                row_copy(tok_ref, slot, u, k).wait()
            return c

        lax.fori_loop(0, stripe, drain, 0)
        w = buf[slot].reshape(tm_e, buf.shape[-1])
        half = w.shape[1]
        xs_ref[:, :half] = pltpu.bitcast(w << 16, F32).astype(BF16)
        xs_ref[:, half:] = pltpu.bitcast(w & jnp.uint32(HI_MASK), F32).astype(BF16)


def _dispatch(tok_buf, nb_used, xp, tm_e):
    dh = xp.shape[1]
    n_rows = tok_buf.shape[0]
    last = n_rows // tm_e - 1
    grid_spec = pltpu.PrefetchScalarGridSpec(
        num_scalar_prefetch=1,
        grid=(n_rows // tm_e,),
        in_specs=[pl.BlockSpec((tm_e,), lambda b, nb: (b,), memory_space=pltpu.SMEM),
                  pl.BlockSpec((tm_e,), lambda b, nb: (jnp.minimum(b + 1, last),), memory_space=pltpu.SMEM),
                  pl.BlockSpec(memory_space=pl.ANY)],
        out_specs=pl.BlockSpec((tm_e, 2 * dh), lambda b, nb: (b, 0)),
        scratch_shapes=[pltpu.VMEM((2, DISPATCH_STRIPES, tm_e // DISPATCH_STRIPES, dh), U32),
                        pltpu.SemaphoreType.DMA((2,))],
    )
    return pl.pallas_call(
        functools.partial(_dispatch_kernel, tm_e=tm_e),
        grid_spec=grid_spec,
        out_shape=jax.ShapeDtypeStruct((n_rows, 2 * dh), BF16),
        compiler_params=_params("arbitrary"),
        name="dispatch",
    )(nb_used, tok_buf, tok_buf, xp)


def _expert_changed(be_ref, nb_ref):
    b = pl.program_id(1)
    fresh = jnp.logical_or(b == 0, be_ref[b] != be_ref[jnp.maximum(b - 1, 0)])
    return b < nb_ref[0], fresh


def _stream_expert_weights(be_ref, nxt_ref, wrap_ref, used, fresh, tile_copies, stage, work):
    j, b = pl.program_id(0), pl.program_id(1)
    nj = pl.num_programs(0)

    @pl.when(jnp.logical_and(j == 0, b == 0))
    def _():
        for c in tile_copies(be_ref[0], 0):
            c.start()

    @pl.when(jnp.logical_and(used, fresh))
    def _():
        for c in tile_copies(be_ref[b], j):
            c.wait()
        for src, dst in zip(stage, work):
            dst[...] = src[...].astype(BF16)
        jn = j + wrap_ref[b]

        @pl.when(jn < nj)
        def _():
            for c in tile_copies(nxt_ref[b], jn):
                c.start()


def _for_valid_rows(used, bv_ref, out_ref, compute):
    tm = out_ref.shape[0]
    half = tm // 2
    few = bv_ref[pl.program_id(1)] <= half

    @pl.when(jnp.logical_and(used, jnp.logical_not(few)))
    def _():
        compute(tm)

    @pl.when(jnp.logical_and(used, few))
    def _():
        compute(half)
        out_ref[half:, :] = jnp.zeros((tm - half, out_ref.shape[1]), out_ref.dtype)


def _moe1_kernel(be_ref, nb_ref, nxt_ref, wrap_ref, bv_ref, xs_ref, w_hbm, bg_ref, bl_ref, h_ref, sg, sl, wg_s, wl_s,
                 sems, *, tn):
    used, fresh = _expert_changed(be_ref, nb_ref)
    n_lin = pl.num_programs(0)

    def tile_copies(e, jt):
        return (pltpu.make_async_copy(w_hbm.at[e, :, pl.ds(pl.multiple_of(jt * tn, tn), tn)], sg, sems.at[0]),
                pltpu.make_async_copy(w_hbm.at[e, :, pl.ds(pl.multiple_of((n_lin + jt) * tn, tn), tn)], sl,
                                      sems.at[1]))

    _stream_expert_weights(be_ref, nxt_ref, wrap_ref, used, fresh, tile_copies, (sg, sl), (wg_s, wl_s))

    @pl.when(jnp.logical_not(used))
    def _():
        h_ref[...] = jnp.zeros_like(h_ref)

    def compute(rows):
        x = xs_ref[:rows, :]
        gate = jnp.minimum(_dot(x, wg_s[...]) + bg_ref[...], SWIGLU_LIMIT)
        lin = jnp.clip(_dot(x, wl_s[...]) + bl_ref[...], -SWIGLU_LIMIT, SWIGLU_LIMIT)
        h_ref[:rows, :] = (gate * _sigmoid(SWIGLU_ALPHA * gate) * (lin + 1.0)).astype(h_ref.dtype)

    _for_valid_rows(used, bv_ref, h_ref, compute)


def _moe1(sched, xs, w_gu, b_gu, tm, tn):
    n_rows, dh = xs.shape
    n_exp, d, de2 = w_gu.shape
    de = de2 // 2
    tn = _pick(de, tn)
    nj = de // tn

    def blk(b, nb):
        return jnp.minimum(b, nb[0] - 1)

    grid_spec = pltpu.PrefetchScalarGridSpec(
        num_scalar_prefetch=5,
        grid=(nj, n_rows // tm),
        in_specs=[pl.BlockSpec((tm, dh), lambda j, b, be, nb, *_: (blk(b, nb), 0)),
                  pl.BlockSpec(memory_space=pl.ANY),
                  pl.BlockSpec((None, 1, tn), lambda j, b, be, nb, *_: (be[blk(b, nb)], 0, j)),
                  pl.BlockSpec((None, 1, tn), lambda j, b, be, nb, *_: (be[blk(b, nb)], 0, nj + j))],
        out_specs=pl.BlockSpec((tm, tn), lambda j, b, *_: (b, j)),
        scratch_shapes=[pltpu.VMEM((d, tn), F32), pltpu.VMEM((d, tn), F32),
                        pltpu.VMEM((d, tn), BF16), pltpu.VMEM((d, tn), BF16),
                        pltpu.SemaphoreType.DMA((2,))],
    )
    return pl.pallas_call(
        functools.partial(_moe1_kernel, tn=tn),
        grid_spec=grid_spec,
        out_shape=jax.ShapeDtypeStruct((n_rows, de), BF16),
        compiler_params=_params("arbitrary", "arbitrary"),
        name="moe_up",
    )(*sched, xs, w_gu, b_gu, b_gu)


def _moe2_kernel(be_ref, nb_ref, nxt_ref, wrap_ref, bv_ref, h_ref, w_hbm, b_ref, y_ref, stage, w_s, sems, *, tn):
    used, fresh = _expert_changed(be_ref, nb_ref)

    def tile_copies(e, jt):
        return (pltpu.make_async_copy(w_hbm.at[e, :, pl.ds(pl.multiple_of(jt * tn, tn), tn)], stage, sems.at[0]),)

    _stream_expert_weights(be_ref, nxt_ref, wrap_ref, used, fresh, tile_copies, (stage,), (w_s,))

    @pl.when(jnp.logical_not(used))
    def _():
        y_ref[...] = jnp.zeros_like(y_ref)

    def compute(rows):
        y_ref[:rows, :] = _dot(h_ref[:rows, :], w_s[...]) + b_ref[...]

    _for_valid_rows(used, bv_ref, y_ref, compute)


def _moe2(sched, h, w_down, b_down, tm, tn):
    n_rows, de = h.shape
    n_exp, _, d = w_down.shape
    tn = _pick(d, tn)

    def blk(b, nb):
        return jnp.minimum(b, nb[0] - 1)

    grid_spec = pltpu.PrefetchScalarGridSpec(
        num_scalar_prefetch=5,
        grid=(d // tn, n_rows // tm),
        in_specs=[pl.BlockSpec((tm, de), lambda j, b, be, nb, *_: (blk(b, nb), 0)),
                  pl.BlockSpec(memory_space=pl.ANY),
                  pl.BlockSpec((None, 1, tn), lambda j, b, be, nb, *_: (be[blk(b, nb)], 0, j))],
        out_specs=pl.BlockSpec((tm, tn), lambda j, b, *_: (b, j)),
        scratch_shapes=[pltpu.VMEM((de, tn), F32), pltpu.VMEM((de, tn), BF16), pltpu.SemaphoreType.DMA((1,))],
    )
    return pl.pallas_call(
        functools.partial(_moe2_kernel, tn=tn),
        grid_spec=grid_spec,
        out_shape=jax.ShapeDtypeStruct((n_rows, d), F32),
        compiler_params=_params("arbitrary", "arbitrary"),
        name="moe_down",
    )(*sched, h, w_down, b_down)


def _combine_kernel(dest_ref, dest_nxt_ref, ys_hbm, x1_ref, wgt_ref, g_ref, b_ref, o_ref, buf, sems, *, tc, top_k,
                    alpha):
    i = pl.program_id(0)
    slot = i % 2

    def row_copy(idx_ref, s, t, j):
        return pltpu.make_async_copy(ys_hbm.at[pl.ds(idx_ref[t * top_k + j], 1)], buf.at[s, j, pl.ds(t, 1)],
                                     sems.at[s])

    def issue(idx_ref, s):
        def body(t, c):
            for j in range(top_k):
                row_copy(idx_ref, s, t, j).start()
            return c
        lax.fori_loop(0, tc, body, 0, unroll=2)

    @pl.when(i == 0)
    def _():
        issue(dest_ref, slot)

    @pl.when(i + 1 < pl.num_programs(0))
    def _():
        issue(dest_nxt_ref, 1 - slot)

    def drain(t, c):
        for j in range(top_k):
            row_copy(dest_ref, slot, t, j).wait()
        return c

    lax.fori_loop(0, tc, drain, 0, unroll=2)
    y = wgt_ref[:, 0:1] * buf[slot, 0]
    for j in range(1, top_k):
        y = y + wgt_ref[:, j:j + 1] * buf[slot, j]
    o_ref[...] = _layer_norm(alpha * x1_ref[...] + y, g_ref[...], b_ref[...])


def _combine(dest_flat, ys, x1, wgt, g, b, alpha, tc):
    t, d = x1.shape
    tc = _pick(t, tc)
    last = t // tc - 1
    return pl.pallas_call(
        functools.partial(_combine_kernel, tc=tc, top_k=TOP_K, alpha=alpha),
        grid=(t // tc,),
        in_specs=[pl.BlockSpec((tc * TOP_K,), lambda i: (i,), memory_space=pltpu.SMEM),
                  pl.BlockSpec((tc * TOP_K,), lambda i: (jnp.minimum(i + 1, last),), memory_space=pltpu.SMEM),
                  pl.BlockSpec(memory_space=pl.ANY),
                  pl.BlockSpec((tc, d), lambda i: (i, 0)),
                  pl.BlockSpec((tc, LANES), lambda i: (i, 0)),
                  pl.BlockSpec((1, d), lambda i: (0, 0)),
                  pl.BlockSpec((1, d), lambda i: (0, 0))],
        out_specs=pl.BlockSpec((tc, d), lambda i: (i, 0)),
        out_shape=jax.ShapeDtypeStruct((t, d), F32),
        scratch_shapes=[pltpu.VMEM((2, TOP_K, tc, d), F32), pltpu.SemaphoreType.DMA((2,))],
        compiler_params=_params("arbitrary"),
        name="combine",
    )(dest_flat, dest_flat, ys, x1, wgt, g, b)


def _layer(x2d, w_in, b_in, a_ws, a_bs, a_ln_g, a_ln_b, gla_w_lr, gla_b_lr, gla_gn_g, w_br_a, w_br_b, w_o,
           ln1_g, ln1_b, w_router, b_router, w_gu, b_gu, w_down, b_down, ln2_g, ln2_b, alpha):
    t, d = x2d.shape
    n_groups, a_chunk, _ = a_ws.shape
    aw = a_ln_g.size
    rank, kw = gla_w_lr.shape
    heads, dv = gla_gn_g.shape
    vw = heads * dv
    n_exp = w_router.shape[1]
    de = w_down.shape[1]
    off_lr = 2 * aw + 2 * kw + 2 * vw
    off_q = 2 * aw

    w_in_t = w_in.T
    w_lr_in = jnp.pad(w_in[:, off_lr:off_lr + rank], ((0, 0), (0, LANES - rank))).astype(BF16)
    b_lr_in = jnp.pad(b_in[off_lr:off_lr + rank], (0, LANES - rank)).reshape(1, LANES)
    wlr = jnp.pad(gla_w_lr, ((0, LANES - rank), (0, 0))).astype(BF16)
    wr = jnp.pad(w_router, ((0, 0), (0, LANES - n_exp))).astype(BF16)
    br = jnp.pad(b_router, (0, LANES - n_exp), constant_values=-jnp.inf).reshape(1, LANES)

    xb = x2d.astype(BF16)
    z = _matmul_bias_stream(xb, w_in_t, b_in[:off_lr].reshape(1, -1), 0, F32, 1024, 1024)
    zg = _matmul_bias_stream(xb, w_in_t, b_in[off_lr + rank:].reshape(1, -1), off_lr + rank, F32, 1024, 1024)
    lr = _matmul_bias(xb, w_lr_in, b_lr_in, F32, 2048, LANES)

    ya = _branch_a(z, a_ws, a_bs, a_ln_g, a_ln_b, aw, 2 * a_chunk)
    yb = _gla(z, lr, wlr, gla_b_lr.reshape(1, kw), gla_gn_g.reshape(1, vw), kw, vw, off_q, heads, 256)
    merged = _merge(ya, w_br_a.astype(BF16), yb, w_br_b.astype(BF16), zg, 0, d, 1024, 512)
    h1 = _out_proj_resid(merged, w_o.astype(BF16), x2d, alpha, 1024, 512)

    x1, xp, idx, wgt, rnk, cnt = _ln_router(h1, ln1_g.reshape(1, d), ln1_b.reshape(1, d), wr, br, 256)

    tm_e = 512 if t * TOP_K >= 512 * n_exp else 64
    n_assign = t * TOP_K
    n_blocks = -(-(n_assign + n_exp * (tm_e - 1)) // tm_e)
    n_rows = n_blocks * tm_e
    counts = cnt[0, :n_exp].astype(I32)
    padded = (counts + tm_e - 1) // tm_e * tm_e
    pend = jnp.cumsum(padded)
    pstart = pend - padded
    sel = idx[:, :TOP_K, None] == jnp.arange(n_exp, dtype=I32)[None, None, :]
    dest = jnp.sum(jnp.where(sel, pstart[None, None, :], 0), axis=-1) + rnk[:, :TOP_K]
    dest_flat = dest.reshape(-1).astype(I32)
    nb_used = (pend[-1] // tm_e).astype(I32).reshape(1)
    block_start = jnp.arange(n_blocks, dtype=I32) * tm_e
    block_e = jnp.minimum(jnp.sum((pend[None, :] <= block_start[:, None]).astype(I32), axis=1), n_exp - 1)

    tok_buf = jnp.zeros((n_rows,), I32).at[dest_flat].set(jnp.arange(n_assign, dtype=I32) // TOP_K,
                                                          unique_indices=True)
    xs = _dispatch(tok_buf, nb_used, xp, tm_e)
    experts = jnp.arange(n_exp, dtype=I32)
    later = jnp.where((padded > 0)[None, :] & (experts[None, :] > block_e[:, None]), experts[None, :], n_exp)
    nxt = jnp.min(later, axis=1)
    wrap = (nxt == n_exp).astype(I32)
    nxt_e = jnp.where(nxt == n_exp, block_e[0], nxt).astype(I32)
    mine = experts[None, :] == block_e[:, None]
    group_end = jnp.sum(jnp.where(mine, (pstart + counts)[None, :], 0), axis=1)
    bvalid = jnp.clip(group_end - block_start, 0, tm_e).astype(I32)
    sched = (block_e, nb_used, nxt_e, wrap, bvalid)
    hmid = _moe1(sched, xs, w_gu, b_gu.reshape(n_exp, 1, 2 * de), tm_e, 768)
    ys = _moe2(sched, hmid, w_down, b_down.reshape(n_exp, 1, d), tm_e, 2048)
    return _combine(dest_flat, ys, x1, wgt, ln2_g.reshape(1, d), ln2_b.reshape(1, d), alpha, 128)


def kernel(x, w_in, b_in, a_ws, a_bs, a_ln_g, a_ln_b, gla_w_lr, gla_b_lr, gla_gn_g, w_br_a, w_br_b, w_o, ln1_g, ln1_b, w_router, b_router, w_gu, b_gu, w_down, b_down, ln2_g, ln2_b):
    bsz, seq, d = x.shape
    depth = w_in.shape[0]
    alpha = (2 * depth) ** 0.25
    outs = []
    for bi in range(bsz):
        h = x[bi]
        for l in range(depth):
            h = _layer(h, w_in[l], b_in[l], a_ws[l], a_bs[l], a_ln_g[l], a_ln_b[l], gla_w_lr[l], gla_b_lr[l],
                       gla_gn_g[l], w_br_a[l], w_br_b[l], w_o[l], ln1_g[l], ln1_b[l], w_router[l], b_router[l],
                       w_gu[l], b_gu[l], w_down[l], b_down[l], ln2_g[l], ln2_b[l], alpha)
        outs.append(h)
    return jnp.stack(outs) if bsz > 1 else outs[0][None]
```

```python
import functools

import jax
import jax.numpy as jnp
from jax import lax
from jax.experimental import pallas as pl
from jax.experimental.pallas import tpu as pltpu

F32 = jnp.float32
BF16 = jnp.bfloat16
U32 = jnp.uint32
I32 = jnp.int32

LN_EPS = 1e-5
GLA_CHUNK = 64
GLA_GATE_TAU = 16.0
TOP_K = 4
SWIGLU_LIMIT = 7.0
SWIGLU_ALPHA = 1.702
LANES = 128
VMEM_LIMIT = 56 * 1024 * 1024
HI_MASK = 0xFFFF0000
DISPATCH_STRIPES = 8

def _pick(n, pref):
    t = min(pref, n)
    while n % t:
        t //= 2
    return t


def _dot(a, b):
    return jnp.dot(a, b, preferred_element_type=F32)


def _dot_t0(a, b):
    return lax.dot_general(a, b, (((0,), (0,)), ((), ())), preferred_element_type=F32)


def _dot_t1(a, b):
    return lax.dot_general(a, b, (((1,), (1,)), ((), ())), preferred_element_type=F32)


def _sigmoid(x):
    return 1.0 / (1.0 + jnp.exp(-x))


def _gelu(x):
    return 0.5 * x * (1.0 + lax.erf(x * (2.0 ** -0.5)))


def _layer_norm(x, g, b):
    mu = jnp.mean(x, axis=-1, keepdims=True)
    xc = x - mu
    var = jnp.mean(xc * xc, axis=-1, keepdims=True)
    return xc * lax.rsqrt(var + LN_EPS) * g + b


def _params(*sem):
    return pltpu.CompilerParams(dimension_semantics=sem, vmem_limit_bytes=VMEM_LIMIT)


def _cast_lr_kernel(x_ref, w_ref, b_ref, xb_ref, lr_ref):
    xb = x_ref[...].astype(BF16)
    xb_ref[...] = xb
    lr_ref[...] = _dot(xb, w_ref[...]) + b_ref[...]


def _cast_and_decay_proj(x, w, b, tm):
    m, k = x.shape
    n = w.shape[1]
    tm = _pick(m, tm)
    return pl.pallas_call(
        _cast_lr_kernel,
        grid=(m // tm,),
        in_specs=[pl.BlockSpec((tm, k), lambda i: (i, 0)),
                  pl.BlockSpec((k, n), lambda i: (0, 0)),
                  pl.BlockSpec((1, n), lambda i: (0, 0))],
        out_specs=[pl.BlockSpec((tm, k), lambda i: (i, 0)),
                   pl.BlockSpec((tm, n), lambda i: (i, 0))],
        out_shape=[jax.ShapeDtypeStruct((m, k), BF16), jax.ShapeDtypeStruct((m, n), F32)],
        compiler_params=_params("parallel"),
        name="cast_lr",
    )(x, w, b)


def _mm_stream_kernel(x_ref, wt_hbm, b_ref, o_ref, stage, work, sem, *, tn, row0):
    j, i = pl.program_id(0), pl.program_id(1)

    def w_copy(jt):
        return pltpu.make_async_copy(wt_hbm.at[pl.ds(pl.multiple_of(row0 + jt * tn, 16), tn), :], stage, sem.at[0])

    @pl.when(jnp.logical_and(j == 0, i == 0))
    def _():
        w_copy(0).start()

    @pl.when(i == 0)
    def _():
        w_copy(j).wait()
        work[...] = stage[...].astype(BF16)

        @pl.when(j + 1 < pl.num_programs(0))
        def _():
            w_copy(j + 1).start()

    o_ref[...] = (_dot_t1(x_ref[...], work[...]) + b_ref[...]).astype(o_ref.dtype)


def _matmul_bias_stream(xb, wt, b, row0, out_dtype, tm, tn):
    m, k = xb.shape
    n = b.shape[1]
    tm, tn = _pick(m, tm), _pick(n, tn)
    assert row0 % 16 == 0 and tn % 16 == 0
    return pl.pallas_call(
        functools.partial(_mm_stream_kernel, tn=tn, row0=row0),
        grid=(n // tn, m // tm),
        in_specs=[pl.BlockSpec((tm, k), lambda j, i: (i, 0)),
                  pl.BlockSpec(memory_space=pl.ANY),
                  pl.BlockSpec((1, tn), lambda j, i: (0, j))],
        out_specs=pl.BlockSpec((tm, tn), lambda j, i: (i, j)),
        out_shape=jax.ShapeDtypeStruct((m, n), out_dtype),
        scratch_shapes=[pltpu.VMEM((tn, k), F32), pltpu.VMEM((tn, k), BF16), pltpu.SemaphoreType.DMA((1,))],
        compiler_params=_params("arbitrary", "arbitrary"),
        name="in_proj",
    )(xb, wt, b)


def _merge_kernel(ya_ref, wa_ref, yb_ref, wb_ref, ga_ref, gb_ref, o_ref):
    a = _sigmoid(ga_ref[...]) * _dot(ya_ref[...], wa_ref[...])
    o_ref[...] = (a + _sigmoid(gb_ref[...]) * _dot(yb_ref[...], wb_ref[...])).astype(o_ref.dtype)


def _merge(ya, wa, yb, wb, z, off_ga, off_gb, tm, tn):
    m, ka = ya.shape
    kb = yb.shape[1]
    n = wa.shape[1]
    tm, tn = _pick(m, tm), _pick(n, tn)
    while off_ga % tn or off_gb % tn:
        tn //= 2
    oa, ob = off_ga // tn, off_gb // tn
    return pl.pallas_call(
        _merge_kernel,
        grid=(m // tm, n // tn),
        in_specs=[pl.BlockSpec((tm, ka), lambda i, j: (i, 0)),
                  pl.BlockSpec((ka, tn), lambda i, j: (0, j)),
                  pl.BlockSpec((tm, kb), lambda i, j: (i, 0)),
                  pl.BlockSpec((kb, tn), lambda i, j: (0, j)),
                  pl.BlockSpec((tm, tn), lambda i, j: (i, oa + j)),
                  pl.BlockSpec((tm, tn), lambda i, j: (i, ob + j))],
        out_specs=pl.BlockSpec((tm, tn), lambda i, j: (i, j)),
        out_shape=jax.ShapeDtypeStruct((m, n), BF16),
        compiler_params=_params("parallel", "parallel"),
        name="merge",
    )(ya, wa, yb, wb, z, z)


def _mm_resid_kernel(m_ref, w_ref, x_ref, o_ref, *, alpha):
    o_ref[...] = alpha * x_ref[...] + _dot(m_ref[...], w_ref[...])


def _out_proj_resid(mb, w, x, alpha, tm, tn):
    m, k = mb.shape
    n = w.shape[1]
    tm, tn = _pick(m, tm), _pick(n, tn)
    return pl.pallas_call(
        functools.partial(_mm_resid_kernel, alpha=alpha),
        grid=(m // tm, n // tn),
        in_specs=[pl.BlockSpec((tm, k), lambda i, j: (i, 0)),
                  pl.BlockSpec((k, tn), lambda i, j: (0, j)),
                  pl.BlockSpec((tm, tn), lambda i, j: (i, j))],
        out_specs=pl.BlockSpec((tm, tn), lambda i, j: (i, j)),
        out_shape=jax.ShapeDtypeStruct((m, n), F32),
        compiler_params=_params("parallel", "parallel"),
        name="out_proj",
    )(mb, w, x)


def _branch_a_kernel(u_ref, v_ref, ws_ref, bs_ref, g_ref, b_ref, o_ref, *, n_groups, chunk):
    tm, aw = u_ref.shape
    gd = aw // n_groups
    row = lax.broadcasted_iota(I32, (chunk, chunk), 0)
    col = lax.broadcasted_iota(I32, (chunk, chunk), 1)
    causal = row >= col
    for g in range(n_groups):
        ws = jnp.where(causal, ws_ref[g], 0.0).astype(BF16)
        cs = slice(g * gd, (g + 1) * gd)
        for c in range(tm // chunk):
            rs = slice(c * chunk, (c + 1) * chunk)
            vn = _layer_norm(_gelu(v_ref[rs, cs]), g_ref[:, cs], b_ref[:, cs])
            mixed = _dot(ws, vn.astype(BF16)) + bs_ref[:, g:g + 1]
            o_ref[rs, cs] = (_gelu(u_ref[rs, cs]) * mixed).astype(o_ref.dtype)


def _branch_a(z, a_ws, a_bs, a_ln_g, a_ln_b, aw, tm):
    t = z.shape[0]
    n_groups, chunk, _ = a_ws.shape
    tm = max(_pick(t, tm), chunk)
    return pl.pallas_call(
        functools.partial(_branch_a_kernel, n_groups=n_groups, chunk=chunk),
        grid=(t // tm,),
        in_specs=[pl.BlockSpec((tm, aw), lambda i: (i, 0)),
                  pl.BlockSpec((tm, aw), lambda i: (i, 1)),
                  pl.BlockSpec((n_groups, chunk, chunk), lambda i: (0, 0, 0)),
                  pl.BlockSpec((chunk, n_groups), lambda i: (0, 0)),
                  pl.BlockSpec((1, aw), lambda i: (0, 0)),
                  pl.BlockSpec((1, aw), lambda i: (0, 0))],
        out_specs=pl.BlockSpec((tm, aw), lambda i: (i, 0)),
        out_shape=jax.ShapeDtypeStruct((t, aw), BF16),
        compiler_params=_params("parallel"),
        name="branch_a",
    )(z, z, a_ws, a_bs.T, a_ln_g.reshape(1, aw), a_ln_b.reshape(1, aw))


def _gla_kernel(q_ref, k_ref, v_ref, r_ref, lr_ref, wlr_ref, blr_ref, gn_ref, o_ref, s_ref, *, heads, chunk):
    @pl.when(pl.program_id(0) == 0)
    def _():
        s_ref[...] = jnp.zeros_like(s_ref)

    tg, kw = q_ref.shape
    vw = v_ref.shape[1]
    dk, dv = kw // heads, vw // heads
    row = lax.broadcasted_iota(I32, (chunk, chunk), 0)
    col = lax.broadcasted_iota(I32, (chunk, chunk), 1)
    causal = row >= col
    tri = causal.astype(BF16)
    ones = jnp.ones((chunk, LANES), BF16)
    scale = dk ** -0.5

    def body(c, carry):
        rs = pl.ds(pl.multiple_of(c * chunk, chunk), chunk)
        gl = _dot(lr_ref[rs, :].astype(BF16), wlr_ref[...]) + blr_ref[...]
        la = (jnp.minimum(gl, 0.0) - jnp.log1p(jnp.exp(-jnp.abs(gl)))) * (1.0 / GLA_GATE_TAU)
        la_hi = la.astype(BF16)
        la_lo = (la - la_hi.astype(F32)).astype(BF16)
        cum = _dot(tri, la_hi) + _dot(tri, la_lo)
        tot = cum[chunk - 1:chunk, :]
        tot_col = _dot_t0(la_hi, ones) + _dot_t0(la_lo, ones)
        dec_col = jnp.exp(tot_col)
        q = q_ref[rs, :] * scale
        k = k_ref[rs, :]
        qt = (q * jnp.exp(cum)).astype(BF16)
        kt = (k * jnp.exp(-cum)).astype(BF16)
        kl = (k * jnp.exp(tot - cum)).astype(BF16)
        for h in range(heads):
            ks = slice(h * dk, (h + 1) * dk)
            vs = slice(h * dv, (h + 1) * dv)
            vh = v_ref[rs, vs].astype(BF16)
            sc = jnp.where(causal, _dot_t1(qt[:, ks], kt[:, ks]), 0.0).astype(BF16)
            state = s_ref[h]
            o = _dot(sc, vh) + _dot(qt[:, ks], state.astype(BF16))
            dec = jnp.concatenate([dec_col[ks, :]] * (dv // LANES), axis=1) if dv >= LANES else dec_col[ks, :dv]
            s_ref[h] = dec * state + _dot_t0(kl[:, ks], vh)
            on = o * lax.rsqrt(jnp.mean(o * o, axis=-1, keepdims=True) + LN_EPS) * gn_ref[:, vs]
            rr = r_ref[rs, vs]
            o_ref[rs, vs] = (on * (rr * _sigmoid(rr))).astype(o_ref.dtype)
        return carry

    lax.fori_loop(0, tg // chunk, body, 0)


def _gla(z, lr, wlr, blr, gn, kw, vw, off_q, heads, tg):
    t = z.shape[0]
    tg = max(_pick(t, tg), GLA_CHUNK)
    oq = off_q // kw
    ok = oq + 1
    ov = (off_q + 2 * kw) // vw
    orr = ov + 1
    return pl.pallas_call(
        functools.partial(_gla_kernel, heads=heads, chunk=GLA_CHUNK),
        grid=(t // tg,),
        in_specs=[pl.BlockSpec((tg, kw), lambda i: (i, oq)),
                  pl.BlockSpec((tg, kw), lambda i: (i, ok)),
                  pl.BlockSpec((tg, vw), lambda i: (i, ov)),
                  pl.BlockSpec((tg, vw), lambda i: (i, orr)),
                  pl.BlockSpec((tg, LANES), lambda i: (i, 0)),
                  pl.BlockSpec((LANES, kw), lambda i: (0, 0)),
                  pl.BlockSpec((1, kw), lambda i: (0, 0)),
                  pl.BlockSpec((1, vw), lambda i: (0, 0))],
        out_specs=pl.BlockSpec((tg, vw), lambda i: (i, 0)),
        out_shape=jax.ShapeDtypeStruct((t, vw), BF16),
        scratch_shapes=[pltpu.VMEM((heads, kw // heads, vw // heads), F32)],
        compiler_params=_params("arbitrary"),
        name="gla",
    )(z, z, z, z, lr, wlr, blr, gn)


def _ln_router_kernel(h_ref, g_ref, b_ref, wr_ref, br_ref, x1_ref, xp_ref, idx_ref, wgt_ref, rank_ref, cnt_ref,
                      base_ref, *, top_k):
    @pl.when(pl.program_id(0) == 0)
    def _():
        base_ref[...] = jnp.zeros_like(base_ref)

    tm, d = h_ref.shape
    half = d // 2
    x1 = _layer_norm(h_ref[...], g_ref[...], b_ref[...])
    x1_ref[...] = x1
    lo = pltpu.bitcast(x1[:, :half].astype(BF16).astype(F32), U32) >> 16
    hi = pltpu.bitcast(x1[:, half:].astype(BF16).astype(F32), U32) & jnp.uint32(HI_MASK)
    xp_ref[...] = lo | hi

    logits = _dot(x1.astype(BF16), wr_ref[...]) + br_ref[...]
    lane = lax.broadcasted_iota(I32, (tm, LANES), 1).astype(F32)
    member = jnp.zeros((tm, LANES), F32)
    vals, idxs = [], []
    cur = logits
    for _ in range(top_k):
        mx = jnp.max(cur, axis=-1, keepdims=True)
        ix = jnp.min(jnp.where(cur == mx, lane, float(LANES)), axis=-1, keepdims=True)
        sel = lane == ix
        vals.append(mx)
        idxs.append(ix)
        member = member + sel.astype(F32)
        cur = jnp.where(sel, -jnp.inf, cur)
    exps = [jnp.exp(v - vals[0]) for v in vals]
    denom = exps[0]
    for e in exps[1:]:
        denom = denom + e
    r_i = lax.broadcasted_iota(I32, (tm, tm), 0)
    c_i = lax.broadcasted_iota(I32, (tm, tm), 1)
    before = _dot((r_i > c_i).astype(BF16), member.astype(BF16)) + base_ref[...]
    idx_out = jnp.zeros((tm, LANES), F32)
    wgt_out = jnp.zeros((tm, LANES), F32)
    rank_out = jnp.zeros((tm, LANES), F32)
    for j in range(top_k):
        rk = jnp.sum(jnp.where(lane == idxs[j], before, 0.0), axis=-1, keepdims=True)
        at = lane == float(j)
        idx_out = jnp.where(at, idxs[j], idx_out)
        wgt_out = jnp.where(at, exps[j] / denom, wgt_out)
        rank_out = jnp.where(at, rk, rank_out)
    idx_ref[...] = idx_out.astype(I32)
    wgt_ref[...] = wgt_out
    rank_ref[...] = rank_out.astype(I32)
    base_ref[...] = base_ref[...] + jnp.sum(member, axis=0, keepdims=True)
    cnt_ref[...] = base_ref[...]


def _ln_router(h1, g, b, wr, br, tm):
    t, d = h1.shape
    tm = _pick(t, tm)
    row = lambda i: (i, 0)
    fixed = lambda i: (0, 0)
    return pl.pallas_call(
        functools.partial(_ln_router_kernel, top_k=TOP_K),
        grid=(t // tm,),
        in_specs=[pl.BlockSpec((tm, d), row),
                  pl.BlockSpec((1, d), fixed),
                  pl.BlockSpec((1, d), fixed),
                  pl.BlockSpec((d, LANES), fixed),
                  pl.BlockSpec((1, LANES), fixed)],
        out_specs=[pl.BlockSpec((tm, d), row),
                   pl.BlockSpec((tm, d // 2), row),
                   pl.BlockSpec((tm, LANES), row),
                   pl.BlockSpec((tm, LANES), row),
                   pl.BlockSpec((tm, LANES), row),
                   pl.BlockSpec((1, LANES), fixed)],
        out_shape=[jax.ShapeDtypeStruct((t, d), F32),
                   jax.ShapeDtypeStruct((t, d // 2), U32),
                   jax.ShapeDtypeStruct((t, LANES), I32),
                   jax.ShapeDtypeStruct((t, LANES), F32),
                   jax.ShapeDtypeStruct((t, LANES), I32),
                   jax.ShapeDtypeStruct((1, LANES), F32)],
        scratch_shapes=[pltpu.VMEM((1, LANES), F32)],
        compiler_params=_params("arbitrary"),
        name="ln_router",
    )(h1, g, b, wr, br)


def _dispatch_kernel(nb_ref, tok_ref, tok_nxt_ref, xp_hbm, xs_ref, buf, sems, *, tm_e):
    b = pl.program_id(0)
    slot = b % 2
    used = b < nb_ref[0]
    stripe = tm_e // DISPATCH_STRIPES

    def row_copy(idx_ref, s, u, k):
        return pltpu.make_async_copy(xp_hbm.at[pl.ds(idx_ref[u * stripe + k], 1)], buf.at[s, u, pl.ds(k, 1)],
                                     sems.at[s])

    def issue(idx_ref, s):
        def body(k, c):
            for u in range(DISPATCH_STRIPES):
                row_copy(idx_ref, s, u, k).start()
            return c
        lax.fori_loop(0, stripe, body, 0)

    @pl.when(b == 0)
    def _():
        issue(tok_ref, slot)

    @pl.when(b + 1 < nb_ref[0])
    def _():
        issue(tok_nxt_ref, 1 - slot)

    @pl.when(jnp.logical_not(used))
    def _():
        xs_ref[...] = jnp.zeros_like(xs_ref)

    @pl.when(used)
    def _():
        def drain(k, c):
            for u in range(DISPATCH_STRIPES):
                row_copy(tok_ref, slot, u, k).wait()
            return c

        lax.fori_loop(0, stripe, drain, 0)
        w = buf[slot].reshape(tm_e, buf.shape[-1])
        half = w.shape[1]
        xs_ref[:, :half] = pltpu.bitcast(w << 16, F32).astype(BF16)
        xs_ref[:, half:] = pltpu.bitcast(w & jnp.uint32(HI_MASK), F32).astype(BF16)


def _dispatch(tok_buf, nb_used, xp, tm_e):
    dh = xp.shape[1]
    n_rows = tok_buf.shape[0]
    last = n_rows // tm_e - 1
    grid_spec = pltpu.PrefetchScalarGridSpec(
        num_scalar_prefetch=1,
        grid=(n_rows // tm_e,),
        in_specs=[pl.BlockSpec((tm_e,), lambda b, nb: (b,), memory_space=pltpu.SMEM),
                  pl.BlockSpec((tm_e,), lambda b, nb: (jnp.minimum(b + 1, last),), memory_space=pltpu.SMEM),
                  pl.BlockSpec(memory_space=pl.ANY)],
        out_specs=pl.BlockSpec((tm_e, 2 * dh), lambda b, nb: (b, 0)),
        scratch_shapes=[pltpu.VMEM((2, DISPATCH_STRIPES, tm_e // DISPATCH_STRIPES, dh), U32),
                        pltpu.SemaphoreType.DMA((2,))],
    )
    return pl.pallas_call(
        functools.partial(_dispatch_kernel, tm_e=tm_e),
        grid_spec=grid_spec,
        out_shape=jax.ShapeDtypeStruct((n_rows, 2 * dh), BF16),
        compiler_params=_params("arbitrary"),
        name="dispatch",
    )(nb_used, tok_buf, tok_buf, xp)


def _expert_changed(be_ref, nb_ref):
    b = pl.program_id(1)
    fresh = jnp.logical_or(b == 0, be_ref[b] != be_ref[jnp.maximum(b - 1, 0)])
    return b < nb_ref[0], fresh


def _stream_expert_weights(be_ref, nxt_ref, wrap_ref, used, fresh, tile_copies, stage, work):
    j, b = pl.program_id(0), pl.program_id(1)
    nj = pl.num_programs(0)

    @pl.when(jnp.logical_and(j == 0, b == 0))
    def _():
        for c in tile_copies(be_ref[0], 0):
            c.start()

    @pl.when(jnp.logical_and(used, fresh))
    def _():
        for c in tile_copies(be_ref[b], j):
            c.wait()
        for src, dst in zip(stage, work):
            dst[...] = src[...].astype(BF16)
        jn = j + wrap_ref[b]

        @pl.when(jn < nj)
        def _():
            for c in tile_copies(nxt_ref[b], jn):
                c.start()


def _for_valid_rows(used, bv_ref, out_ref, compute):
    tm = out_ref.shape[0]
    half = tm // 2
    few = bv_ref[pl.program_id(1)] <= half

    @pl.when(jnp.logical_and(used, jnp.logical_not(few)))
    def _():
        compute(tm)

    @pl.when(jnp.logical_and(used, few))
    def _():
        compute(half)
        out_ref[half:, :] = jnp.zeros((tm - half, out_ref.shape[1]), out_ref.dtype)


def _moe1_kernel(be_ref, nb_ref, nxt_ref, wrap_ref, bv_ref, xs_ref, w_hbm, bg_ref, bl_ref, h_ref, sg, sl, wg_s, wl_s,
                 sems, *, tn):
    used, fresh = _expert_changed(be_ref, nb_ref)
    n_lin = pl.num_programs(0)

    def tile_copies(e, jt):
        return (pltpu.make_async_copy(w_hbm.at[e, :, pl.ds(pl.multiple_of(jt * tn, tn), tn)], sg, sems.at[0]),
                pltpu.make_async_copy(w_hbm.at[e, :, pl.ds(pl.multiple_of((n_lin + jt) * tn, tn), tn)], sl,
                                      sems.at[1]))

    _stream_expert_weights(be_ref, nxt_ref, wrap_ref, used, fresh, tile_copies, (sg, sl), (wg_s, wl_s))

    @pl.when(jnp.logical_not(used))
    def _():
        h_ref[...] = jnp.zeros_like(h_ref)

    def compute(rows):
        x = xs_ref[:rows, :]
        gate = jnp.minimum(_dot(x, wg_s[...]) + bg_ref[...], SWIGLU_LIMIT)
        lin = jnp.clip(_dot(x, wl_s[...]) + bl_ref[...], -SWIGLU_LIMIT, SWIGLU_LIMIT)
        h_ref[:rows, :] = (gate * _sigmoid(SWIGLU_ALPHA * gate) * (lin + 1.0)).astype(h_ref.dtype)

    _for_valid_rows(used, bv_ref, h_ref, compute)


def _moe1(sched, xs, w_gu, b_gu, tm, tn):
    n_rows, dh = xs.shape
    n_exp, d, de2 = w_gu.shape
    de = de2 // 2
    tn = _pick(de, tn)
    nj = de // tn

    def blk(b, nb):
        return jnp.minimum(b, nb[0] - 1)

    grid_spec = pltpu.PrefetchScalarGridSpec(
        num_scalar_prefetch=5,
        grid=(nj, n_rows // tm),
        in_specs=[pl.BlockSpec((tm, dh), lambda j, b, be, nb, *_: (blk(b, nb), 0)),
                  pl.BlockSpec(memory_space=pl.ANY),
                  pl.BlockSpec((None, 1, tn), lambda j, b, be, nb, *_: (be[blk(b, nb)], 0, j)),
                  pl.BlockSpec((None, 1, tn), lambda j, b, be, nb, *_: (be[blk(b, nb)], 0, nj + j))],
        out_specs=pl.BlockSpec((tm, tn), lambda j, b, *_: (b, j)),
        scratch_shapes=[pltpu.VMEM((d, tn), F32), pltpu.VMEM((d, tn), F32),
                        pltpu.VMEM((d, tn), BF16), pltpu.VMEM((d, tn), BF16),
                        pltpu.SemaphoreType.DMA((2,))],
    )
    return pl.pallas_call(
        functools.partial(_moe1_kernel, tn=tn),
        grid_spec=grid_spec,
        out_shape=jax.ShapeDtypeStruct((n_rows, de), BF16),
        compiler_params=_params("arbitrary", "arbitrary"),
        name="moe_up",
    )(*sched, xs, w_gu, b_gu, b_gu)


def _moe2_kernel(be_ref, nb_ref, nxt_ref, wrap_ref, bv_ref, h_ref, w_hbm, b_ref, y_ref, stage, w_s, sems, *, tn):
    used, fresh = _expert_changed(be_ref, nb_ref)

    def tile_copies(e, jt):
        return (pltpu.make_async_copy(w_hbm.at[e, :, pl.ds(pl.multiple_of(jt * tn, tn), tn)], stage, sems.at[0]),)

    _stream_expert_weights(be_ref, nxt_ref, wrap_ref, used, fresh, tile_copies, (stage,), (w_s,))

    @pl.when(jnp.logical_not(used))
    def _():
        y_ref[...] = jnp.zeros_like(y_ref)

    def compute(rows):
        y_ref[:rows, :] = _dot(h_ref[:rows, :], w_s[...]) + b_ref[...]

    _for_valid_rows(used, bv_ref, y_ref, compute)


def _moe2(sched, h, w_down, b_down, tm, tn):
    n_rows, de = h.shape
    n_exp, _, d = w_down.shape
    tn = _pick(d, tn)

    def blk(b, nb):
        return jnp.minimum(b, nb[0] - 1)

    grid_spec = pltpu.PrefetchScalarGridSpec(
        num_scalar_prefetch=5,
        grid=(d // tn, n_rows // tm),
        in_specs=[pl.BlockSpec((tm, de), lambda j, b, be, nb, *_: (blk(b, nb), 0)),
                  pl.BlockSpec(memory_space=pl.ANY),
                  pl.BlockSpec((None, 1, tn), lambda j, b, be, nb, *_: (be[blk(b, nb)], 0, j))],
        out_specs=pl.BlockSpec((tm, tn), lambda j, b, *_: (b, j)),
        scratch_shapes=[pltpu.VMEM((de, tn), F32), pltpu.VMEM((de, tn), BF16), pltpu.SemaphoreType.DMA((1,))],
    )
    return pl.pallas_call(
        functools.partial(_moe2_kernel, tn=tn),
        grid_spec=grid_spec,
        out_shape=jax.ShapeDtypeStruct((n_rows, d), F32),
        compiler_params=_params("arbitrary", "arbitrary"),
        name="moe_down",
    )(*sched, h, w_down, b_down)


def _combine_kernel(dest_ref, dest_nxt_ref, ys_hbm, x1_ref, wgt_ref, g_ref, b_ref, o_ref, buf, sems, *, tc, top_k,
                    alpha):
    i = pl.program_id(0)
    slot = i % 2

    def row_copy(idx_ref, s, t, j):
        return pltpu.make_async_copy(ys_hbm.at[pl.ds(idx_ref[t * top_k + j], 1)], buf.at[s, j, pl.ds(t, 1)],
                                     sems.at[s])

    def issue(idx_ref, s):
        def body(t, c):
            for j in range(top_k):
                row_copy(idx_ref, s, t, j).start()
            return c
        lax.fori_loop(0, tc, body, 0, unroll=2)

    @pl.when(i == 0)
    def _():
        issue(dest_ref, slot)

    @pl.when(i + 1 < pl.num_programs(0))
    def _():
        issue(dest_nxt_ref, 1 - slot)

    def drain(t, c):
        for j in range(top_k):
            row_copy(dest_ref, slot, t, j).wait()
        return c

    lax.fori_loop(0, tc, drain, 0, unroll=2)
    y = wgt_ref[:, 0:1] * buf[slot, 0]
    for j in range(1, top_k):
        y = y + wgt_ref[:, j:j + 1] * buf[slot, j]
    o_ref[...] = _layer_norm(alpha * x1_ref[...] + y, g_ref[...], b_ref[...])


def _combine(dest_flat, ys, x1, wgt, g, b, alpha, tc):
    t, d = x1.shape
    tc = _pick(t, tc)
    last = t // tc - 1
    return pl.pallas_call(
        functools.partial(_combine_kernel, tc=tc, top_k=TOP_K, alpha=alpha),
        grid=(t // tc,),
        in_specs=[pl.BlockSpec((tc * TOP_K,), lambda i: (i,), memory_space=pltpu.SMEM),
                  pl.BlockSpec((tc * TOP_K,), lambda i: (jnp.minimum(i + 1, last),), memory_space=pltpu.SMEM),
                  pl.BlockSpec(memory_space=pl.ANY),
                  pl.BlockSpec((tc, d), lambda i: (i, 0)),
                  pl.BlockSpec((tc, LANES), lambda i: (i, 0)),
                  pl.BlockSpec((1, d), lambda i: (0, 0)),
                  pl.BlockSpec((1, d), lambda i: (0, 0))],
        out_specs=pl.BlockSpec((tc, d), lambda i: (i, 0)),
        out_shape=jax.ShapeDtypeStruct((t, d), F32),
        scratch_shapes=[pltpu.VMEM((2, TOP_K, tc, d), F32), pltpu.SemaphoreType.DMA((2,))],
        compiler_params=_params("arbitrary"),
        name="combine",
    )(dest_flat, dest_flat, ys, x1, wgt, g, b)


def _layer(x2d, w_in, b_in, a_ws, a_bs, a_ln_g, a_ln_b, gla_w_lr, gla_b_lr, gla_gn_g, w_br_a, w_br_b, w_o,
           ln1_g, ln1_b, w_router, b_router, w_gu, b_gu, w_down, b_down, ln2_g, ln2_b, alpha):
    t, d = x2d.shape
    n_groups, a_chunk, _ = a_ws.shape
    aw = a_ln_g.size
    rank, kw = gla_w_lr.shape
    heads, dv = gla_gn_g.shape
    vw = heads * dv
    n_exp = w_router.shape[1]
    de = w_down.shape[1]
    off_lr = 2 * aw + 2 * kw + 2 * vw
    off_q = 2 * aw

    w_in_t = w_in.T
    w_lr_in = jnp.pad(w_in[:, off_lr:off_lr + rank], ((0, 0), (0, LANES - rank))).astype(BF16)
    b_lr_in = jnp.pad(b_in[off_lr:off_lr + rank], (0, LANES - rank)).reshape(1, LANES)
    wlr = jnp.pad(gla_w_lr, ((0, LANES - rank), (0, 0))).astype(BF16)
    wr = jnp.pad(w_router, ((0, 0), (0, LANES - n_exp))).astype(BF16)
    br = jnp.pad(b_router, (0, LANES - n_exp), constant_values=-jnp.inf).reshape(1, LANES)

    xb, lr = _cast_and_decay_proj(x2d, w_lr_in, b_lr_in, 512)
    z = _matmul_bias_stream(xb, w_in_t, b_in[:off_lr].reshape(1, -1), 0, F32, 1024, 1024)
    zg = _matmul_bias_stream(xb, w_in_t, b_in[off_lr + rank:].reshape(1, -1), off_lr + rank, F32, 1024, 1024)

    ya = _branch_a(z, a_ws, a_bs, a_ln_g, a_ln_b, aw, 2 * a_chunk)
    yb = _gla(z, lr, wlr, gla_b_lr.reshape(1, kw), gla_gn_g.reshape(1, vw), kw, vw, off_q, heads, 256)
    merged = _merge(ya, w_br_a.astype(BF16), yb, w_br_b.astype(BF16), zg, 0, d, 1024, 512)
    h1 = _out_proj_resid(merged, w_o.astype(BF16), x2d, alpha, 1024, 512)

    x1, xp, idx, wgt, rnk, cnt = _ln_router(h1, ln1_g.reshape(1, d), ln1_b.reshape(1, d), wr, br, 256)

    tm_e = 512 if t * TOP_K >= 512 * n_exp else 64
    n_assign = t * TOP_K
    n_blocks = -(-(n_assign + n_exp * (tm_e - 1)) // tm_e)
    n_rows = n_blocks * tm_e
    counts = cnt[0, :n_exp].astype(I32)
    padded = (counts + tm_e - 1) // tm_e * tm_e
    pend = jnp.cumsum(padded)
    pstart = pend - padded
    sel = idx[:, :TOP_K, None] == jnp.arange(n_exp, dtype=I32)[None, None, :]
    dest = jnp.sum(jnp.where(sel, pstart[None, None, :], 0), axis=-1) + rnk[:, :TOP_K]
    dest_flat = dest.reshape(-1).astype(I32)
    nb_used = (pend[-1] // tm_e).astype(I32).reshape(1)
    block_start = jnp.arange(n_blocks, dtype=I32) * tm_e
    block_e = jnp.minimum(jnp.sum((pend[None, :] <= block_start[:, None]).astype(I32), axis=1), n_exp - 1)

    tok_buf = jnp.zeros((n_rows,), I32).at[dest_flat].set(jnp.arange(n_assign, dtype=I32) // TOP_K,
                                                          unique_indices=True)
    xs = _dispatch(tok_buf, nb_used, xp, tm_e)
    experts = jnp.arange(n_exp, dtype=I32)
    later = jnp.where((padded > 0)[None, :] & (experts[None, :] > block_e[:, None]), experts[None, :], n_exp)
    nxt = jnp.min(later, axis=1)
    wrap = (nxt == n_exp).astype(I32)
    nxt_e = jnp.where(nxt == n_exp, block_e[0], nxt).astype(I32)
    mine = experts[None, :] == block_e[:, None]
    group_end = jnp.sum(jnp.where(mine, (pstart + counts)[None, :], 0), axis=1)
    bvalid = jnp.clip(group_end - block_start, 0, tm_e).astype(I32)
    sched = (block_e, nb_used, nxt_e, wrap, bvalid)
    hmid = _moe1(sched, xs, w_gu, b_gu.reshape(n_exp, 1, 2 * de), tm_e, 768)
    ys = _moe2(sched, hmid, w_down, b_down.reshape(n_exp, 1, d), tm_e, 2048)
    return _combine(dest_flat, ys, x1, wgt, ln2_g.reshape(1, d), ln2_b.reshape(1, d), alpha, 128)


def kernel(x, w_in, b_in, a_ws, a_bs, a_ln_g, a_ln_b, gla_w_lr, gla_b_lr, gla_gn_g, w_br_a, w_br_b, w_o, ln1_g, ln1_b, w_router, b_router, w_gu, b_gu, w_down, b_down, ln2_g, ln2_b):
    bsz, seq, d = x.shape
    depth = w_in.shape[0]
    alpha = (2 * depth) ** 0.25
    outs = []
    for bi in range(bsz):
        h = x[bi]
        for l in range(depth):
            h = _layer(h, w_in[l], b_in[l], a_ws[l], a_bs[l], a_ln_g[l], a_ln_b[l], gla_w_lr[l], gla_b_lr[l],
                       gla_gn_g[l], w_br_a[l], w_br_b[l], w_o[l], ln1_g[l], ln1_b[l], w_router[l], b_router[l],
                       w_gu[l], b_gu[l], w_down[l], b_down[l], ln2_g[l], ln2_b[l], alpha)
        outs.append(h)
    return jnp.stack(outs) if bsz > 1 else outs[0][None]
```

```python
import functools

import jax
import jax.numpy as jnp
from jax import lax
from jax.experimental import pallas as pl
from jax.experimental.pallas import tpu as pltpu

F32 = jnp.float32
BF16 = jnp.bfloat16
U32 = jnp.uint32
I32 = jnp.int32

LN_EPS = 1e-5
GLA_CHUNK = 64
GLA_GATE_TAU = 16.0
TOP_K = 4
SWIGLU_LIMIT = 7.0
SWIGLU_ALPHA = 1.702
LANES = 128
VMEM_LIMIT = 56 * 1024 * 1024
HI_MASK = 0xFFFF0000
DISPATCH_STRIPES = 8

def _pick(n, pref):
    t = min(pref, n)
    while n % t:
        t //= 2
    return t


def _dot(a, b):
    return jnp.dot(a, b, preferred_element_type=F32)


def _dot_t0(a, b):
    return lax.dot_general(a, b, (((0,), (0,)), ((), ())), preferred_element_type=F32)


def _dot_t1(a, b):
    return lax.dot_general(a, b, (((1,), (1,)), ((), ())), preferred_element_type=F32)


def _sigmoid(x):
    return 1.0 / (1.0 + jnp.exp(-x))


def _gelu(x):
    return 0.5 * x * (1.0 + lax.erf(x * (2.0 ** -0.5)))


def _layer_norm(x, g, b):
    mu = jnp.mean(x, axis=-1, keepdims=True)
    xc = x - mu
    var = jnp.mean(xc * xc, axis=-1, keepdims=True)
    return xc * lax.rsqrt(var + LN_EPS) * g + b


def _params(*sem):
    return pltpu.CompilerParams(dimension_semantics=sem, vmem_limit_bytes=VMEM_LIMIT)


def _cast_lr_kernel(x_ref, w_ref, b_ref, xb_ref, lr_ref):
    xb = x_ref[...].astype(BF16)
    xb_ref[...] = xb
    lr_ref[...] = _dot(xb, w_ref[...]) + b_ref[...]


def _cast_and_decay_proj(x, w, b, tm):
    m, k = x.shape
    n = w.shape[1]
    tm = _pick(m, tm)
    return pl.pallas_call(
        _cast_lr_kernel,
        grid=(m // tm,),
        in_specs=[pl.BlockSpec((tm, k), lambda i: (i, 0)),
                  pl.BlockSpec((k, n), lambda i: (0, 0)),
                  pl.BlockSpec((1, n), lambda i: (0, 0))],
        out_specs=[pl.BlockSpec((tm, k), lambda i: (i, 0)),
                   pl.BlockSpec((tm, n), lambda i: (i, 0))],
        out_shape=[jax.ShapeDtypeStruct((m, k), BF16), jax.ShapeDtypeStruct((m, n), F32)],
        compiler_params=_params("parallel"),
        name="cast_lr",
    )(x, w, b)


def _mm_stream_kernel(x_ref, wt_hbm, b_ref, o_ref, stage, work, sem, *, tn, row0):
    j, i = pl.program_id(0), pl.program_id(1)

    def w_copy(jt):
        return pltpu.make_async_copy(wt_hbm.at[pl.ds(pl.multiple_of(row0 + jt * tn, 16), tn), :], stage, sem.at[0])

    @pl.when(jnp.logical_and(j == 0, i == 0))
    def _():
        w_copy(0).start()

    @pl.when(i == 0)
    def _():
        w_copy(j).wait()
        work[...] = stage[...].astype(BF16)

        @pl.when(j + 1 < pl.num_programs(0))
        def _():
            w_copy(j + 1).start()

    o_ref[...] = (_dot_t1(x_ref[...], work[...]) + b_ref[...]).astype(o_ref.dtype)


def _matmul_bias_stream(xb, wt, b, row0, out_dtype, tm, tn):
    m, k = xb.shape
    n = b.shape[1]
    tm, tn = _pick(m, tm), _pick(n, tn)
    assert row0 % 16 == 0 and tn % 16 == 0
    return pl.pallas_call(
        functools.partial(_mm_stream_kernel, tn=tn, row0=row0),
        grid=(n // tn, m // tm),
        in_specs=[pl.BlockSpec((tm, k), lambda j, i: (i, 0)),
                  pl.BlockSpec(memory_space=pl.ANY),
                  pl.BlockSpec((1, tn), lambda j, i: (0, j))],
        out_specs=pl.BlockSpec((tm, tn), lambda j, i: (i, j)),
        out_shape=jax.ShapeDtypeStruct((m, n), out_dtype),
        scratch_shapes=[pltpu.VMEM((tn, k), F32), pltpu.VMEM((tn, k), BF16), pltpu.SemaphoreType.DMA((1,))],
        compiler_params=_params("arbitrary", "arbitrary"),
        name="in_proj",
    )(xb, wt, b)


def _merge_kernel(ya_ref, wa_ref, yb_ref, wb_ref, ga_ref, gb_ref, o_ref):
    a = _sigmoid(ga_ref[...]) * _dot(ya_ref[...], wa_ref[...])
    o_ref[...] = (a + _sigmoid(gb_ref[...]) * _dot(yb_ref[...], wb_ref[...])).astype(o_ref.dtype)


def _merge(ya, wa, yb, wb, z, off_ga, off_gb, tm, tn):
    m, ka = ya.shape
    kb = yb.shape[1]
    n = wa.shape[1]
    tm, tn = _pick(m, tm), _pick(n, tn)
    while off_ga % tn or off_gb % tn:
        tn //= 2
    oa, ob = off_ga // tn, off_gb // tn
    return pl.pallas_call(
        _merge_kernel,
        grid=(m // tm, n // tn),
        in_specs=[pl.BlockSpec((tm, ka), lambda i, j: (i, 0)),
                  pl.BlockSpec((ka, tn), lambda i, j: (0, j)),
                  pl.BlockSpec((tm, kb), lambda i, j: (i, 0)),
                  pl.BlockSpec((kb, tn), lambda i, j: (0, j)),
                  pl.BlockSpec((tm, tn), lambda i, j: (i, oa + j)),
                  pl.BlockSpec((tm, tn), lambda i, j: (i, ob + j))],
        out_specs=pl.BlockSpec((tm, tn), lambda i, j: (i, j)),
        out_shape=jax.ShapeDtypeStruct((m, n), BF16),
        compiler_params=_params("parallel", "parallel"),
        name="merge",
    )(ya, wa, yb, wb, z, z)


def _mm_resid_kernel(m_ref, w_ref, x_ref, o_ref, *, alpha):
    o_ref[...] = alpha * x_ref[...] + _dot(m_ref[...], w_ref[...])


def _out_proj_resid(mb, w, x, alpha, tm, tn):
    m, k = mb.shape
    n = w.shape[1]
    tm, tn = _pick(m, tm), _pick(n, tn)
    return pl.pallas_call(
        functools.partial(_mm_resid_kernel, alpha=alpha),
        grid=(m // tm, n // tn),
        in_specs=[pl.BlockSpec((tm, k), lambda i, j: (i, 0)),
                  pl.BlockSpec((k, tn), lambda i, j: (0, j)),
                  pl.BlockSpec((tm, tn), lambda i, j: (i, j))],
        out_specs=pl.BlockSpec((tm, tn), lambda i, j: (i, j)),
        out_shape=jax.ShapeDtypeStruct((m, n), F32),
        compiler_params=_params("parallel", "parallel"),
        name="out_proj",
    )(mb, w, x)


def _branch_a_kernel(u_ref, v_ref, ws_ref, bs_ref, g_ref, b_ref, o_ref, *, n_groups, chunk):
    tm, aw = u_ref.shape
    gd = aw // n_groups
    row = lax.broadcasted_iota(I32, (chunk, chunk), 0)
    col = lax.broadcasted_iota(I32, (chunk, chunk), 1)
    causal = row >= col
    for g in range(n_groups):
        ws = jnp.where(causal, ws_ref[g], 0.0).astype(BF16)
        cs = slice(g * gd, (g + 1) * gd)
        for c in range(tm // chunk):
            rs = slice(c * chunk, (c + 1) * chunk)
            vn = _layer_norm(_gelu(v_ref[rs, cs]), g_ref[:, cs], b_ref[:, cs])
            mixed = _dot(ws, vn.astype(BF16)) + bs_ref[:, g:g + 1]
            o_ref[rs, cs] = (_gelu(u_ref[rs, cs]) * mixed).astype(o_ref.dtype)


def _branch_a(z, a_ws, a_bs, a_ln_g, a_ln_b, aw, tm):
    t = z.shape[0]
    n_groups, chunk, _ = a_ws.shape
    tm = max(_pick(t, tm), chunk)
    return pl.pallas_call(
        functools.partial(_branch_a_kernel, n_groups=n_groups, chunk=chunk),
        grid=(t // tm,),
        in_specs=[pl.BlockSpec((tm, aw), lambda i: (i, 0)),
                  pl.BlockSpec((tm, aw), lambda i: (i, 1)),
                  pl.BlockSpec((n_groups, chunk, chunk), lambda i: (0, 0, 0)),
                  pl.BlockSpec((chunk, n_groups), lambda i: (0, 0)),
                  pl.BlockSpec((1, aw), lambda i: (0, 0)),
                  pl.BlockSpec((1, aw), lambda i: (0, 0))],
        out_specs=pl.BlockSpec((tm, aw), lambda i: (i, 0)),
        out_shape=jax.ShapeDtypeStruct((t, aw), BF16),
        compiler_params=_params("parallel"),
        name="branch_a",
    )(z, z, a_ws, a_bs.T, a_ln_g.reshape(1, aw), a_ln_b.reshape(1, aw))


def _gla_kernel(q_ref, k_ref, v_ref, r_ref, lr_ref, wlr_ref, blr_ref, gn_ref, o_ref, s_ref, *, heads, chunk):
    @pl.when(pl.program_id(0) == 0)
    def _():
        s_ref[...] = jnp.zeros_like(s_ref)

    tg, kw = q_ref.shape
    vw = v_ref.shape[1]
    dk, dv = kw // heads, vw // heads
    row = lax.broadcasted_iota(I32, (chunk, chunk), 0)
    col = lax.broadcasted_iota(I32, (chunk, chunk), 1)
    causal = row >= col
    tri = causal.astype(BF16)
    ones = jnp.ones((chunk, LANES), BF16)
    scale = dk ** -0.5

    def body(c, carry):
        rs = pl.ds(pl.multiple_of(c * chunk, chunk), chunk)
        gl = _dot(lr_ref[rs, :].astype(BF16), wlr_ref[...]) + blr_ref[...]
        la = (jnp.minimum(gl, 0.0) - jnp.log1p(jnp.exp(-jnp.abs(gl)))) * (1.0 / GLA_GATE_TAU)
        la_hi = la.astype(BF16)
        la_lo = (la - la_hi.astype(F32)).astype(BF16)
        cum = _dot(tri, la_hi) + _dot(tri, la_lo)
        tot = cum[chunk - 1:chunk, :]
        tot_col = _dot_t0(la_hi, ones) + _dot_t0(la_lo, ones)
        dec_col = jnp.exp(tot_col)
        q = q_ref[rs, :] * scale
        k = k_ref[rs, :]
        qt = (q * jnp.exp(cum)).astype(BF16)
        kt = (k * jnp.exp(-cum)).astype(BF16)
        kl = (k * jnp.exp(tot - cum)).astype(BF16)
        for h in range(heads):
            ks = slice(h * dk, (h + 1) * dk)
            vs = slice(h * dv, (h + 1) * dv)
            vh = v_ref[rs, vs].astype(BF16)
            sc = jnp.where(causal, _dot_t1(qt[:, ks], kt[:, ks]), 0.0).astype(BF16)
            state = s_ref[h]
            o = _dot(sc, vh) + _dot(qt[:, ks], state.astype(BF16))
            dec = jnp.concatenate([dec_col[ks, :]] * (dv // LANES), axis=1) if dv >= LANES else dec_col[ks, :dv]
            s_ref[h] = dec * state + _dot_t0(kl[:, ks], vh)
            on = o * lax.rsqrt(jnp.mean(o * o, axis=-1, keepdims=True) + LN_EPS) * gn_ref[:, vs]
            rr = r_ref[rs, vs]
            o_ref[rs, vs] = (on * (rr * _sigmoid(rr))).astype(o_ref.dtype)
        return carry

    lax.fori_loop(0, tg // chunk, body, 0)


def _gla(z, lr, wlr, blr, gn, kw, vw, off_q, heads, tg):
    t = z.shape[0]
    tg = max(_pick(t, tg), GLA_CHUNK)
    oq = off_q // kw
    ok = oq + 1
    ov = (off_q + 2 * kw) // vw
    orr = ov + 1
    return pl.pallas_call(
        functools.partial(_gla_kernel, heads=heads, chunk=GLA_CHUNK),
        grid=(t // tg,),
        in_specs=[pl.BlockSpec((tg, kw), lambda i: (i, oq)),
                  pl.BlockSpec((tg, kw), lambda i: (i, ok)),
                  pl.BlockSpec((tg, vw), lambda i: (i, ov)),
                  pl.BlockSpec((tg, vw), lambda i: (i, orr)),
                  pl.BlockSpec((tg, LANES), lambda i: (i, 0)),
                  pl.BlockSpec((LANES, kw), lambda i: (0, 0)),
                  pl.BlockSpec((1, kw), lambda i: (0, 0)),
                  pl.BlockSpec((1, vw), lambda i: (0, 0))],
        out_specs=pl.BlockSpec((tg, vw), lambda i: (i, 0)),
        out_shape=jax.ShapeDtypeStruct((t, vw), BF16),
        scratch_shapes=[pltpu.VMEM((heads, kw // heads, vw // heads), F32)],
        compiler_params=_params("arbitrary"),
        name="gla",
    )(z, z, z, z, lr, wlr, blr, gn)


def _ln_router_kernel(h_ref, g_ref, b_ref, wr_ref, br_ref, x1_ref, xp_ref, idx_ref, wgt_ref, rank_ref, cnt_ref,
                      base_ref, *, top_k):
    @pl.when(pl.program_id(0) == 0)
    def _():
        base_ref[...] = jnp.zeros_like(base_ref)

    tm, d = h_ref.shape
    half = d // 2
    x1 = _layer_norm(h_ref[...], g_ref[...], b_ref[...])
    x1_ref[...] = x1
    lo = pltpu.bitcast(x1[:, :half].astype(BF16).astype(F32), U32) >> 16
    hi = pltpu.bitcast(x1[:, half:].astype(BF16).astype(F32), U32) & jnp.uint32(HI_MASK)
    xp_ref[...] = lo | hi

    logits = _dot(x1.astype(BF16), wr_ref[...]) + br_ref[...]
    lane = lax.broadcasted_iota(I32, (tm, LANES), 1).astype(F32)
    member = jnp.zeros((tm, LANES), F32)
    vals, idxs = [], []
    cur = logits
    for _ in range(top_k):
        mx = jnp.max(cur, axis=-1, keepdims=True)
        ix = jnp.min(jnp.where(cur == mx, lane, float(LANES)), axis=-1, keepdims=True)
        sel = lane == ix
        vals.append(mx)
        idxs.append(ix)
        member = member + sel.astype(F32)
        cur = jnp.where(sel, -jnp.inf, cur)
    exps = [jnp.exp(v - vals[0]) for v in vals]
    denom = exps[0]
    for e in exps[1:]:
        denom = denom + e
    r_i = lax.broadcasted_iota(I32, (tm, tm), 0)
    c_i = lax.broadcasted_iota(I32, (tm, tm), 1)
    before = _dot((r_i > c_i).astype(BF16), member.astype(BF16)) + base_ref[...]
    idx_out = jnp.zeros((tm, LANES), F32)
    wgt_out = jnp.zeros((tm, LANES), F32)
    rank_out = jnp.zeros((tm, LANES), F32)
    for j in range(top_k):
        rk = jnp.sum(jnp.where(lane == idxs[j], before, 0.0), axis=-1, keepdims=True)
        at = lane == float(j)
        idx_out = jnp.where(at, idxs[j], idx_out)
        wgt_out = jnp.where(at, exps[j] / denom, wgt_out)
        rank_out = jnp.where(at, rk, rank_out)
    idx_ref[...] = idx_out.astype(I32)
    wgt_ref[...] = wgt_out
    rank_ref[...] = rank_out.astype(I32)
    base_ref[...] = base_ref[...] + jnp.sum(member, axis=0, keepdims=True)
    cnt_ref[...] = base_ref[...]


def _ln_router(h1, g, b, wr, br, tm):
    t, d = h1.shape
    tm = _pick(t, tm)
    row = lambda i: (i, 0)
    fixed = lambda i: (0, 0)
    return pl.pallas_call(
        functools.partial(_ln_router_kernel, top_k=TOP_K),
        grid=(t // tm,),
        in_specs=[pl.BlockSpec((tm, d), row),
                  pl.BlockSpec((1, d), fixed),
                  pl.BlockSpec((1, d), fixed),
                  pl.BlockSpec((d, LANES), fixed),
                  pl.BlockSpec((1, LANES), fixed)],
        out_specs=[pl.BlockSpec((tm, d), row),
                   pl.BlockSpec((tm, d // 2), row),
                   pl.BlockSpec((tm, LANES), row),
                   pl.BlockSpec((tm, LANES), row),
                   pl.BlockSpec((tm, LANES), row),
                   pl.BlockSpec((1, LANES), fixed)],
        out_shape=[jax.ShapeDtypeStruct((t, d), F32),
                   jax.ShapeDtypeStruct((t, d // 2), U32),
                   jax.ShapeDtypeStruct((t, LANES), I32),
                   jax.ShapeDtypeStruct((t, LANES), F32),
                   jax.ShapeDtypeStruct((t, LANES), I32),
                   jax.ShapeDtypeStruct((1, LANES), F32)],
        scratch_shapes=[pltpu.VMEM((1, LANES), F32)],
        compiler_params=_params("arbitrary"),
        name="ln_router",
    )(h1, g, b, wr, br)


def _expert_changed(be_ref, nb_ref):
    b = pl.program_id(1)
    fresh = jnp.logical_or(b == 0, be_ref[b] != be_ref[jnp.maximum(b - 1, 0)])
    return b < nb_ref[0], fresh


def _stream_expert_weights(be_ref, nxt_ref, wrap_ref, used, fresh, tile_copies, stage, work, j0):
    j, b = pl.program_id(0), pl.program_id(1)
    nj = pl.num_programs(0)

    @pl.when(jnp.logical_and(j == 0, b == 0))
    def _():
        for c in tile_copies(be_ref[0], j0):
            c.start()

    @pl.when(jnp.logical_and(used, fresh))
    def _():
        for c in tile_copies(be_ref[b], j0 + j):
            c.wait()
        for src, dst in zip(stage, work):
            dst[...] = src[...].astype(BF16)
        jn = j + wrap_ref[b]

        @pl.when(jn < nj)
        def _():
            for c in tile_copies(nxt_ref[b], j0 + jn):
                c.start()


def _for_valid_rows(used, bv_ref, out_ref, compute):
    tm = out_ref.shape[0]
    half = tm // 2
    few = bv_ref[pl.program_id(1)] <= half

    @pl.when(jnp.logical_and(used, jnp.logical_not(few)))
    def _():
        compute(tm)

    @pl.when(jnp.logical_and(used, few))
    def _():
        compute(half)
        out_ref[half:, :] = jnp.zeros((tm - half, out_ref.shape[1]), out_ref.dtype)


def _request_rows(nb_ref, tok_ref, tok_nxt_ref, xp_hbm, buf, sems, tm_e):
    b = pl.program_id(1)
    slot = b % 2
    stripe = tm_e // DISPATCH_STRIPES

    def row_copy(idx_ref, s, u, k):
        return pltpu.make_async_copy(xp_hbm.at[pl.ds(idx_ref[u * stripe + k], 1)], buf.at[s, u, pl.ds(k, 1)],
                                     sems.at[s])

    def issue(idx_ref, s):
        def body(k, c):
            for u in range(DISPATCH_STRIPES):
                row_copy(idx_ref, s, u, k).start()
            return c
        lax.fori_loop(0, stripe, body, 0)

    @pl.when(b == 0)
    def _():
        issue(tok_ref, slot)

    @pl.when(b + 1 < nb_ref[0])
    def _():
        issue(tok_nxt_ref, 1 - slot)

    def wait_rows():
        def drain(k, c):
            for u in range(DISPATCH_STRIPES):
                row_copy(tok_ref, slot, u, k).wait()
            return c

        lax.fori_loop(0, stripe, drain, 0)
        return buf[slot].reshape(tm_e, buf.shape[-1])

    return wait_rows


def _gate_lin_copies(w_hbm, sg, sl, sems, tn, n_col_tiles):
    def tile_copies(e, jt):
        return (pltpu.make_async_copy(w_hbm.at[e, :, pl.ds(pl.multiple_of(jt * tn, tn), tn)], sg, sems.at[0]),
                pltpu.make_async_copy(w_hbm.at[e, :, pl.ds(pl.multiple_of((n_col_tiles + jt) * tn, tn), tn)], sl,
                                      sems.at[1]))
    return tile_copies


def _swiglu_rows(xs_ref, wg_s, wl_s, bg_ref, bl_ref, h_ref, rows):
    x = xs_ref[:rows, :]
    gate = jnp.minimum(_dot(x, wg_s[...]) + bg_ref[...], SWIGLU_LIMIT)
    lin = jnp.clip(_dot(x, wl_s[...]) + bl_ref[...], -SWIGLU_LIMIT, SWIGLU_LIMIT)
    h_ref[:rows, :] = (gate * _sigmoid(SWIGLU_ALPHA * gate) * (lin + 1.0)).astype(h_ref.dtype)


def _moe1_first_kernel(be_ref, nb_ref, nxt_ref, wrap_ref, bv_ref, tok_ref, tok_nxt_ref, xp_hbm, w_hbm, bg_ref, bl_ref,
                       xs_ref, h_ref, gbuf, sg, sl, wg_s, wl_s, gsems, sems, *, tn, n_col_tiles, tm_e):
    used, fresh = _expert_changed(be_ref, nb_ref)
    _stream_expert_weights(be_ref, nxt_ref, wrap_ref, used, fresh,
                           _gate_lin_copies(w_hbm, sg, sl, sems, tn, n_col_tiles), (sg, sl), (wg_s, wl_s), 0)
    wait_rows = _request_rows(nb_ref, tok_ref, tok_nxt_ref, xp_hbm, gbuf, gsems, tm_e)

    @pl.when(jnp.logical_not(used))
    def _():
        xs_ref[...] = jnp.zeros_like(xs_ref)
        h_ref[...] = jnp.zeros_like(h_ref)

    @pl.when(used)
    def _():
        w = wait_rows()
        half = w.shape[1]
        xs_ref[:, :half] = pltpu.bitcast(w << 16, F32).astype(BF16)
        xs_ref[:, half:] = pltpu.bitcast(w & jnp.uint32(HI_MASK), F32).astype(BF16)

    _for_valid_rows(used, bv_ref, h_ref, functools.partial(_swiglu_rows, xs_ref, wg_s, wl_s, bg_ref, bl_ref, h_ref))


def _moe1_rest_kernel(be_ref, nb_ref, nxt_ref, wrap_ref, bv_ref, xs_ref, w_hbm, bg_ref, bl_ref, h_ref, sg, sl, wg_s,
                      wl_s, sems, *, tn, n_col_tiles):
    used, fresh = _expert_changed(be_ref, nb_ref)
    _stream_expert_weights(be_ref, nxt_ref, wrap_ref, used, fresh,
                           _gate_lin_copies(w_hbm, sg, sl, sems, tn, n_col_tiles), (sg, sl), (wg_s, wl_s), 1)

    @pl.when(jnp.logical_not(used))
    def _():
        h_ref[...] = jnp.zeros_like(h_ref)

    _for_valid_rows(used, bv_ref, h_ref, functools.partial(_swiglu_rows, xs_ref, wg_s, wl_s, bg_ref, bl_ref, h_ref))


def _blk(b, nb):
    return jnp.minimum(b, nb[0] - 1)


def _moe1(sched, tok_buf, xp, w_gu, b_gu, tm, tn):
    dh = xp.shape[1]
    n_rows = tok_buf.shape[0]
    n_exp, d, de2 = w_gu.shape
    de = de2 // 2
    tn = _pick(de, tn)
    nj = de // tn
    assert nj >= 2
    last = n_rows // tm - 1
    weight_scratch = [pltpu.VMEM((d, tn), F32), pltpu.VMEM((d, tn), F32),
                      pltpu.VMEM((d, tn), BF16), pltpu.VMEM((d, tn), BF16)]

    first_spec = pltpu.PrefetchScalarGridSpec(
        num_scalar_prefetch=5,
        grid=(1, n_rows // tm),
        in_specs=[pl.BlockSpec((tm,), lambda j, b, *_: (b,), memory_space=pltpu.SMEM),
                  pl.BlockSpec((tm,), lambda j, b, *_: (jnp.minimum(b + 1, last),), memory_space=pltpu.SMEM),
                  pl.BlockSpec(memory_space=pl.ANY),
                  pl.BlockSpec(memory_space=pl.ANY),
                  pl.BlockSpec((None, 1, tn), lambda j, b, be, nb, *_: (be[_blk(b, nb)], 0, 0)),
                  pl.BlockSpec((None, 1, tn), lambda j, b, be, nb, *_: (be[_blk(b, nb)], 0, nj))],
        out_specs=[pl.BlockSpec((tm, 2 * dh), lambda j, b, *_: (b, 0)),
                   pl.BlockSpec((tm, tn), lambda j, b, *_: (b, 0))],
        scratch_shapes=[pltpu.VMEM((2, DISPATCH_STRIPES, tm // DISPATCH_STRIPES, dh), U32)] + weight_scratch
                       + [pltpu.SemaphoreType.DMA((2,)), pltpu.SemaphoreType.DMA((2,))],
    )
    xs, h_first = pl.pallas_call(
        functools.partial(_moe1_first_kernel, tn=tn, n_col_tiles=nj, tm_e=tm),
        grid_spec=first_spec,
        out_shape=[jax.ShapeDtypeStruct((n_rows, 2 * dh), BF16), jax.ShapeDtypeStruct((n_rows, tn), BF16)],
        compiler_params=_params("arbitrary", "arbitrary"),
        name="moe_up_gather",
    )(*sched, tok_buf, tok_buf, xp, w_gu, b_gu, b_gu)

    rest_spec = pltpu.PrefetchScalarGridSpec(
        num_scalar_prefetch=5,
        grid=(nj - 1, n_rows // tm),
        in_specs=[pl.BlockSpec((tm, 2 * dh), lambda j, b, be, nb, *_: (_blk(b, nb), 0)),
                  pl.BlockSpec(memory_space=pl.ANY),
                  pl.BlockSpec((None, 1, tn), lambda j, b, be, nb, *_: (be[_blk(b, nb)], 0, 1 + j)),
                  pl.BlockSpec((None, 1, tn), lambda j, b, be, nb, *_: (be[_blk(b, nb)], 0, nj + 1 + j))],
        out_specs=pl.BlockSpec((tm, tn), lambda j, b, *_: (b, j)),
        scratch_shapes=weight_scratch + [pltpu.SemaphoreType.DMA((2,))],
    )
    h_rest = pl.pallas_call(
        functools.partial(_moe1_rest_kernel, tn=tn, n_col_tiles=nj),
        grid_spec=rest_spec,
        out_shape=jax.ShapeDtypeStruct((n_rows, de - tn), BF16),
        compiler_params=_params("arbitrary", "arbitrary"),
        name="moe_up",
    )(*sched, xs, w_gu, b_gu, b_gu)
    return h_first, h_rest


def _moe2_kernel(be_ref, nb_ref, nxt_ref, wrap_ref, bv_ref, ha_ref, hb_ref, w_hbm, b_ref, y_ref, stage, w_s, sems, *,
                 tn):
    used, fresh = _expert_changed(be_ref, nb_ref)
    ka = ha_ref.shape[1]

    def tile_copies(e, jt):
        return (pltpu.make_async_copy(w_hbm.at[e, :, pl.ds(pl.multiple_of(jt * tn, tn), tn)], stage, sems.at[0]),)

    _stream_expert_weights(be_ref, nxt_ref, wrap_ref, used, fresh, tile_copies, (stage,), (w_s,), 0)

    @pl.when(jnp.logical_not(used))
    def _():
        y_ref[...] = jnp.zeros_like(y_ref)

    def compute(rows):
        y_ref[:rows, :] = (_dot(ha_ref[:rows, :], w_s[:ka, :]) + _dot(hb_ref[:rows, :], w_s[ka:, :]) + b_ref[...])

    _for_valid_rows(used, bv_ref, y_ref, compute)


def _moe2(sched, h_first, h_rest, w_down, b_down, tm, tn):
    n_rows, ka = h_first.shape
    kb = h_rest.shape[1]
    n_exp, de, d = w_down.shape
    tn = _pick(d, tn)
    grid_spec = pltpu.PrefetchScalarGridSpec(
        num_scalar_prefetch=5,
        grid=(d // tn, n_rows // tm),
        in_specs=[pl.BlockSpec((tm, ka), lambda j, b, be, nb, *_: (_blk(b, nb), 0)),
                  pl.BlockSpec((tm, kb), lambda j, b, be, nb, *_: (_blk(b, nb), 0)),
                  pl.BlockSpec(memory_space=pl.ANY),
                  pl.BlockSpec((None, 1, tn), lambda j, b, be, nb, *_: (be[_blk(b, nb)], 0, j))],
        out_specs=pl.BlockSpec((tm, tn), lambda j, b, *_: (b, j)),
        scratch_shapes=[pltpu.VMEM((de, tn), F32), pltpu.VMEM((de, tn), BF16), pltpu.SemaphoreType.DMA((1,))],
    )
    return pl.pallas_call(
        functools.partial(_moe2_kernel, tn=tn),
        grid_spec=grid_spec,
        out_shape=jax.ShapeDtypeStruct((n_rows, d), F32),
        compiler_params=_params("arbitrary", "arbitrary"),
        name="moe_down",
    )(*sched, h_first, h_rest, w_down, b_down)


def _combine_kernel(dest_ref, dest_nxt_ref, ys_hbm, x1_ref, wgt_ref, g_ref, b_ref, o_ref, buf, sems, *, tc, top_k,
                    alpha):
    i = pl.program_id(0)
    slot = i % 2

    def row_copy(idx_ref, s, t, j):
        return pltpu.make_async_copy(ys_hbm.at[pl.ds(idx_ref[t * top_k + j], 1)], buf.at[s, j, pl.ds(t, 1)],
                                     sems.at[s])

    def issue(idx_ref, s):
        def body(t, c):
            for j in range(top_k):
                row_copy(idx_ref, s, t, j).start()
            return c
        lax.fori_loop(0, tc, body, 0, unroll=2)

    @pl.when(i == 0)
    def _():
        issue(dest_ref, slot)

    @pl.when(i + 1 < pl.num_programs(0))
    def _():
        issue(dest_nxt_ref, 1 - slot)

    def drain(t, c):
        for j in range(top_k):
            row_copy(dest_ref, slot, t, j).wait()
        return c

    lax.fori_loop(0, tc, drain, 0, unroll=2)
    y = wgt_ref[:, 0:1] * buf[slot, 0]
    for j in range(1, top_k):
        y = y + wgt_ref[:, j:j + 1] * buf[slot, j]
    o_ref[...] = _layer_norm(alpha * x1_ref[...] + y, g_ref[...], b_ref[...])


def _combine(dest_flat, ys, x1, wgt, g, b, alpha, tc):
    t, d = x1.shape
    tc = _pick(t, tc)
    last = t // tc - 1
    return pl.pallas_call(
        functools.partial(_combine_kernel, tc=tc, top_k=TOP_K, alpha=alpha),
        grid=(t // tc,),
        in_specs=[pl.BlockSpec((tc * TOP_K,), lambda i: (i,), memory_space=pltpu.SMEM),
                  pl.BlockSpec((tc * TOP_K,), lambda i: (jnp.minimum(i + 1, last),), memory_space=pltpu.SMEM),
                  pl.BlockSpec(memory_space=pl.ANY),
                  pl.BlockSpec((tc, d), lambda i: (i, 0)),
                  pl.BlockSpec((tc, LANES), lambda i: (i, 0)),
                  pl.BlockSpec((1, d), lambda i: (0, 0)),
                  pl.BlockSpec((1, d), lambda i: (0, 0))],
        out_specs=pl.BlockSpec((tc, d), lambda i: (i, 0)),
        out_shape=jax.ShapeDtypeStruct((t, d), F32),
        scratch_shapes=[pltpu.VMEM((2, TOP_K, tc, d), F32), pltpu.SemaphoreType.DMA((2,))],
        compiler_params=_params("arbitrary"),
        name="combine",
    )(dest_flat, dest_flat, ys, x1, wgt, g, b)


def _layer(x2d, w_in, b_in, a_ws, a_bs, a_ln_g, a_ln_b, gla_w_lr, gla_b_lr, gla_gn_g, w_br_a, w_br_b, w_o,
           ln1_g, ln1_b, w_router, b_router, w_gu, b_gu, w_down, b_down, ln2_g, ln2_b, alpha):
    t, d = x2d.shape
    n_groups, a_chunk, _ = a_ws.shape
    aw = a_ln_g.size
    rank, kw = gla_w_lr.shape
    heads, dv = gla_gn_g.shape
    vw = heads * dv
    n_exp = w_router.shape[1]
    de = w_down.shape[1]
    off_lr = 2 * aw + 2 * kw + 2 * vw
    off_q = 2 * aw

    w_in_t = w_in.T
    w_lr_in = jnp.pad(w_in[:, off_lr:off_lr + rank], ((0, 0), (0, LANES - rank))).astype(BF16)
    b_lr_in = jnp.pad(b_in[off_lr:off_lr + rank], (0, LANES - rank)).reshape(1, LANES)
    wlr = jnp.pad(gla_w_lr, ((0, LANES - rank), (0, 0))).astype(BF16)
    wr = jnp.pad(w_router, ((0, 0), (0, LANES - n_exp))).astype(BF16)
    br = jnp.pad(b_router, (0, LANES - n_exp), constant_values=-jnp.inf).reshape(1, LANES)

    xb, lr = _cast_and_decay_proj(x2d, w_lr_in, b_lr_in, 512)
    z = _matmul_bias_stream(xb, w_in_t, b_in[:off_lr].reshape(1, -1), 0, F32, 1024, 1024)
    zg = _matmul_bias_stream(xb, w_in_t, b_in[off_lr + rank:].reshape(1, -1), off_lr + rank, F32, 1024, 1024)

    ya = _branch_a(z, a_ws, a_bs, a_ln_g, a_ln_b, aw, 2 * a_chunk)
    yb = _gla(z, lr, wlr, gla_b_lr.reshape(1, kw), gla_gn_g.reshape(1, vw), kw, vw, off_q, heads, 256)
    merged = _merge(ya, w_br_a.astype(BF16), yb, w_br_b.astype(BF16), zg, 0, d, 1024, 512)
    h1 = _out_proj_resid(merged, w_o.astype(BF16), x2d, alpha, 1024, 512)

    x1, xp, idx, wgt, rnk, cnt = _ln_router(h1, ln1_g.reshape(1, d), ln1_b.reshape(1, d), wr, br, 256)

    tm_e = 512 if t * TOP_K >= 512 * n_exp else 64
    n_assign = t * TOP_K
    n_blocks = -(-(n_assign + n_exp * (tm_e - 1)) // tm_e)
    n_rows = n_blocks * tm_e
    counts = cnt[0, :n_exp].astype(I32)
    padded = (counts + tm_e - 1) // tm_e * tm_e
    pend = jnp.cumsum(padded)
    pstart = pend - padded
    sel = idx[:, :TOP_K, None] == jnp.arange(n_exp, dtype=I32)[None, None, :]
    dest = jnp.sum(jnp.where(sel, pstart[None, None, :], 0), axis=-1) + rnk[:, :TOP_K]
    dest_flat = dest.reshape(-1).astype(I32)
    nb_used = (pend[-1] // tm_e).astype(I32).reshape(1)
    block_start = jnp.arange(n_blocks, dtype=I32) * tm_e
    block_e = jnp.minimum(jnp.sum((pend[None, :] <= block_start[:, None]).astype(I32), axis=1), n_exp - 1)

    tok_buf = jnp.zeros((n_rows,), I32).at[dest_flat].set(jnp.arange(n_assign, dtype=I32) // TOP_K,
                                                          unique_indices=True)
    experts = jnp.arange(n_exp, dtype=I32)
    later = jnp.where((padded > 0)[None, :] & (experts[None, :] > block_e[:, None]), experts[None, :], n_exp)
    nxt = jnp.min(later, axis=1)
    wrap = (nxt == n_exp).astype(I32)
    nxt_e = jnp.where(nxt == n_exp, block_e[0], nxt).astype(I32)
    mine = experts[None, :] == block_e[:, None]
    group_end = jnp.sum(jnp.where(mine, (pstart + counts)[None, :], 0), axis=1)
    bvalid = jnp.clip(group_end - block_start, 0, tm_e).astype(I32)
    sched = (block_e, nb_used, nxt_e, wrap, bvalid)
    h_first, h_rest = _moe1(sched, tok_buf, xp, w_gu, b_gu.reshape(n_exp, 1, 2 * de), tm_e, 512)
    ys = _moe2(sched, h_first, h_rest, w_down, b_down.reshape(n_exp, 1, d), tm_e, 2048)
    return _combine(dest_flat, ys, x1, wgt, ln2_g.reshape(1, d), ln2_b.reshape(1, d), alpha, 128)


def kernel(x, w_in, b_in, a_ws, a_bs, a_ln_g, a_ln_b, gla_w_lr, gla_b_lr, gla_gn_g, w_br_a, w_br_b, w_o, ln1_g, ln1_b, w_router, b_router, w_gu, b_gu, w_down, b_down, ln2_g, ln2_b):
    bsz, seq, d = x.shape
    depth = w_in.shape[0]
    alpha = (2 * depth) ** 0.25
    outs = []
    for bi in range(bsz):
        h = x[bi]
        for l in range(depth):
            h = _layer(h, w_in[l], b_in[l], a_ws[l], a_bs[l], a_ln_g[l], a_ln_b[l], gla_w_lr[l], gla_b_lr[l],
                       gla_gn_g[l], w_br_a[l], w_br_b[l], w_o[l], ln1_g[l], ln1_b[l], w_router[l], b_router[l],
                       w_gu[l], b_gu[l], w_down[l], b_down[l], ln2_g[l], ln2_b[l], alpha)
        outs.append(h)
    return jnp.stack(outs) if bsz > 1 else outs[0][None]
```

```python
import functools

import jax
import jax.numpy as jnp
from jax import lax
from jax.experimental import pallas as pl
from jax.experimental.pallas import tpu as pltpu

F32 = jnp.float32
BF16 = jnp.bfloat16
U32 = jnp.uint32
I32 = jnp.int32

LN_EPS = 1e-5
GLA_CHUNK = 64
GLA_GATE_TAU = 16.0
TOP_K = 4
SWIGLU_LIMIT = 7.0
SWIGLU_ALPHA = 1.702
LANES = 128
VMEM_LIMIT = 56 * 1024 * 1024
HI_MASK = 0xFFFF0000
DISPATCH_STRIPES = 8

TILE_IN_PROJ = (1024, 1024)
TILE_CAST_ROWS = 512
TILE_BRANCH_A_CHUNKS = 2
TILE_GLA_ROWS = 256
TILE_MERGE = (1024, 512)
TILE_OUT_PROJ = (1024, 512)
TILE_ROUTER_ROWS = 256
EXPERT_BLOCK_ROWS = 512
TILE_EXPERT_UP_COLS = 768
TILE_EXPERT_DOWN_COLS = 2048
TILE_COMBINE_ROWS = 128


def _pick(n, pref):
    t = min(pref, n)
    while n % t:
        t //= 2
    return t


def _dot(a, b):
    return jnp.dot(a, b, preferred_element_type=F32)


def _dot_t0(a, b):
    return lax.dot_general(a, b, (((0,), (0,)), ((), ())), preferred_element_type=F32)


def _dot_t1(a, b):
    return lax.dot_general(a, b, (((1,), (1,)), ((), ())), preferred_element_type=F32)


def _sigmoid(x):
    return 1.0 / (1.0 + jnp.exp(-x))


def _gelu(x):
    return 0.5 * x * (1.0 + lax.erf(x * (2.0 ** -0.5)))


def _layer_norm(x, g, b):
    mu = jnp.mean(x, axis=-1, keepdims=True)
    xc = x - mu
    var = jnp.mean(xc * xc, axis=-1, keepdims=True)
    return xc * lax.rsqrt(var + LN_EPS) * g + b


def _params(*sem):
    return pltpu.CompilerParams(dimension_semantics=sem, vmem_limit_bytes=VMEM_LIMIT)


def _cast_lr_kernel(x_ref, w_ref, b_ref, xb_ref, lr_ref):
    xb = x_ref[...].astype(BF16)
    xb_ref[...] = xb
    lr_ref[...] = _dot(xb, w_ref[...]) + b_ref[...]


def _cast_and_decay_proj(x, w, b, tm):
    m, k = x.shape
    n = w.shape[1]
    tm = _pick(m, tm)
    return pl.pallas_call(
        _cast_lr_kernel,
        grid=(m // tm,),
        in_specs=[pl.BlockSpec((tm, k), lambda i: (i, 0)),
                  pl.BlockSpec((k, n), lambda i: (0, 0)),
                  pl.BlockSpec((1, n), lambda i: (0, 0))],
        out_specs=[pl.BlockSpec((tm, k), lambda i: (i, 0)),
                   pl.BlockSpec((tm, n), lambda i: (i, 0))],
        out_shape=[jax.ShapeDtypeStruct((m, k), BF16), jax.ShapeDtypeStruct((m, n), F32)],
        compiler_params=_params("parallel"),
        name="cast_lr",
    )(x, w, b)


def _mm_stream_kernel(x_ref, wt_hbm, b_ref, o_ref, stage, work, sem, *, tn, row0):
    j, i = pl.program_id(0), pl.program_id(1)

    def w_copy(jt):
        return pltpu.make_async_copy(wt_hbm.at[pl.ds(pl.multiple_of(row0 + jt * tn, 16), tn), :], stage, sem.at[0])

    @pl.when(jnp.logical_and(j == 0, i == 0))
    def _():
        w_copy(0).start()

    @pl.when(i == 0)
    def _():
        w_copy(j).wait()
        work[...] = stage[...].astype(BF16)

        @pl.when(j + 1 < pl.num_programs(0))
        def _():
            w_copy(j + 1).start()

    o_ref[...] = (_dot_t1(x_ref[...], work[...]) + b_ref[...]).astype(o_ref.dtype)


def _matmul_bias_stream(xb, wt, b, row0, out_dtype, tm, tn):
    m, k = xb.shape
    n = b.shape[1]
    tm, tn = _pick(m, tm), _pick(n, tn)
    assert row0 % 16 == 0 and tn % 16 == 0
    return pl.pallas_call(
        functools.partial(_mm_stream_kernel, tn=tn, row0=row0),
        grid=(n // tn, m // tm),
        in_specs=[pl.BlockSpec((tm, k), lambda j, i: (i, 0)),
                  pl.BlockSpec(memory_space=pl.ANY),
                  pl.BlockSpec((1, tn), lambda j, i: (0, j))],
        out_specs=pl.BlockSpec((tm, tn), lambda j, i: (i, j)),
        out_shape=jax.ShapeDtypeStruct((m, n), out_dtype),
        scratch_shapes=[pltpu.VMEM((tn, k), F32), pltpu.VMEM((tn, k), BF16), pltpu.SemaphoreType.DMA((1,))],
        compiler_params=_params("arbitrary", "arbitrary"),
        name="in_proj",
    )(xb, wt, b)


def _merge_kernel(ya_ref, wa_ref, yb_ref, wb_ref, ga_ref, gb_ref, o_ref):
    a = _sigmoid(ga_ref[...]) * _dot(ya_ref[...], wa_ref[...])
    o_ref[...] = (a + _sigmoid(gb_ref[...]) * _dot(yb_ref[...], wb_ref[...])).astype(o_ref.dtype)


def _merge(ya, wa, yb, wb, z, off_ga, off_gb, tm, tn):
    m, ka = ya.shape
    kb = yb.shape[1]
    n = wa.shape[1]
    tm, tn = _pick(m, tm), _pick(n, tn)
    while off_ga % tn or off_gb % tn:
        tn //= 2
    oa, ob = off_ga // tn, off_gb // tn
    return pl.pallas_call(
        _merge_kernel,
        grid=(m // tm, n // tn),
        in_specs=[pl.BlockSpec((tm, ka), lambda i, j: (i, 0)),
                  pl.BlockSpec((ka, tn), lambda i, j: (0, j)),
                  pl.BlockSpec((tm, kb), lambda i, j: (i, 0)),
                  pl.BlockSpec((kb, tn), lambda i, j: (0, j)),
                  pl.BlockSpec((tm, tn), lambda i, j: (i, oa + j)),
                  pl.BlockSpec((tm, tn), lambda i, j: (i, ob + j))],
        out_specs=pl.BlockSpec((tm, tn), lambda i, j: (i, j)),
        out_shape=jax.ShapeDtypeStruct((m, n), BF16),
        compiler_params=_params("parallel", "parallel"),
        name="merge",
    )(ya, wa, yb, wb, z, z)


def _mm_resid_kernel(m_ref, w_ref, x_ref, o_ref, *, alpha):
    o_ref[...] = alpha * x_ref[...] + _dot(m_ref[...], w_ref[...])


def _out_proj_resid(mb, w, x, alpha, tm, tn):
    m, k = mb.shape
    n = w.shape[1]
    tm, tn = _pick(m, tm), _pick(n, tn)
    return pl.pallas_call(
        functools.partial(_mm_resid_kernel, alpha=alpha),
        grid=(m // tm, n // tn),
        in_specs=[pl.BlockSpec((tm, k), lambda i, j: (i, 0)),
                  pl.BlockSpec((k, tn), lambda i, j: (0, j)),
                  pl.BlockSpec((tm, tn), lambda i, j: (i, j))],
        out_specs=pl.BlockSpec((tm, tn), lambda i, j: (i, j)),
        out_shape=jax.ShapeDtypeStruct((m, n), F32),
        compiler_params=_params("parallel", "parallel"),
        name="out_proj",
    )(mb, w, x)


def _branch_a_kernel(u_ref, v_ref, ws_ref, bs_ref, g_ref, b_ref, o_ref, *, n_groups, chunk):
    tm, aw = u_ref.shape
    gd = aw // n_groups
    row = lax.broadcasted_iota(I32, (chunk, chunk), 0)
    col = lax.broadcasted_iota(I32, (chunk, chunk), 1)
    causal = row >= col
    for g in range(n_groups):
        ws = jnp.where(causal, ws_ref[g], 0.0).astype(BF16)
        cs = slice(g * gd, (g + 1) * gd)
        for c in range(tm // chunk):
            rs = slice(c * chunk, (c + 1) * chunk)
            vn = _layer_norm(_gelu(v_ref[rs, cs]), g_ref[:, cs], b_ref[:, cs])
            mixed = _dot(ws, vn.astype(BF16)) + bs_ref[:, g:g + 1]
            o_ref[rs, cs] = (_gelu(u_ref[rs, cs]) * mixed).astype(o_ref.dtype)


def _branch_a(z, a_ws, a_bs, a_ln_g, a_ln_b, aw, tm):
    t = z.shape[0]
    n_groups, chunk, _ = a_ws.shape
    tm = max(_pick(t, tm), chunk)
    return pl.pallas_call(
        functools.partial(_branch_a_kernel, n_groups=n_groups, chunk=chunk),
        grid=(t // tm,),
        in_specs=[pl.BlockSpec((tm, aw), lambda i: (i, 0)),
                  pl.BlockSpec((tm, aw), lambda i: (i, 1)),
                  pl.BlockSpec((n_groups, chunk, chunk), lambda i: (0, 0, 0)),
                  pl.BlockSpec((chunk, n_groups), lambda i: (0, 0)),
                  pl.BlockSpec((1, aw), lambda i: (0, 0)),
                  pl.BlockSpec((1, aw), lambda i: (0, 0))],
        out_specs=pl.BlockSpec((tm, aw), lambda i: (i, 0)),
        out_shape=jax.ShapeDtypeStruct((t, aw), BF16),
        compiler_params=_params("parallel"),
        name="branch_a",
    )(z, z, a_ws, a_bs.T, a_ln_g.reshape(1, aw), a_ln_b.reshape(1, aw))


def _gla_kernel(q_ref, k_ref, v_ref, r_ref, lr_ref, wlr_ref, blr_ref, gn_ref, o_ref, s_ref, *, heads, chunk):
    @pl.when(pl.program_id(0) == 0)
    def _():
        s_ref[...] = jnp.zeros_like(s_ref)

    tg, kw = q_ref.shape
    vw = v_ref.shape[1]
    dk, dv = kw // heads, vw // heads
    row = lax.broadcasted_iota(I32, (chunk, chunk), 0)
    col = lax.broadcasted_iota(I32, (chunk, chunk), 1)
    causal = row >= col
    tri = causal.astype(BF16)
    ones = jnp.ones((chunk, LANES), BF16)
    scale = dk ** -0.5

    def body(c, carry):
        rs = pl.ds(pl.multiple_of(c * chunk, chunk), chunk)
        gl = _dot(lr_ref[rs, :].astype(BF16), wlr_ref[...]) + blr_ref[...]
        la = (jnp.minimum(gl, 0.0) - jnp.log1p(jnp.exp(-jnp.abs(gl)))) * (1.0 / GLA_GATE_TAU)
        la_hi = la.astype(BF16)
        la_lo = (la - la_hi.astype(F32)).astype(BF16)
        cum = _dot(tri, la_hi) + _dot(tri, la_lo)
        tot = cum[chunk - 1:chunk, :]
        tot_col = _dot_t0(la_hi, ones) + _dot_t0(la_lo, ones)
        dec_col = jnp.exp(tot_col)
        q = q_ref[rs, :] * scale
        k = k_ref[rs, :]
        qt = (q * jnp.exp(cum)).astype(BF16)
        kt = (k * jnp.exp(-cum)).astype(BF16)
        kl = (k * jnp.exp(tot - cum)).astype(BF16)
        for h in range(heads):
            ks = slice(h * dk, (h + 1) * dk)
            vs = slice(h * dv, (h + 1) * dv)
            vh = v_ref[rs, vs].astype(BF16)
            sc = jnp.where(causal, _dot_t1(qt[:, ks], kt[:, ks]), 0.0).astype(BF16)
            state = s_ref[h]
            o = _dot(sc, vh) + _dot(qt[:, ks], state.astype(BF16))
            dec = jnp.concatenate([dec_col[ks, :]] * (dv // LANES), axis=1) if dv >= LANES else dec_col[ks, :dv]
            s_ref[h] = dec * state + _dot_t0(kl[:, ks], vh)
            on = o * lax.rsqrt(jnp.mean(o * o, axis=-1, keepdims=True) + LN_EPS) * gn_ref[:, vs]
            rr = r_ref[rs, vs]
            o_ref[rs, vs] = (on * (rr * _sigmoid(rr))).astype(o_ref.dtype)
        return carry

    lax.fori_loop(0, tg // chunk, body, 0)


def _gla(z, lr, wlr, blr, gn, kw, vw, off_q, heads, tg):
    t = z.shape[0]
    tg = max(_pick(t, tg), GLA_CHUNK)
    oq = off_q // kw
    ok = oq + 1
    ov = (off_q + 2 * kw) // vw
    orr = ov + 1
    return pl.pallas_call(
        functools.partial(_gla_kernel, heads=heads, chunk=GLA_CHUNK),
        grid=(t // tg,),
        in_specs=[pl.BlockSpec((tg, kw), lambda i: (i, oq)),
                  pl.BlockSpec((tg, kw), lambda i: (i, ok)),
                  pl.BlockSpec((tg, vw), lambda i: (i, ov)),
                  pl.BlockSpec((tg, vw), lambda i: (i, orr)),
                  pl.BlockSpec((tg, LANES), lambda i: (i, 0)),
                  pl.BlockSpec((LANES, kw), lambda i: (0, 0)),
                  pl.BlockSpec((1, kw), lambda i: (0, 0)),
                  pl.BlockSpec((1, vw), lambda i: (0, 0))],
        out_specs=pl.BlockSpec((tg, vw), lambda i: (i, 0)),
        out_shape=jax.ShapeDtypeStruct((t, vw), BF16),
        scratch_shapes=[pltpu.VMEM((heads, kw // heads, vw // heads), F32)],
        compiler_params=_params("arbitrary"),
        name="gla",
    )(z, z, z, z, lr, wlr, blr, gn)


def _ln_router_kernel(h_ref, g_ref, b_ref, wr_ref, br_ref, x1_ref, xp_ref, idx_ref, wgt_ref, rank_ref, cnt_ref,
                      base_ref, *, top_k):
    @pl.when(pl.program_id(0) == 0)
    def _():
        base_ref[...] = jnp.zeros_like(base_ref)

    tm, d = h_ref.shape
    half = d // 2
    x1 = _layer_norm(h_ref[...], g_ref[...], b_ref[...])
    x1_ref[...] = x1
    lo = pltpu.bitcast(x1[:, :half].astype(BF16).astype(F32), U32) >> 16
    hi = pltpu.bitcast(x1[:, half:].astype(BF16).astype(F32), U32) & jnp.uint32(HI_MASK)
    xp_ref[...] = lo | hi

    logits = _dot(x1.astype(BF16), wr_ref[...]) + br_ref[...]
    lane = lax.broadcasted_iota(I32, (tm, LANES), 1).astype(F32)
    member = jnp.zeros((tm, LANES), F32)
    vals, idxs = [], []
    cur = logits
    for _ in range(top_k):
        mx = jnp.max(cur, axis=-1, keepdims=True)
        ix = jnp.min(jnp.where(cur == mx, lane, float(LANES)), axis=-1, keepdims=True)
        sel = lane == ix
        vals.append(mx)
        idxs.append(ix)
        member = member + sel.astype(F32)
        cur = jnp.where(sel, -jnp.inf, cur)
    exps = [jnp.exp(v - vals[0]) for v in vals]
    denom = exps[0]
    for e in exps[1:]:
        denom = denom + e
    r_i = lax.broadcasted_iota(I32, (tm, tm), 0)
    c_i = lax.broadcasted_iota(I32, (tm, tm), 1)
    before = _dot((r_i > c_i).astype(BF16), member.astype(BF16)) + base_ref[...]
    idx_out = jnp.zeros((tm, LANES), F32)
    wgt_out = jnp.zeros((tm, LANES), F32)
    rank_out = jnp.zeros((tm, LANES), F32)
    for j in range(top_k):
        rk = jnp.sum(jnp.where(lane == idxs[j], before, 0.0), axis=-1, keepdims=True)
        at = lane == float(j)
        idx_out = jnp.where(at, idxs[j], idx_out)
        wgt_out = jnp.where(at, exps[j] / denom, wgt_out)
        rank_out = jnp.where(at, rk, rank_out)
    idx_ref[...] = idx_out.astype(I32)
    wgt_ref[...] = wgt_out
    rank_ref[...] = rank_out.astype(I32)
    base_ref[...] = base_ref[...] + jnp.sum(member, axis=0, keepdims=True)
    cnt_ref[...] = base_ref[...]


def _ln_router(h1, g, b, wr, br, tm):
    t, d = h1.shape
    tm = _pick(t, tm)
    row = lambda i: (i, 0)
    fixed = lambda i: (0, 0)
    return pl.pallas_call(
        functools.partial(_ln_router_kernel, top_k=TOP_K),
        grid=(t // tm,),
        in_specs=[pl.BlockSpec((tm, d), row),
                  pl.BlockSpec((1, d), fixed),
                  pl.BlockSpec((1, d), fixed),
                  pl.BlockSpec((d, LANES), fixed),
                  pl.BlockSpec((1, LANES), fixed)],
        out_specs=[pl.BlockSpec((tm, d), row),
                   pl.BlockSpec((tm, d // 2), row),
                   pl.BlockSpec((tm, LANES), row),
                   pl.BlockSpec((tm, LANES), row),
                   pl.BlockSpec((tm, LANES), row),
                   pl.BlockSpec((1, LANES), fixed)],
        out_shape=[jax.ShapeDtypeStruct((t, d), F32),
                   jax.ShapeDtypeStruct((t, d // 2), U32),
                   jax.ShapeDtypeStruct((t, LANES), I32),
                   jax.ShapeDtypeStruct((t, LANES), F32),
                   jax.ShapeDtypeStruct((t, LANES), I32),
                   jax.ShapeDtypeStruct((1, LANES), F32)],
        scratch_shapes=[pltpu.VMEM((1, LANES), F32)],
        compiler_params=_params("arbitrary"),
        name="ln_router",
    )(h1, g, b, wr, br)


def _dispatch_kernel(nb_ref, tok_ref, tok_nxt_ref, xp_hbm, xs_ref, buf, sems, *, tm_e):
    b = pl.program_id(0)
    slot = b % 2
    used = b < nb_ref[0]
    stripe = tm_e // DISPATCH_STRIPES

    def row_copy(idx_ref, s, u, k):
        return pltpu.make_async_copy(xp_hbm.at[pl.ds(idx_ref[u * stripe + k], 1)], buf.at[s, u, pl.ds(k, 1)],
                                     sems.at[s])

    def issue(idx_ref, s):
        def body(k, c):
            for u in range(DISPATCH_STRIPES):
                row_copy(idx_ref, s, u, k).start(priority=u % 2)
            return c
        lax.fori_loop(0, stripe, body, 0)

    @pl.when(b == 0)
    def _():
        issue(tok_ref, slot)

    @pl.when(b + 1 < nb_ref[0])
    def _():
        issue(tok_nxt_ref, 1 - slot)

    @pl.when(jnp.logical_not(used))
    def _():
        xs_ref[...] = jnp.zeros_like(xs_ref)

    @pl.when(used)
    def _():
        def drain(k, c):
            for u in range(DISPATCH_STRIPES):
                row_copy(tok_ref, slot, u, k).wait()
            return c

        lax.fori_loop(0, stripe, drain, 0)
        w = buf[slot].reshape(tm_e, buf.shape[-1])
        half = w.shape[1]
        xs_ref[:, :half] = pltpu.bitcast(w << 16, F32).astype(BF16)
        xs_ref[:, half:] = pltpu.bitcast(w & jnp.uint32(HI_MASK), F32).astype(BF16)


def _dispatch(tok_buf, nb_used, xp, tm_e):
    dh = xp.shape[1]
    n_rows = tok_buf.shape[0]
    last = n_rows // tm_e - 1
    grid_spec = pltpu.PrefetchScalarGridSpec(
        num_scalar_prefetch=1,
        grid=(n_rows // tm_e,),
        in_specs=[pl.BlockSpec((tm_e,), lambda b, nb: (b,), memory_space=pltpu.SMEM),
                  pl.BlockSpec((tm_e,), lambda b, nb: (jnp.minimum(b + 1, last),), memory_space=pltpu.SMEM),
                  pl.BlockSpec(memory_space=pl.ANY)],
        out_specs=pl.BlockSpec((tm_e, 2 * dh), lambda b, nb: (b, 0)),
        scratch_shapes=[pltpu.VMEM((2, DISPATCH_STRIPES, tm_e // DISPATCH_STRIPES, dh), U32),
                        pltpu.SemaphoreType.DMA((2,))],
    )
    return pl.pallas_call(
        functools.partial(_dispatch_kernel, tm_e=tm_e),
        grid_spec=grid_spec,
        out_shape=jax.ShapeDtypeStruct((n_rows, 2 * dh), BF16),
        compiler_params=_params("arbitrary"),
        name="dispatch",
    )(nb_used, tok_buf, tok_buf, xp)


def _expert_changed(be_ref, nb_ref):
    b = pl.program_id(1)
    fresh = jnp.logical_or(b == 0, be_ref[b] != be_ref[jnp.maximum(b - 1, 0)])
    return b < nb_ref[0], fresh


def _stream_expert_weights(be_ref, nxt_ref, wrap_ref, used, fresh, tile_copies, stage, work):
    j, b = pl.program_id(0), pl.program_id(1)
    nj = pl.num_programs(0)

    @pl.when(jnp.logical_and(j == 0, b == 0))
    def _():
        for c in tile_copies(be_ref[0], 0):
            c.start()

    @pl.when(jnp.logical_and(used, fresh))
    def _():
        for c in tile_copies(be_ref[b], j):
            c.wait()
        for src, dst in zip(stage, work):
            dst[...] = src[...].astype(BF16)
        jn = j + wrap_ref[b]

        @pl.when(jn < nj)
        def _():
            for c in tile_copies(nxt_ref[b], jn):
                c.start()


def _for_valid_rows(used, bv_ref, out_ref, compute):
    tm = out_ref.shape[0]
    half = tm // 2
    few = bv_ref[pl.program_id(1)] <= half

    @pl.when(jnp.logical_and(used, jnp.logical_not(few)))
    def _():
        compute(tm)

    @pl.when(jnp.logical_and(used, few))
    def _():
        compute(half)
        out_ref[half:, :] = jnp.zeros((tm - half, out_ref.shape[1]), out_ref.dtype)


def _moe1_kernel(be_ref, nb_ref, nxt_ref, wrap_ref, bv_ref, xs_ref, w_hbm, bg_ref, bl_ref, h_ref, sg, sl, wg_s, wl_s,
                 sems, *, tn):
    used, fresh = _expert_changed(be_ref, nb_ref)
    n_lin = pl.num_programs(0)

    def tile_copies(e, jt):
        return (pltpu.make_async_copy(w_hbm.at[e, :, pl.ds(pl.multiple_of(jt * tn, tn), tn)], sg, sems.at[0]),
                pltpu.make_async_copy(w_hbm.at[e, :, pl.ds(pl.multiple_of((n_lin + jt) * tn, tn), tn)], sl,
                                      sems.at[1]))

    _stream_expert_weights(be_ref, nxt_ref, wrap_ref, used, fresh, tile_copies, (sg, sl), (wg_s, wl_s))

    @pl.when(jnp.logical_not(used))
    def _():
        h_ref[...] = jnp.zeros_like(h_ref)

    def compute(rows):
        x = xs_ref[:rows, :]
        gate = jnp.minimum(_dot(x, wg_s[...]) + bg_ref[...], SWIGLU_LIMIT)
        lin = jnp.clip(_dot(x, wl_s[...]) + bl_ref[...], -SWIGLU_LIMIT, SWIGLU_LIMIT)
        h_ref[:rows, :] = (gate * _sigmoid(SWIGLU_ALPHA * gate) * (lin + 1.0)).astype(h_ref.dtype)

    _for_valid_rows(used, bv_ref, h_ref, compute)


def _moe1(sched, xs, w_gu, b_gu, tm, tn):
    n_rows, dh = xs.shape
    n_exp, d, de2 = w_gu.shape
    de = de2 // 2
    tn = _pick(de, tn)
    nj = de // tn

    def blk(b, nb):
        return jnp.minimum(b, nb[0] - 1)

    grid_spec = pltpu.PrefetchScalarGridSpec(
        num_scalar_prefetch=5,
        grid=(nj, n_rows // tm),
        in_specs=[pl.BlockSpec((tm, dh), lambda j, b, be, nb, *_: (blk(b, nb), 0)),
                  pl.BlockSpec(memory_space=pl.ANY),
                  pl.BlockSpec((None, 1, tn), lambda j, b, be, nb, *_: (be[blk(b, nb)], 0, j)),
                  pl.BlockSpec((None, 1, tn), lambda j, b, be, nb, *_: (be[blk(b, nb)], 0, nj + j))],
        out_specs=pl.BlockSpec((tm, tn), lambda j, b, *_: (b, j)),
        scratch_shapes=[pltpu.VMEM((d, tn), F32), pltpu.VMEM((d, tn), F32),
                        pltpu.VMEM((d, tn), BF16), pltpu.VMEM((d, tn), BF16),
                        pltpu.SemaphoreType.DMA((2,))],
    )
    return pl.pallas_call(
        functools.partial(_moe1_kernel, tn=tn),
        grid_spec=grid_spec,
        out_shape=jax.ShapeDtypeStruct((n_rows, de), BF16),
        compiler_params=_params("arbitrary", "arbitrary"),
        name="moe_up",
    )(*sched, xs, w_gu, b_gu, b_gu)


def _moe2_kernel(be_ref, nb_ref, nxt_ref, wrap_ref, bv_ref, h_ref, w_hbm, b_ref, y_ref, stage, w_s, sems, *, tn):
    used, fresh = _expert_changed(be_ref, nb_ref)

    def tile_copies(e, jt):
        return (pltpu.make_async_copy(w_hbm.at[e, :, pl.ds(pl.multiple_of(jt * tn, tn), tn)], stage, sems.at[0]),)

    _stream_expert_weights(be_ref, nxt_ref, wrap_ref, used, fresh, tile_copies, (stage,), (w_s,))

    @pl.when(jnp.logical_not(used))
    def _():
        y_ref[...] = jnp.zeros_like(y_ref)

    def compute(rows):
        y_ref[:rows, :] = _dot(h_ref[:rows, :], w_s[...]) + b_ref[...]

    _for_valid_rows(used, bv_ref, y_ref, compute)


def _moe2(sched, h, w_down, b_down, tm, tn):
    n_rows, de = h.shape
    n_exp, _, d = w_down.shape
    tn = _pick(d, tn)

    def blk(b, nb):
        return jnp.minimum(b, nb[0] - 1)

    grid_spec = pltpu.PrefetchScalarGridSpec(
        num_scalar_prefetch=5,
        grid=(d // tn, n_rows // tm),
        in_specs=[pl.BlockSpec((tm, de), lambda j, b, be, nb, *_: (blk(b, nb), 0)),
                  pl.BlockSpec(memory_space=pl.ANY),
                  pl.BlockSpec((None, 1, tn), lambda j, b, be, nb, *_: (be[blk(b, nb)], 0, j))],
        out_specs=pl.BlockSpec((tm, tn), lambda j, b, *_: (b, j)),
        scratch_shapes=[pltpu.VMEM((de, tn), F32), pltpu.VMEM((de, tn), BF16), pltpu.SemaphoreType.DMA((1,))],
    )
    return pl.pallas_call(
        functools.partial(_moe2_kernel, tn=tn),
        grid_spec=grid_spec,
        out_shape=jax.ShapeDtypeStruct((n_rows, d), F32),
        compiler_params=_params("arbitrary", "arbitrary"),
        name="moe_down",
    )(*sched, h, w_down, b_down)


def _combine_kernel(dest_ref, dest_nxt_ref, ys_hbm, x1_ref, wgt_ref, g_ref, b_ref, o_ref, buf, sems, *, tc, top_k,
                    alpha):
    i = pl.program_id(0)
    slot = i % 2

    def row_copy(idx_ref, s, t, j):
        return pltpu.make_async_copy(ys_hbm.at[pl.ds(idx_ref[t * top_k + j], 1)], buf.at[s, j, pl.ds(t, 1)],
                                     sems.at[s])

    def issue(idx_ref, s):
        def body(t, c):
            for j in range(top_k):
                row_copy(idx_ref, s, t, j).start(priority=j % 2)
            return c
        lax.fori_loop(0, tc, body, 0, unroll=2)

    @pl.when(i == 0)
    def _():
        issue(dest_ref, slot)

    @pl.when(i + 1 < pl.num_programs(0))
    def _():
        issue(dest_nxt_ref, 1 - slot)

    def drain(t, c):
        for j in range(top_k):
            row_copy(dest_ref, slot, t, j).wait()
        return c

    lax.fori_loop(0, tc, drain, 0, unroll=2)
    y = wgt_ref[:, 0:1] * buf[slot, 0]
    for j in range(1, top_k):
        y = y + wgt_ref[:, j:j + 1] * buf[slot, j]
    o_ref[...] = _layer_norm(alpha * x1_ref[...] + y, g_ref[...], b_ref[...])


def _combine(dest_flat, ys, x1, wgt, g, b, alpha, tc):
    t, d = x1.shape
    tc = _pick(t, tc)
    last = t // tc - 1
    return pl.pallas_call(
        functools.partial(_combine_kernel, tc=tc, top_k=TOP_K, alpha=alpha),
        grid=(t // tc,),
        in_specs=[pl.BlockSpec((tc * TOP_K,), lambda i: (i,), memory_space=pltpu.SMEM),
                  pl.BlockSpec((tc * TOP_K,), lambda i: (jnp.minimum(i + 1, last),), memory_space=pltpu.SMEM),
                  pl.BlockSpec(memory_space=pl.ANY),
                  pl.BlockSpec((tc, d), lambda i: (i, 0)),
                  pl.BlockSpec((tc, LANES), lambda i: (i, 0)),
                  pl.BlockSpec((1, d), lambda i: (0, 0)),
                  pl.BlockSpec((1, d), lambda i: (0, 0))],
        out_specs=pl.BlockSpec((tc, d), lambda i: (i, 0)),
        out_shape=jax.ShapeDtypeStruct((t, d), F32),
        scratch_shapes=[pltpu.VMEM((2, TOP_K, tc, d), F32), pltpu.SemaphoreType.DMA((2,))],
        compiler_params=_params("arbitrary"),
        name="combine",
    )(dest_flat, dest_flat, ys, x1, wgt, g, b)


def _layer(x2d, w_in, b_in, a_ws, a_bs, a_ln_g, a_ln_b, gla_w_lr, gla_b_lr, gla_gn_g, w_br_a, w_br_b, w_o,
           ln1_g, ln1_b, w_router, b_router, w_gu, b_gu, w_down, b_down, ln2_g, ln2_b, alpha):
    t, d = x2d.shape
    n_groups, a_chunk, _ = a_ws.shape
    aw = a_ln_g.size
    rank, kw = gla_w_lr.shape
    heads, dv = gla_gn_g.shape
    vw = heads * dv
    n_exp = w_router.shape[1]
    de = w_down.shape[1]
    off_lr = 2 * aw + 2 * kw + 2 * vw
    off_q = 2 * aw

    w_in_t = w_in.T
    w_lr_in = jnp.pad(w_in[:, off_lr:off_lr + rank], ((0, 0), (0, LANES - rank))).astype(BF16)
    b_lr_in = jnp.pad(b_in[off_lr:off_lr + rank], (0, LANES - rank)).reshape(1, LANES)
    wlr = jnp.pad(gla_w_lr, ((0, LANES - rank), (0, 0))).astype(BF16)
    wr = jnp.pad(w_router, ((0, 0), (0, LANES - n_exp))).astype(BF16)
    br = jnp.pad(b_router, (0, LANES - n_exp), constant_values=-jnp.inf).reshape(1, LANES)

    xb, lr = _cast_and_decay_proj(x2d, w_lr_in, b_lr_in, TILE_CAST_ROWS)
    z = _matmul_bias_stream(xb, w_in_t, b_in[:off_lr].reshape(1, -1), 0, F32, *TILE_IN_PROJ)
    zg = _matmul_bias_stream(xb, w_in_t, b_in[off_lr + rank:].reshape(1, -1), off_lr + rank, F32, *TILE_IN_PROJ)

    ya = _branch_a(z, a_ws, a_bs, a_ln_g, a_ln_b, aw, TILE_BRANCH_A_CHUNKS * a_chunk)
    yb = _gla(z, lr, wlr, gla_b_lr.reshape(1, kw), gla_gn_g.reshape(1, vw), kw, vw, off_q, heads,
              TILE_GLA_ROWS)
    merged = _merge(ya, w_br_a.astype(BF16), yb, w_br_b.astype(BF16), zg, 0, d, *TILE_MERGE)
    h1 = _out_proj_resid(merged, w_o.astype(BF16), x2d, alpha, *TILE_OUT_PROJ)

    x1, xp, idx, wgt, rnk, cnt = _ln_router(h1, ln1_g.reshape(1, d), ln1_b.reshape(1, d), wr, br, TILE_ROUTER_ROWS)

    n_assign = t * TOP_K
    tm_e = min(EXPERT_BLOCK_ROWS, max(n_assign // n_exp // 2, 8 * DISPATCH_STRIPES))
    n_blocks = -(-(n_assign + n_exp * (tm_e - 1)) // tm_e)
    n_rows = n_blocks * tm_e
    counts = cnt[0, :n_exp].astype(I32)
    padded = (counts + tm_e - 1) // tm_e * tm_e
    pend = jnp.cumsum(padded)
    pstart = pend - padded
    sel = idx[:, :TOP_K, None] == jnp.arange(n_exp, dtype=I32)[None, None, :]
    dest = jnp.sum(jnp.where(sel, pstart[None, None, :], 0), axis=-1) + rnk[:, :TOP_K]
    dest_flat = dest.reshape(-1).astype(I32)
    nb_used = (pend[-1] // tm_e).astype(I32).reshape(1)
    block_start = jnp.arange(n_blocks, dtype=I32) * tm_e
    block_e = jnp.minimum(jnp.sum((pend[None, :] <= block_start[:, None]).astype(I32), axis=1), n_exp - 1)

    tok_buf = jnp.zeros((n_rows,), I32).at[dest_flat].set(jnp.arange(n_assign, dtype=I32) // TOP_K,
                                                          unique_indices=True)
    xs = _dispatch(tok_buf, nb_used, xp, tm_e)
    experts = jnp.arange(n_exp, dtype=I32)
    later = jnp.where((padded > 0)[None, :] & (experts[None, :] > block_e[:, None]), experts[None, :], n_exp)
    nxt = jnp.min(later, axis=1)
    wrap = (nxt == n_exp).astype(I32)
    nxt_e = jnp.where(nxt == n_exp, block_e[0], nxt).astype(I32)
    mine = experts[None, :] == block_e[:, None]
    group_end = jnp.sum(jnp.where(mine, (pstart + counts)[None, :], 0), axis=1)
    bvalid = jnp.clip(group_end - block_start, 0, tm_e).astype(I32)
    sched = (block_e, nb_used, nxt_e, wrap, bvalid)
    hmid = _moe1(sched, xs, w_gu, b_gu.reshape(n_exp, 1, 2 * de), tm_e, TILE_EXPERT_UP_COLS)
    ys = _moe2(sched, hmid, w_down, b_down.reshape(n_exp, 1, d), tm_e, TILE_EXPERT_DOWN_COLS)
    return _combine(dest_flat, ys, x1, wgt, ln2_g.reshape(1, d), ln2_b.reshape(1, d), alpha, TILE_COMBINE_ROWS)


def kernel(x, w_in, b_in, a_ws, a_bs, a_ln_g, a_ln_b, gla_w_lr, gla_b_lr, gla_gn_g, w_br_a, w_br_b, w_o, ln1_g, ln1_b, w_router, b_router, w_gu, b_gu, w_down, b_down, ln2_g, ln2_b):
    bsz, seq, d = x.shape
    depth = w_in.shape[0]
    alpha = (2 * depth) ** 0.25
    outs = []
    for bi in range(bsz):
        h = x[bi]
        for l in range(depth):
            h = _layer(h, w_in[l], b_in[l], a_ws[l], a_bs[l], a_ln_g[l], a_ln_b[l], gla_w_lr[l], gla_b_lr[l],
                       gla_gn_g[l], w_br_a[l], w_br_b[l], w_o[l], ln1_g[l], ln1_b[l], w_router[l], b_router[l],
                       w_gu[l], b_gu[l], w_down[l], b_down[l], ln2_g[l], ln2_b[l], alpha)
        outs.append(h)
    return jnp.stack(outs) if bsz > 1 else outs[0][None]
```

```python
import functools

import jax
import jax.numpy as jnp
from jax import lax
from jax.experimental import pallas as pl
from jax.experimental.pallas import tpu as pltpu

F32 = jnp.float32
BF16 = jnp.bfloat16
U32 = jnp.uint32
I32 = jnp.int32

LN_EPS = 1e-5
GLA_CHUNK = 64
GLA_GATE_TAU = 16.0
TOP_K = 4
SWIGLU_LIMIT = 7.0
SWIGLU_ALPHA = 1.702
LANES = 128
VMEM_LIMIT = 56 * 1024 * 1024
HI_MASK = 0xFFFF0000
DISPATCH_STRIPES = 8

TILE_IN_PROJ = (1024, 1024)
TILE_CAST_ROWS = 512
TILE_BRANCH_A_CHUNKS = 2
TILE_GLA_ROWS = 256
TILE_MERGE = (1024, 512)
TILE_OUT_PROJ = (1024, 512)
TILE_ROUTER_ROWS = 256
EXPERT_BLOCK_ROWS = 512
TILE_EXPERT_UP_COLS = 768
TILE_EXPERT_DOWN_COLS = 2048
TILE_COMBINE_ROWS = 128


def _pick(n, pref):
    t = min(pref, n)
    while n % t:
        t //= 2
    return t


def _dot(a, b):
    return jnp.dot(a, b, preferred_element_type=F32)


def _dot_t0(a, b):
    return lax.dot_general(a, b, (((0,), (0,)), ((), ())), preferred_element_type=F32)


def _dot_t1(a, b):
    return lax.dot_general(a, b, (((1,), (1,)), ((), ())), preferred_element_type=F32)


def _sigmoid(x):
    return 1.0 / (1.0 + jnp.exp(-x))


def _gelu(x):
    return 0.5 * x * (1.0 + lax.erf(x * (2.0 ** -0.5)))


def _layer_norm(x, g, b):
    mu = jnp.mean(x, axis=-1, keepdims=True)
    xc = x - mu
    var = jnp.mean(xc * xc, axis=-1, keepdims=True)
    return xc * lax.rsqrt(var + LN_EPS) * g + b


def _params(*sem):
    return pltpu.CompilerParams(dimension_semantics=sem, vmem_limit_bytes=VMEM_LIMIT)


def _cast_lr_kernel(x_ref, w_ref, b_ref, xb_ref, lr_ref):
    xb = x_ref[...].astype(BF16)
    xb_ref[...] = xb
    lr_ref[...] = _dot(xb, w_ref[...]) + b_ref[...]


def _cast_and_decay_proj(x, w, b, tm):
    m, k = x.shape
    n = w.shape[1]
    tm = _pick(m, tm)
    return pl.pallas_call(
        _cast_lr_kernel,
        grid=(m // tm,),
        in_specs=[pl.BlockSpec((tm, k), lambda i: (i, 0)),
                  pl.BlockSpec((k, n), lambda i: (0, 0)),
                  pl.BlockSpec((1, n), lambda i: (0, 0))],
        out_specs=[pl.BlockSpec((tm, k), lambda i: (i, 0)),
                   pl.BlockSpec((tm, n), lambda i: (i, 0))],
        out_shape=[jax.ShapeDtypeStruct((m, k), BF16), jax.ShapeDtypeStruct((m, n), F32)],
        compiler_params=_params("parallel"),
        name="cast_lr",
    )(x, w, b)


def _mm_stream_kernel(x_ref, wt_hbm, b_ref, o_ref, stage, work, sem, *, tn, row0):
    j, i = pl.program_id(0), pl.program_id(1)

    def w_copy(jt):
        return pltpu.make_async_copy(wt_hbm.at[pl.ds(pl.multiple_of(row0 + jt * tn, 16), tn), :], stage, sem.at[0])

    @pl.when(jnp.logical_and(j == 0, i == 0))
    def _():
        w_copy(0).start()

    @pl.when(i == 0)
    def _():
        w_copy(j).wait()
        work[...] = stage[...].astype(BF16)

        @pl.when(j + 1 < pl.num_programs(0))
        def _():
            w_copy(j + 1).start()

    o_ref[...] = (_dot_t1(x_ref[...], work[...]) + b_ref[...]).astype(o_ref.dtype)


def _matmul_bias_stream(xb, wt, b, row0, out_dtype, tm, tn):
    m, k = xb.shape
    n = b.shape[1]
    tm, tn = _pick(m, tm), _pick(n, tn)
    assert row0 % 16 == 0 and tn % 16 == 0
    return pl.pallas_call(
        functools.partial(_mm_stream_kernel, tn=tn, row0=row0),
        grid=(n // tn, m // tm),
        in_specs=[pl.BlockSpec((tm, k), lambda j, i: (i, 0)),
                  pl.BlockSpec(memory_space=pl.ANY),
                  pl.BlockSpec((1, tn), lambda j, i: (0, j))],
        out_specs=pl.BlockSpec((tm, tn), lambda j, i: (i, j)),
        out_shape=jax.ShapeDtypeStruct((m, n), out_dtype),
        scratch_shapes=[pltpu.VMEM((tn, k), F32), pltpu.VMEM((tn, k), BF16), pltpu.SemaphoreType.DMA((1,))],
        compiler_params=_params("arbitrary", "arbitrary"),
        name="in_proj",
    )(xb, wt, b)


def _merge_kernel(ya_ref, wa_ref, yb_ref, wb_ref, ga_ref, gb_ref, o_ref):
    a = _sigmoid(ga_ref[...]) * _dot(ya_ref[...], wa_ref[...])
    o_ref[...] = (a + _sigmoid(gb_ref[...]) * _dot(yb_ref[...], wb_ref[...])).astype(o_ref.dtype)


def _merge(ya, wa, yb, wb, z, off_ga, off_gb, tm, tn):
    m, ka = ya.shape
    kb = yb.shape[1]
    n = wa.shape[1]
    tm, tn = _pick(m, tm), _pick(n, tn)
    while off_ga % tn or off_gb % tn:
        tn //= 2
    oa, ob = off_ga // tn, off_gb // tn
    return pl.pallas_call(
        _merge_kernel,
        grid=(m // tm, n // tn),
        in_specs=[pl.BlockSpec((tm, ka), lambda i, j: (i, 0)),
                  pl.BlockSpec((ka, tn), lambda i, j: (0, j)),
                  pl.BlockSpec((tm, kb), lambda i, j: (i, 0)),
                  pl.BlockSpec((kb, tn), lambda i, j: (0, j)),
                  pl.BlockSpec((tm, tn), lambda i, j: (i, oa + j)),
                  pl.BlockSpec((tm, tn), lambda i, j: (i, ob + j))],
        out_specs=pl.BlockSpec((tm, tn), lambda i, j: (i, j)),
        out_shape=jax.ShapeDtypeStruct((m, n), BF16),
        compiler_params=_params("parallel", "parallel"),
        name="merge",
    )(ya, wa, yb, wb, z, z)


def _mm_resid_kernel(m_ref, w_ref, x_ref, o_ref, *, alpha):
    o_ref[...] = alpha * x_ref[...] + _dot(m_ref[...], w_ref[...])


def _out_proj_resid(mb, w, x, alpha, tm, tn):
    m, k = mb.shape
    n = w.shape[1]
    tm, tn = _pick(m, tm), _pick(n, tn)
    return pl.pallas_call(
        functools.partial(_mm_resid_kernel, alpha=alpha),
        grid=(m // tm, n // tn),
        in_specs=[pl.BlockSpec((tm, k), lambda i, j: (i, 0)),
                  pl.BlockSpec((k, tn), lambda i, j: (0, j)),
                  pl.BlockSpec((tm, tn), lambda i, j: (i, j))],
        out_specs=pl.BlockSpec((tm, tn), lambda i, j: (i, j)),
        out_shape=jax.ShapeDtypeStruct((m, n), F32),
        compiler_params=_params("parallel", "parallel"),
        name="out_proj",
    )(mb, w, x)


def _branch_a_kernel(u_ref, v_ref, ws_ref, bs_ref, g_ref, b_ref, o_ref, *, n_groups, chunk):
    tm, aw = u_ref.shape
    gd = aw // n_groups
    row = lax.broadcasted_iota(I32, (chunk, chunk), 0)
    col = lax.broadcasted_iota(I32, (chunk, chunk), 1)
    causal = row >= col
    for g in range(n_groups):
        ws = jnp.where(causal, ws_ref[g], 0.0).astype(BF16)
        cs = slice(g * gd, (g + 1) * gd)
        for c in range(tm // chunk):
            rs = slice(c * chunk, (c + 1) * chunk)
            vn = _layer_norm(_gelu(v_ref[rs, cs]), g_ref[:, cs], b_ref[:, cs])
            mixed = _dot(ws, vn.astype(BF16)) + bs_ref[:, g:g + 1]
            o_ref[rs, cs] = (_gelu(u_ref[rs, cs]) * mixed).astype(o_ref.dtype)


def _branch_a(z, a_ws, a_bs, a_ln_g, a_ln_b, aw, tm):
    t = z.shape[0]
    n_groups, chunk, _ = a_ws.shape
    tm = max(_pick(t, tm), chunk)
    return pl.pallas_call(
        functools.partial(_branch_a_kernel, n_groups=n_groups, chunk=chunk),
        grid=(t // tm,),
        in_specs=[pl.BlockSpec((tm, aw), lambda i: (i, 0)),
                  pl.BlockSpec((tm, aw), lambda i: (i, 1)),
                  pl.BlockSpec((n_groups, chunk, chunk), lambda i: (0, 0, 0)),
                  pl.BlockSpec((chunk, n_groups), lambda i: (0, 0)),
                  pl.BlockSpec((1, aw), lambda i: (0, 0)),
                  pl.BlockSpec((1, aw), lambda i: (0, 0))],
        out_specs=pl.BlockSpec((tm, aw), lambda i: (i, 0)),
        out_shape=jax.ShapeDtypeStruct((t, aw), BF16),
        compiler_params=_params("parallel"),
        name="branch_a",
    )(z, z, a_ws, a_bs.T, a_ln_g.reshape(1, aw), a_ln_b.reshape(1, aw))


def _gla_kernel(q_ref, k_ref, v_ref, r_ref, lr_ref, wlr_ref, blr_ref, gn_ref, o_ref, s_ref, *, heads, chunk):
    @pl.when(pl.program_id(0) == 0)
    def _():
        s_ref[...] = jnp.zeros_like(s_ref)

    tg, kw = q_ref.shape
    vw = v_ref.shape[1]
    dk, dv = kw // heads, vw // heads
    row = lax.broadcasted_iota(I32, (chunk, chunk), 0)
    col = lax.broadcasted_iota(I32, (chunk, chunk), 1)
    causal = row >= col
    tri = causal.astype(BF16)
    ones = jnp.ones((chunk, LANES), BF16)
    scale = dk ** -0.5

    def body(c, carry):
        rs = pl.ds(pl.multiple_of(c * chunk, chunk), chunk)
        gl = _dot(lr_ref[rs, :].astype(BF16), wlr_ref[...]) + blr_ref[...]
        la = (jnp.minimum(gl, 0.0) - jnp.log1p(jnp.exp(-jnp.abs(gl)))) * (1.0 / GLA_GATE_TAU)
        la_hi = la.astype(BF16)
        la_lo = (la - la_hi.astype(F32)).astype(BF16)
        cum = _dot(tri, la_hi) + _dot(tri, la_lo)
        tot = cum[chunk - 1:chunk, :]
        tot_col = _dot_t0(la_hi, ones) + _dot_t0(la_lo, ones)
        dec_col = jnp.exp(tot_col)
        q = q_ref[rs, :] * scale
        k = k_ref[rs, :]
        qt = (q * jnp.exp(cum)).astype(BF16)
        kt = (k * jnp.exp(-cum)).astype(BF16)
        kl = (k * jnp.exp(tot - cum)).astype(BF16)
        for h in range(heads):
            ks = slice(h * dk, (h + 1) * dk)
            vs = slice(h * dv, (h + 1) * dv)
            vh = v_ref[rs, vs].astype(BF16)
            sc = jnp.where(causal, _dot_t1(qt[:, ks], kt[:, ks]), 0.0).astype(BF16)
            state = s_ref[h]
            o = _dot(sc, vh) + _dot(qt[:, ks], state.astype(BF16))
            dec = jnp.concatenate([dec_col[ks, :]] * (dv // LANES), axis=1) if dv >= LANES else dec_col[ks, :dv]
            s_ref[h] = dec * state + _dot_t0(kl[:, ks], vh)
            on = o * lax.rsqrt(jnp.mean(o * o, axis=-1, keepdims=True) + LN_EPS) * gn_ref[:, vs]
            rr = r_ref[rs, vs]
            o_ref[rs, vs] = (on * (rr * _sigmoid(rr))).astype(o_ref.dtype)
        return carry

    lax.fori_loop(0, tg // chunk, body, 0)


def _gla(z, lr, wlr, blr, gn, kw, vw, off_q, heads, tg):
    t = z.shape[0]
    tg = max(_pick(t, tg), GLA_CHUNK)
    oq = off_q // kw
    ok = oq + 1
    ov = (off_q + 2 * kw) // vw
    orr = ov + 1
    return pl.pallas_call(
        functools.partial(_gla_kernel, heads=heads, chunk=GLA_CHUNK),
        grid=(t // tg,),
        in_specs=[pl.BlockSpec((tg, kw), lambda i: (i, oq)),
                  pl.BlockSpec((tg, kw), lambda i: (i, ok)),
                  pl.BlockSpec((tg, vw), lambda i: (i, ov)),
                  pl.BlockSpec((tg, vw), lambda i: (i, orr)),
                  pl.BlockSpec((tg, LANES), lambda i: (i, 0)),
                  pl.BlockSpec((LANES, kw), lambda i: (0, 0)),
                  pl.BlockSpec((1, kw), lambda i: (0, 0)),
                  pl.BlockSpec((1, vw), lambda i: (0, 0))],
        out_specs=pl.BlockSpec((tg, vw), lambda i: (i, 0)),
        out_shape=jax.ShapeDtypeStruct((t, vw), BF16),
        scratch_shapes=[pltpu.VMEM((heads, kw // heads, vw // heads), F32)],
        compiler_params=_params("arbitrary"),
        name="gla",
    )(z, z, z, z, lr, wlr, blr, gn)


def _ln_router_kernel(h_ref, g_ref, b_ref, wr_ref, br_ref, x1_ref, xp_ref, idx_ref, wgt_ref, rank_ref, cnt_ref,
                      base_ref, *, top_k):
    @pl.when(pl.program_id(0) == 0)
    def _():
        base_ref[...] = jnp.zeros_like(base_ref)

    tm, d = h_ref.shape
    half = d // 2
    x1 = _layer_norm(h_ref[...], g_ref[...], b_ref[...])
    x1_ref[...] = x1
    lo = pltpu.bitcast(x1[:, :half].astype(BF16).astype(F32), U32) >> 16
    hi = pltpu.bitcast(x1[:, half:].astype(BF16).astype(F32), U32) & jnp.uint32(HI_MASK)
    xp_ref[...] = lo | hi

    logits = _dot(x1.astype(BF16), wr_ref[...]) + br_ref[...]
    lane = lax.broadcasted_iota(I32, (tm, LANES), 1).astype(F32)
    member = jnp.zeros((tm, LANES), F32)
    vals, idxs = [], []
    cur = logits
    for _ in range(top_k):
        mx = jnp.max(cur, axis=-1, keepdims=True)
        ix = jnp.min(jnp.where(cur == mx, lane, float(LANES)), axis=-1, keepdims=True)
        sel = lane == ix
        vals.append(mx)
        idxs.append(ix)
        member = member + sel.astype(F32)
        cur = jnp.where(sel, -jnp.inf, cur)
    exps = [jnp.exp(v - vals[0]) for v in vals]
    denom = exps[0]
    for e in exps[1:]:
        denom = denom + e
    r_i = lax.broadcasted_iota(I32, (tm, tm), 0)
    c_i = lax.broadcasted_iota(I32, (tm, tm), 1)
    before = _dot((r_i > c_i).astype(BF16), member.astype(BF16)) + base_ref[...]
    idx_out = jnp.zeros((tm, LANES), F32)
    wgt_out = jnp.zeros((tm, LANES), F32)
    rank_out = jnp.zeros((tm, LANES), F32)
    for j in range(top_k):
        rk = jnp.sum(jnp.where(lane == idxs[j], before, 0.0), axis=-1, keepdims=True)
        at = lane == float(j)
        idx_out = jnp.where(at, idxs[j], idx_out)
        wgt_out = jnp.where(at, exps[j] / denom, wgt_out)
        rank_out = jnp.where(at, rk, rank_out)
    idx_ref[...] = idx_out.astype(I32)
    wgt_ref[...] = wgt_out
    rank_ref[...] = rank_out.astype(I32)
    base_ref[...] = base_ref[...] + jnp.sum(member, axis=0, keepdims=True)
    cnt_ref[...] = base_ref[...]


def _ln_router(h1, g, b, wr, br, tm):
    t, d = h1.shape
    tm = _pick(t, tm)
    row = lambda i: (i, 0)
    fixed = lambda i: (0, 0)
    return pl.pallas_call(
        functools.partial(_ln_router_kernel, top_k=TOP_K),
        grid=(t // tm,),
        in_specs=[pl.BlockSpec((tm, d), row),
                  pl.BlockSpec((1, d), fixed),
                  pl.BlockSpec((1, d), fixed),
                  pl.BlockSpec((d, LANES), fixed),
                  pl.BlockSpec((1, LANES), fixed)],
        out_specs=[pl.BlockSpec((tm, d), row),
                   pl.BlockSpec((tm, d // 2), row),
                   pl.BlockSpec((tm, LANES), row),
                   pl.BlockSpec((tm, LANES), row),
                   pl.BlockSpec((tm, LANES), row),
                   pl.BlockSpec((1, LANES), fixed)],
        out_shape=[jax.ShapeDtypeStruct((t, d), F32),
                   jax.ShapeDtypeStruct((t, d // 2), U32),
                   jax.ShapeDtypeStruct((t, LANES), I32),
                   jax.ShapeDtypeStruct((t, LANES), F32),
                   jax.ShapeDtypeStruct((t, LANES), I32),
                   jax.ShapeDtypeStruct((1, LANES), F32)],
        scratch_shapes=[pltpu.VMEM((1, LANES), F32)],
        compiler_params=_params("arbitrary"),
        name="ln_router",
    )(h1, g, b, wr, br)


def _dispatch_kernel(nb_ref, tok_ref, tok_nxt_ref, xp_hbm, xs_ref, buf, sems, *, tm_e):
    b = pl.program_id(0)
    slot = b % 2
    used = b < nb_ref[0]
    stripe = tm_e // DISPATCH_STRIPES

    def row_copy(idx_ref, s, u, k):
        return pltpu.make_async_copy(xp_hbm.at[pl.ds(idx_ref[u * stripe + k], 1)], buf.at[s, u, pl.ds(k, 1)],
                                     sems.at[s, u])

    def issue(idx_ref, s):
        def body(k, c):
            for u in range(DISPATCH_STRIPES):
                row_copy(idx_ref, s, u, k).start(priority=u % 2)
            return c
        lax.fori_loop(0, stripe, body, 0)

    @pl.when(b == 0)
    def _():
        issue(tok_ref, slot)

    @pl.when(b + 1 < nb_ref[0])
    def _():
        issue(tok_nxt_ref, 1 - slot)

    @pl.when(jnp.logical_not(used))
    def _():
        xs_ref[...] = jnp.zeros_like(xs_ref)

    @pl.when(used)
    def _():
        def drain(k, c):
            for u in range(DISPATCH_STRIPES):
                row_copy(tok_ref, slot, u, k).wait()
            return c

        lax.fori_loop(0, stripe, drain, 0)
        w = buf[slot].reshape(tm_e, buf.shape[-1])
        half = w.shape[1]
        xs_ref[:, :half] = pltpu.bitcast(w << 16, F32).astype(BF16)
        xs_ref[:, half:] = pltpu.bitcast(w & jnp.uint32(HI_MASK), F32).astype(BF16)


def _dispatch(tok_buf, nb_used, xp, tm_e):
    dh = xp.shape[1]
    n_rows = tok_buf.shape[0]
    last = n_rows // tm_e - 1
    grid_spec = pltpu.PrefetchScalarGridSpec(
        num_scalar_prefetch=1,
        grid=(n_rows // tm_e,),
        in_specs=[pl.BlockSpec((tm_e,), lambda b, nb: (b,), memory_space=pltpu.SMEM),
                  pl.BlockSpec((tm_e,), lambda b, nb: (jnp.minimum(b + 1, last),), memory_space=pltpu.SMEM),
                  pl.BlockSpec(memory_space=pl.ANY)],
        out_specs=pl.BlockSpec((tm_e, 2 * dh), lambda b, nb: (b, 0)),
        scratch_shapes=[pltpu.VMEM((2, DISPATCH_STRIPES, tm_e // DISPATCH_STRIPES, dh), U32),
                        pltpu.SemaphoreType.DMA((2, DISPATCH_STRIPES))],
    )
    return pl.pallas_call(
        functools.partial(_dispatch_kernel, tm_e=tm_e),
        grid_spec=grid_spec,
        out_shape=jax.ShapeDtypeStruct((n_rows, 2 * dh), BF16),
        compiler_params=_params("arbitrary"),
        name="dispatch",
    )(nb_used, tok_buf, tok_buf, xp)


def _expert_changed(be_ref, nb_ref):
    b = pl.program_id(1)
    fresh = jnp.logical_or(b == 0, be_ref[b] != be_ref[jnp.maximum(b - 1, 0)])
    return b < nb_ref[0], fresh


def _stream_expert_weights(be_ref, nxt_ref, wrap_ref, used, fresh, tile_copies, stage, work):
    j, b = pl.program_id(0), pl.program_id(1)
    nj = pl.num_programs(0)

    @pl.when(jnp.logical_and(j == 0, b == 0))
    def _():
        for c in tile_copies(be_ref[0], 0):
            c.start()

    @pl.when(jnp.logical_and(used, fresh))
    def _():
        for c in tile_copies(be_ref[b], j):
            c.wait()
        for src, dst in zip(stage, work):
            dst[...] = src[...].astype(BF16)
        jn = j + wrap_ref[b]

        @pl.when(jn < nj)
        def _():
            for c in tile_copies(nxt_ref[b], jn):
                c.start()


def _for_valid_rows(used, bv_ref, out_ref, compute):
    tm = out_ref.shape[0]
    half = tm // 2
    few = bv_ref[pl.program_id(1)] <= half

    @pl.when(jnp.logical_and(used, jnp.logical_not(few)))
    def _():
        compute(tm)

    @pl.when(jnp.logical_and(used, few))
    def _():
        compute(half)
        out_ref[half:, :] = jnp.zeros((tm - half, out_ref.shape[1]), out_ref.dtype)


def _moe1_kernel(be_ref, nb_ref, nxt_ref, wrap_ref, bv_ref, xs_ref, w_hbm, bg_ref, bl_ref, h_ref, sg, sl, wg_s, wl_s,
                 sems, *, tn):
    used, fresh = _expert_changed(be_ref, nb_ref)
    n_lin = pl.num_programs(0)

    def tile_copies(e, jt):
        return (pltpu.make_async_copy(w_hbm.at[e, :, pl.ds(pl.multiple_of(jt * tn, tn), tn)], sg, sems.at[0]),
                pltpu.make_async_copy(w_hbm.at[e, :, pl.ds(pl.multiple_of((n_lin + jt) * tn, tn), tn)], sl,
                                      sems.at[1]))

    _stream_expert_weights(be_ref, nxt_ref, wrap_ref, used, fresh, tile_copies, (sg, sl), (wg_s, wl_s))

    @pl.when(jnp.logical_not(used))
    def _():
        h_ref[...] = jnp.zeros_like(h_ref)

    def compute(rows):
        x = xs_ref[:rows, :]
        gate = jnp.minimum(_dot(x, wg_s[...]) + bg_ref[...], SWIGLU_LIMIT)
        lin = jnp.clip(_dot(x, wl_s[...]) + bl_ref[...], -SWIGLU_LIMIT, SWIGLU_LIMIT)
        h_ref[:rows, :] = (gate * _sigmoid(SWIGLU_ALPHA * gate) * (lin + 1.0)).astype(h_ref.dtype)

    _for_valid_rows(used, bv_ref, h_ref, compute)


def _moe1(sched, xs, w_gu, b_gu, tm, tn):
    n_rows, dh = xs.shape
    n_exp, d, de2 = w_gu.shape
    de = de2 // 2
    tn = _pick(de, tn)
    nj = de // tn

    def blk(b, nb):
        return jnp.minimum(b, nb[0] - 1)

    grid_spec = pltpu.PrefetchScalarGridSpec(
        num_scalar_prefetch=5,
        grid=(nj, n_rows // tm),
        in_specs=[pl.BlockSpec((tm, dh), lambda j, b, be, nb, *_: (blk(b, nb), 0)),
                  pl.BlockSpec(memory_space=pl.ANY),
                  pl.BlockSpec((None, 1, tn), lambda j, b, be, nb, *_: (be[blk(b, nb)], 0, j)),
                  pl.BlockSpec((None, 1, tn), lambda j, b, be, nb, *_: (be[blk(b, nb)], 0, nj + j))],
        out_specs=pl.BlockSpec((tm, tn), lambda j, b, *_: (b, j)),
        scratch_shapes=[pltpu.VMEM((d, tn), F32), pltpu.VMEM((d, tn), F32),
                        pltpu.VMEM((d, tn), BF16), pltpu.VMEM((d, tn), BF16),
                        pltpu.SemaphoreType.DMA((2,))],
    )
    return pl.pallas_call(
        functools.partial(_moe1_kernel, tn=tn),
        grid_spec=grid_spec,
        out_shape=jax.ShapeDtypeStruct((n_rows, de), BF16),
        compiler_params=_params("arbitrary", "arbitrary"),
        name="moe_up",
    )(*sched, xs, w_gu, b_gu, b_gu)


def _moe2_kernel(be_ref, nb_ref, nxt_ref, wrap_ref, bv_ref, h_ref, w_hbm, b_ref, y_ref, stage, w_s, sems, *, tn):
    used, fresh = _expert_changed(be_ref, nb_ref)

    def tile_copies(e, jt):
        return (pltpu.make_async_copy(w_hbm.at[e, :, pl.ds(pl.multiple_of(jt * tn, tn), tn)], stage, sems.at[0]),)

    _stream_expert_weights(be_ref, nxt_ref, wrap_ref, used, fresh, tile_copies, (stage,), (w_s,))

    @pl.when(jnp.logical_not(used))
    def _():
        y_ref[...] = jnp.zeros_like(y_ref)

    def compute(rows):
        y_ref[:rows, :] = _dot(h_ref[:rows, :], w_s[...]) + b_ref[...]

    _for_valid_rows(used, bv_ref, y_ref, compute)


def _moe2(sched, h, w_down, b_down, tm, tn):
    n_rows, de = h.shape
    n_exp, _, d = w_down.shape
    tn = _pick(d, tn)

    def blk(b, nb):
        return jnp.minimum(b, nb[0] - 1)

    grid_spec = pltpu.PrefetchScalarGridSpec(
        num_scalar_prefetch=5,
        grid=(d // tn, n_rows // tm),
        in_specs=[pl.BlockSpec((tm, de), lambda j, b, be, nb, *_: (blk(b, nb), 0)),
                  pl.BlockSpec(memory_space=pl.ANY),
                  pl.BlockSpec((None, 1, tn), lambda j, b, be, nb, *_: (be[blk(b, nb)], 0, j))],
        out_specs=pl.BlockSpec((tm, tn), lambda j, b, *_: (b, j)),
        scratch_shapes=[pltpu.VMEM((de, tn), F32), pltpu.VMEM((de, tn), BF16), pltpu.SemaphoreType.DMA((1,))],
    )
    return pl.pallas_call(
        functools.partial(_moe2_kernel, tn=tn),
        grid_spec=grid_spec,
        out_shape=jax.ShapeDtypeStruct((n_rows, d), F32),
        compiler_params=_params("arbitrary", "arbitrary"),
        name="moe_down",
    )(*sched, h, w_down, b_down)


def _combine_kernel(dest_ref, dest_nxt_ref, ys_hbm, x1_ref, wgt_ref, g_ref, b_ref, o_ref, buf, sems, *, tc, top_k,
                    alpha):
    i = pl.program_id(0)
    slot = i % 2

    def row_copy(idx_ref, s, t, j):
        return pltpu.make_async_copy(ys_hbm.at[pl.ds(idx_ref[t * top_k + j], 1)], buf.at[s, j, pl.ds(t, 1)],
                                     sems.at[s, j])

    def issue(idx_ref, s):
        def body(t, c):
            for j in range(top_k):
                row_copy(idx_ref, s, t, j).start(priority=j % 2)
            return c
        lax.fori_loop(0, tc, body, 0, unroll=2)

    @pl.when(i == 0)
    def _():
        issue(dest_ref, slot)

    @pl.when(i + 1 < pl.num_programs(0))
    def _():
        issue(dest_nxt_ref, 1 - slot)

    def drain(t, c):
        for j in range(top_k):
            row_copy(dest_ref, slot, t, j).wait()
        return c

    lax.fori_loop(0, tc, drain, 0, unroll=2)
    y = wgt_ref[:, 0:1] * buf[slot, 0]
    for j in range(1, top_k):
        y = y + wgt_ref[:, j:j + 1] * buf[slot, j]
    o_ref[...] = _layer_norm(alpha * x1_ref[...] + y, g_ref[...], b_ref[...])


def _combine(dest_flat, ys, x1, wgt, g, b, alpha, tc):
    t, d = x1.shape
    tc = _pick(t, tc)
    last = t // tc - 1
    return pl.pallas_call(
        functools.partial(_combine_kernel, tc=tc, top_k=TOP_K, alpha=alpha),
        grid=(t // tc,),
        in_specs=[pl.BlockSpec((tc * TOP_K,), lambda i: (i,), memory_space=pltpu.SMEM),
                  pl.BlockSpec((tc * TOP_K,), lambda i: (jnp.minimum(i + 1, last),), memory_space=pltpu.SMEM),
                  pl.BlockSpec(memory_space=pl.ANY),
                  pl.BlockSpec((tc, d), lambda i: (i, 0)),
                  pl.BlockSpec((tc, LANES), lambda i: (i, 0)),
                  pl.BlockSpec((1, d), lambda i: (0, 0)),
                  pl.BlockSpec((1, d), lambda i: (0, 0))],
        out_specs=pl.BlockSpec((tc, d), lambda i: (i, 0)),
        out_shape=jax.ShapeDtypeStruct((t, d), F32),
        scratch_shapes=[pltpu.VMEM((2, TOP_K, tc, d), F32), pltpu.SemaphoreType.DMA((2, TOP_K))],
        compiler_params=_params("arbitrary"),
        name="combine",
    )(dest_flat, dest_flat, ys, x1, wgt, g, b)


def _layer(x2d, w_in, b_in, a_ws, a_bs, a_ln_g, a_ln_b, gla_w_lr, gla_b_lr, gla_gn_g, w_br_a, w_br_b, w_o,
           ln1_g, ln1_b, w_router, b_router, w_gu, b_gu, w_down, b_down, ln2_g, ln2_b, alpha):
    t, d = x2d.shape
    n_groups, a_chunk, _ = a_ws.shape
    aw = a_ln_g.size
    rank, kw = gla_w_lr.shape
    heads, dv = gla_gn_g.shape
    vw = heads * dv
    n_exp = w_router.shape[1]
    de = w_down.shape[1]
    off_lr = 2 * aw + 2 * kw + 2 * vw
    off_q = 2 * aw

    w_in_t = w_in.T
    w_lr_in = jnp.pad(w_in[:, off_lr:off_lr + rank], ((0, 0), (0, LANES - rank))).astype(BF16)
    b_lr_in = jnp.pad(b_in[off_lr:off_lr + rank], (0, LANES - rank)).reshape(1, LANES)
    wlr = jnp.pad(gla_w_lr, ((0, LANES - rank), (0, 0))).astype(BF16)
    wr = jnp.pad(w_router, ((0, 0), (0, LANES - n_exp))).astype(BF16)
    br = jnp.pad(b_router, (0, LANES - n_exp), constant_values=-jnp.inf).reshape(1, LANES)

    xb, lr = _cast_and_decay_proj(x2d, w_lr_in, b_lr_in, TILE_CAST_ROWS)
    z = _matmul_bias_stream(xb, w_in_t, b_in[:off_lr].reshape(1, -1), 0, F32, *TILE_IN_PROJ)
    zg = _matmul_bias_stream(xb, w_in_t, b_in[off_lr + rank:].reshape(1, -1), off_lr + rank, F32, *TILE_IN_PROJ)

    ya = _branch_a(z, a_ws, a_bs, a_ln_g, a_ln_b, aw, TILE_BRANCH_A_CHUNKS * a_chunk)
    yb = _gla(z, lr, wlr, gla_b_lr.reshape(1, kw), gla_gn_g.reshape(1, vw), kw, vw, off_q, heads,
              TILE_GLA_ROWS)
    merged = _merge(ya, w_br_a.astype(BF16), yb, w_br_b.astype(BF16), zg, 0, d, *TILE_MERGE)
    h1 = _out_proj_resid(merged, w_o.astype(BF16), x2d, alpha, *TILE_OUT_PROJ)

    x1, xp, idx, wgt, rnk, cnt = _ln_router(h1, ln1_g.reshape(1, d), ln1_b.reshape(1, d), wr, br, TILE_ROUTER_ROWS)

    n_assign = t * TOP_K
    tm_e = min(EXPERT_BLOCK_ROWS, max(n_assign // n_exp // 2, 8 * DISPATCH_STRIPES))
    n_blocks = -(-(n_assign + n_exp * (tm_e - 1)) // tm_e)
    n_rows = n_blocks * tm_e
    counts = cnt[0, :n_exp].astype(I32)
    padded = (counts + tm_e - 1) // tm_e * tm_e
    pend = jnp.cumsum(padded)
    pstart = pend - padded
    sel = idx[:, :TOP_K, None] == jnp.arange(n_exp, dtype=I32)[None, None, :]
    dest = jnp.sum(jnp.where(sel, pstart[None, None, :], 0), axis=-1) + rnk[:, :TOP_K]
    dest_flat = dest.reshape(-1).astype(I32)
    nb_used = (pend[-1] // tm_e).astype(I32).reshape(1)
    block_start = jnp.arange(n_blocks, dtype=I32) * tm_e
    block_e = jnp.minimum(jnp.sum((pend[None, :] <= block_start[:, None]).astype(I32), axis=1), n_exp - 1)

    tok_buf = jnp.zeros((n_rows,), I32).at[dest_flat].set(jnp.arange(n_assign, dtype=I32) // TOP_K,
                                                          unique_indices=True)
    xs = _dispatch(tok_buf, nb_used, xp, tm_e)
    experts = jnp.arange(n_exp, dtype=I32)
    later = jnp.where((padded > 0)[None, :] & (experts[None, :] > block_e[:, None]), experts[None, :], n_exp)
    nxt = jnp.min(later, axis=1)
    wrap = (nxt == n_exp).astype(I32)
    nxt_e = jnp.where(nxt == n_exp, block_e[0], nxt).astype(I32)
    mine = experts[None, :] == block_e[:, None]
    group_end = jnp.sum(jnp.where(mine, (pstart + counts)[None, :], 0), axis=1)
    bvalid = jnp.clip(group_end - block_start, 0, tm_e).astype(I32)
    sched = (block_e, nb_used, nxt_e, wrap, bvalid)
    hmid = _moe1(sched, xs, w_gu, b_gu.reshape(n_exp, 1, 2 * de), tm_e, TILE_EXPERT_UP_COLS)
    ys = _moe2(sched, hmid, w_down, b_down.reshape(n_exp, 1, d), tm_e, TILE_EXPERT_DOWN_COLS)
    return _combine(dest_flat, ys, x1, wgt, ln2_g.reshape(1, d), ln2_b.reshape(1, d), alpha, TILE_COMBINE_ROWS)


def kernel(x, w_in, b_in, a_ws, a_bs, a_ln_g, a_ln_b, gla_w_lr, gla_b_lr, gla_gn_g, w_br_a, w_br_b, w_o, ln1_g, ln1_b, w_router, b_router, w_gu, b_gu, w_down, b_down, ln2_g, ln2_b):
    bsz, seq, d = x.shape
    depth = w_in.shape[0]
    alpha = (2 * depth) ** 0.25
    outs = []
    for bi in range(bsz):
        h = x[bi]
        for l in range(depth):
            h = _layer(h, w_in[l], b_in[l], a_ws[l], a_bs[l], a_ln_g[l], a_ln_b[l], gla_w_lr[l], gla_b_lr[l],
                       gla_gn_g[l], w_br_a[l], w_br_b[l], w_o[l], ln1_g[l], ln1_b[l], w_router[l], b_router[l],
                       w_gu[l], b_gu[l], w_down[l], b_down[l], ln2_g[l], ln2_b[l], alpha)
        outs.append(h)
    return jnp.stack(outs) if bsz > 1 else outs[0][None]
```

```python
import functools

import jax
import jax.numpy as jnp
from jax import lax
from jax.experimental import pallas as pl
from jax.experimental.pallas import tpu as pltpu

F32 = jnp.float32
BF16 = jnp.bfloat16
U32 = jnp.uint32
I32 = jnp.int32

LN_EPS = 1e-5
GLA_CHUNK = 64
GLA_GATE_TAU = 16.0
TOP_K = 4
SWIGLU_LIMIT = 7.0
SWIGLU_ALPHA = 1.702
LANES = 128
SUBLANES = 8
VMEM_LIMIT = 56 * 1024 * 1024
HI_MASK = 0xFFFF0000
DISPATCH_STRIPES = 8

TILE_IN_PROJ = (1024, 1024)
TILE_CAST_ROWS = 512
TILE_BRANCH_A_CHUNKS = 2
TILE_GLA_ROWS = 256
TILE_MERGE = (1024, 512)
TILE_OUT_PROJ = (1024, 512)
TILE_ROUTER_ROWS = 256
EXPERT_BLOCK_ROWS = 512
TILE_EXPERT_UP_COLS = 768
TILE_EXPERT_DOWN_COLS = 2048
TILE_COMBINE_ROWS = 128


def _pick(n, pref):
    t = min(pref, n)
    while n % t:
        t //= 2
    return t


def _dot(a, b):
    return jnp.dot(a, b, preferred_element_type=F32)


def _dot_t0(a, b):
    return lax.dot_general(a, b, (((0,), (0,)), ((), ())), preferred_element_type=F32)


def _dot_t1(a, b):
    return lax.dot_general(a, b, (((1,), (1,)), ((), ())), preferred_element_type=F32)


def _sigmoid(x):
    return 1.0 / (1.0 + jnp.exp(-x))


def _gelu(x):
    return 0.5 * x * (1.0 + lax.erf(x * (2.0 ** -0.5)))


def _layer_norm(x, g, b):
    mu = jnp.mean(x, axis=-1, keepdims=True)
    xc = x - mu
    var = jnp.mean(xc * xc, axis=-1, keepdims=True)
    return xc * lax.rsqrt(var + LN_EPS) * g + b


def _params(*sem):
    return pltpu.CompilerParams(dimension_semantics=sem, vmem_limit_bytes=VMEM_LIMIT)


def _cast_lr_kernel(x_ref, w_ref, b_ref, xb_ref, lr_ref):
    xb = x_ref[...].astype(BF16)
    xb_ref[...] = xb
    lr_ref[...] = _dot(xb, w_ref[...]) + b_ref[...]


def _cast_and_decay_proj(x, w, b, tm):
    m, k = x.shape
    n = w.shape[1]
    tm = _pick(m, tm)
    return pl.pallas_call(
        _cast_lr_kernel,
        grid=(m // tm,),
        in_specs=[pl.BlockSpec((tm, k), lambda i: (i, 0)),
                  pl.BlockSpec((k, n), lambda i: (0, 0)),
                  pl.BlockSpec((1, n), lambda i: (0, 0))],
        out_specs=[pl.BlockSpec((tm, k), lambda i: (i, 0)),
                   pl.BlockSpec((tm, n), lambda i: (i, 0))],
        out_shape=[jax.ShapeDtypeStruct((m, k), BF16), jax.ShapeDtypeStruct((m, n), F32)],
        compiler_params=_params("parallel"),
        name="cast_lr",
    )(x, w, b)


def _mm_stream_kernel(x_ref, wt_hbm, b_ref, o_ref, stage, work, sem, *, tn, row0):
    j, i = pl.program_id(0), pl.program_id(1)

    def w_copy(jt):
        return pltpu.make_async_copy(wt_hbm.at[pl.ds(pl.multiple_of(row0 + jt * tn, 16), tn), :], stage, sem.at[0])

    @pl.when(jnp.logical_and(j == 0, i == 0))
    def _():
        w_copy(0).start()

    @pl.when(i == 0)
    def _():
        w_copy(j).wait()
        work[...] = stage[...].astype(BF16)

        @pl.when(j + 1 < pl.num_programs(0))
        def _():
            w_copy(j + 1).start()

    o_ref[...] = (_dot_t1(x_ref[...], work[...]) + b_ref[...]).astype(o_ref.dtype)


def _matmul_bias_stream(xb, wt, b, row0, out_dtype, tm, tn):
    m, k = xb.shape
    n = b.shape[1]
    tm, tn = _pick(m, tm), _pick(n, tn)
    assert row0 % 16 == 0 and tn % 16 == 0
    return pl.pallas_call(
        functools.partial(_mm_stream_kernel, tn=tn, row0=row0),
        grid=(n // tn, m // tm),
        in_specs=[pl.BlockSpec((tm, k), lambda j, i: (i, 0)),
                  pl.BlockSpec(memory_space=pl.ANY),
                  pl.BlockSpec((1, tn), lambda j, i: (0, j))],
        out_specs=pl.BlockSpec((tm, tn), lambda j, i: (i, j)),
        out_shape=jax.ShapeDtypeStruct((m, n), out_dtype),
        scratch_shapes=[pltpu.VMEM((tn, k), F32), pltpu.VMEM((tn, k), BF16), pltpu.SemaphoreType.DMA((1,))],
        compiler_params=_params("arbitrary", "arbitrary"),
        name="in_proj",
    )(xb, wt, b)


def _merge_kernel(ya_ref, wa_ref, yb_ref, wb_ref, ga_ref, gb_ref, o_ref):
    a = _sigmoid(ga_ref[...]) * _dot(ya_ref[...], wa_ref[...])
    o_ref[...] = (a + _sigmoid(gb_ref[...]) * _dot(yb_ref[...], wb_ref[...])).astype(o_ref.dtype)


def _merge(ya, wa, yb, wb, z, off_ga, off_gb, tm, tn):
    m, ka = ya.shape
    kb = yb.shape[1]
    n = wa.shape[1]
    tm, tn = _pick(m, tm), _pick(n, tn)
    while off_ga % tn or off_gb % tn:
        tn //= 2
    oa, ob = off_ga // tn, off_gb // tn
    return pl.pallas_call(
        _merge_kernel,
        grid=(m // tm, n // tn),
        in_specs=[pl.BlockSpec((tm, ka), lambda i, j: (i, 0)),
                  pl.BlockSpec((ka, tn), lambda i, j: (0, j)),
                  pl.BlockSpec((tm, kb), lambda i, j: (i, 0)),
                  pl.BlockSpec((kb, tn), lambda i, j: (0, j)),
                  pl.BlockSpec((tm, tn), lambda i, j: (i, oa + j)),
                  pl.BlockSpec((tm, tn), lambda i, j: (i, ob + j))],
        out_specs=pl.BlockSpec((tm, tn), lambda i, j: (i, j)),
        out_shape=jax.ShapeDtypeStruct((m, n), BF16),
        compiler_params=_params("parallel", "parallel"),
        name="merge",
    )(ya, wa, yb, wb, z, z)


def _mm_resid_kernel(m_ref, w_ref, x_ref, o_ref, *, alpha):
    o_ref[...] = alpha * x_ref[...] + _dot(m_ref[...], w_ref[...])


def _out_proj_resid(mb, w, x, alpha, tm, tn):
    m, k = mb.shape
    n = w.shape[1]
    tm, tn = _pick(m, tm), _pick(n, tn)
    return pl.pallas_call(
        functools.partial(_mm_resid_kernel, alpha=alpha),
        grid=(m // tm, n // tn),
        in_specs=[pl.BlockSpec((tm, k), lambda i, j: (i, 0)),
                  pl.BlockSpec((k, tn), lambda i, j: (0, j)),
                  pl.BlockSpec((tm, tn), lambda i, j: (i, j))],
        out_specs=pl.BlockSpec((tm, tn), lambda i, j: (i, j)),
        out_shape=jax.ShapeDtypeStruct((m, n), F32),
        compiler_params=_params("parallel", "parallel"),
        name="out_proj",
    )(mb, w, x)


def _branch_a_kernel(u_ref, v_ref, ws_ref, bs_ref, g_ref, b_ref, o_ref, *, n_groups, chunk):
    tm, aw = u_ref.shape
    gd = aw // n_groups
    row = lax.broadcasted_iota(I32, (chunk, chunk), 0)
    col = lax.broadcasted_iota(I32, (chunk, chunk), 1)
    causal = row >= col
    for g in range(n_groups):
        ws = jnp.where(causal, ws_ref[g], 0.0).astype(BF16)
        cs = slice(g * gd, (g + 1) * gd)
        for c in range(tm // chunk):
            rs = slice(c * chunk, (c + 1) * chunk)
            vn = _layer_norm(_gelu(v_ref[rs, cs]), g_ref[:, cs], b_ref[:, cs])
            mixed = _dot(ws, vn.astype(BF16)) + bs_ref[:, g:g + 1]
            o_ref[rs, cs] = (_gelu(u_ref[rs, cs]) * mixed).astype(o_ref.dtype)


def _branch_a(z, a_ws, a_bs, a_ln_g, a_ln_b, aw, tm):
    t = z.shape[0]
    n_groups, chunk, _ = a_ws.shape
    tm = max(_pick(t, tm), chunk)
    return pl.pallas_call(
        functools.partial(_branch_a_kernel, n_groups=n_groups, chunk=chunk),
        grid=(t // tm,),
        in_specs=[pl.BlockSpec((tm, aw), lambda i: (i, 0)),
                  pl.BlockSpec((tm, aw), lambda i: (i, 1)),
                  pl.BlockSpec((n_groups, chunk, chunk), lambda i: (0, 0, 0)),
                  pl.BlockSpec((chunk, n_groups), lambda i: (0, 0)),
                  pl.BlockSpec((1, aw), lambda i: (0, 0)),
                  pl.BlockSpec((1, aw), lambda i: (0, 0))],
        out_specs=pl.BlockSpec((tm, aw), lambda i: (i, 0)),
        out_shape=jax.ShapeDtypeStruct((t, aw), BF16),
        compiler_params=_params("parallel"),
        name="branch_a",
    )(z, z, a_ws, a_bs.T, a_ln_g.reshape(1, aw), a_ln_b.reshape(1, aw))


def _gla_kernel(q_ref, k_ref, v_ref, r_ref, lr_ref, wlr_ref, blr_ref, gn_ref, o_ref, s_ref, *, heads, chunk):
    @pl.when(pl.program_id(0) == 0)
    def _():
        s_ref[...] = jnp.zeros_like(s_ref)

    tg, kw = q_ref.shape
    vw = v_ref.shape[1]
    dk, dv = kw // heads, vw // heads
    row = lax.broadcasted_iota(I32, (chunk, chunk), 0)
    col = lax.broadcasted_iota(I32, (chunk, chunk), 1)
    causal = row >= col
    tri = causal.astype(BF16)
    ones = jnp.ones((chunk, LANES), BF16)
    scale = dk ** -0.5

    def body(c, carry):
        rs = pl.ds(pl.multiple_of(c * chunk, chunk), chunk)
        gl = _dot(lr_ref[rs, :].astype(BF16), wlr_ref[...]) + blr_ref[...]
        la = (jnp.minimum(gl, 0.0) - jnp.log1p(jnp.exp(-jnp.abs(gl)))) * (1.0 / GLA_GATE_TAU)
        la_hi = la.astype(BF16)
        la_lo = (la - la_hi.astype(F32)).astype(BF16)
        cum = _dot(tri, la_hi) + _dot(tri, la_lo)
        tot = cum[chunk - 1:chunk, :]
        tot_col = _dot_t0(la_hi, ones) + _dot_t0(la_lo, ones)
        dec_col = jnp.exp(tot_col)
        q = q_ref[rs, :] * scale
        k = k_ref[rs, :]
        qt = (q * jnp.exp(cum)).astype(BF16)
        kt = (k * jnp.exp(-cum)).astype(BF16)
        kl = (k * jnp.exp(tot - cum)).astype(BF16)
        for h in range(heads):
            ks = slice(h * dk, (h + 1) * dk)
            vs = slice(h * dv, (h + 1) * dv)
            vh = v_ref[rs, vs].astype(BF16)
            sc = jnp.where(causal, _dot_t1(qt[:, ks], kt[:, ks]), 0.0).astype(BF16)
            state = s_ref[h]
            o = _dot(sc, vh) + _dot(qt[:, ks], state.astype(BF16))
            dec = jnp.concatenate([dec_col[ks, :]] * (dv // LANES), axis=1) if dv >= LANES else dec_col[ks, :dv]
            s_ref[h] = dec * state + _dot_t0(kl[:, ks], vh)
            on = o * lax.rsqrt(jnp.mean(o * o, axis=-1, keepdims=True) + LN_EPS) * gn_ref[:, vs]
            rr = r_ref[rs, vs]
            o_ref[rs, vs] = (on * (rr * _sigmoid(rr))).astype(o_ref.dtype)
        return carry

    lax.fori_loop(0, tg // chunk, body, 0)


def _gla(z, lr, wlr, blr, gn, kw, vw, off_q, heads, tg):
    t = z.shape[0]
    tg = max(_pick(t, tg), GLA_CHUNK)
    oq = off_q // kw
    ok = oq + 1
    ov = (off_q + 2 * kw) // vw
    orr = ov + 1
    return pl.pallas_call(
        functools.partial(_gla_kernel, heads=heads, chunk=GLA_CHUNK),
        grid=(t // tg,),
        in_specs=[pl.BlockSpec((tg, kw), lambda i: (i, oq)),
                  pl.BlockSpec((tg, kw), lambda i: (i, ok)),
                  pl.BlockSpec((tg, vw), lambda i: (i, ov)),
                  pl.BlockSpec((tg, vw), lambda i: (i, orr)),
                  pl.BlockSpec((tg, LANES), lambda i: (i, 0)),
                  pl.BlockSpec((LANES, kw), lambda i: (0, 0)),
                  pl.BlockSpec((1, kw), lambda i: (0, 0)),
                  pl.BlockSpec((1, vw), lambda i: (0, 0))],
        out_specs=pl.BlockSpec((tg, vw), lambda i: (i, 0)),
        out_shape=jax.ShapeDtypeStruct((t, vw), BF16),
        scratch_shapes=[pltpu.VMEM((heads, kw // heads, vw // heads), F32)],
        compiler_params=_params("arbitrary"),
        name="gla",
    )(z, z, z, z, lr, wlr, blr, gn)


def _ln_router_kernel(h_ref, g_ref, b_ref, wr_ref, br_ref, x1_ref, xp_ref, idx_ref, wgt_ref, rank_ref, cnt_ref,
                      base_ref, *, top_k):
    @pl.when(pl.program_id(0) == 0)
    def _():
        base_ref[...] = jnp.zeros_like(base_ref)

    tm, d = h_ref.shape
    half = d // 2
    x1 = _layer_norm(h_ref[...], g_ref[...], b_ref[...])
    x1_ref[...] = x1
    lo = pltpu.bitcast(x1[:, :half].astype(BF16).astype(F32), U32) >> 16
    hi = pltpu.bitcast(x1[:, half:].astype(BF16).astype(F32), U32) & jnp.uint32(HI_MASK)
    xp_ref[...] = lo | hi

    logits = _dot(x1.astype(BF16), wr_ref[...]) + br_ref[...]
    lane = lax.broadcasted_iota(I32, (tm, LANES), 1).astype(F32)
    member = jnp.zeros((tm, LANES), F32)
    vals, idxs = [], []
    cur = logits
    for _ in range(top_k):
        mx = jnp.max(cur, axis=-1, keepdims=True)
        ix = jnp.min(jnp.where(cur == mx, lane, float(LANES)), axis=-1, keepdims=True)
        sel = lane == ix
        vals.append(mx)
        idxs.append(ix)
        member = member + sel.astype(F32)
        cur = jnp.where(sel, -jnp.inf, cur)
    exps = [jnp.exp(v - vals[0]) for v in vals]
    denom = exps[0]
    for e in exps[1:]:
        denom = denom + e
    r_i = lax.broadcasted_iota(I32, (tm, tm), 0)
    c_i = lax.broadcasted_iota(I32, (tm, tm), 1)
    before = _dot((r_i > c_i).astype(BF16), member.astype(BF16)) + base_ref[...]
    idx_out = jnp.zeros((tm, LANES), F32)
    wgt_out = jnp.zeros((tm, LANES), F32)
    rank_out = jnp.zeros((tm, LANES), F32)
    for j in range(top_k):
        rk = jnp.sum(jnp.where(lane == idxs[j], before, 0.0), axis=-1, keepdims=True)
        at = lane == float(j)
        idx_out = jnp.where(at, idxs[j], idx_out)
        wgt_out = jnp.where(at, exps[j] / denom, wgt_out)
        rank_out = jnp.where(at, rk, rank_out)
    idx_ref[...] = idx_out.astype(I32)
    wgt_ref[...] = wgt_out
    rank_ref[...] = rank_out.astype(I32)
    base_ref[...] = base_ref[...] + jnp.sum(member, axis=0, keepdims=True)
    cnt_ref[...] = base_ref[...]


def _ln_router(h1, g, b, wr, br, tm):
    t, d = h1.shape
    tm = _pick(t, tm)
    row = lambda i: (i, 0)
    fixed = lambda i: (0, 0)
    return pl.pallas_call(
        functools.partial(_ln_router_kernel, top_k=TOP_K),
        grid=(t // tm,),
        in_specs=[pl.BlockSpec((tm, d), row),
                  pl.BlockSpec((1, d), fixed),
                  pl.BlockSpec((1, d), fixed),
                  pl.BlockSpec((d, LANES), fixed),
                  pl.BlockSpec((1, LANES), fixed)],
        out_specs=[pl.BlockSpec((tm, d), row),
                   pl.BlockSpec((tm, d // 2), row),
                   pl.BlockSpec((tm, LANES), row),
                   pl.BlockSpec((tm, LANES), row),
                   pl.BlockSpec((tm, LANES), row),
                   pl.BlockSpec((1, LANES), fixed)],
        out_shape=[jax.ShapeDtypeStruct((t, d), F32),
                   jax.ShapeDtypeStruct((t, d // 2), U32),
                   jax.ShapeDtypeStruct((t, LANES), I32),
                   jax.ShapeDtypeStruct((t, LANES), F32),
                   jax.ShapeDtypeStruct((t, LANES), I32),
                   jax.ShapeDtypeStruct((1, LANES), F32)],
        scratch_shapes=[pltpu.VMEM((1, LANES), F32)],
        compiler_params=_params("arbitrary"),
        name="ln_router",
    )(h1, g, b, wr, br)


def _dispatch_kernel(nb_ref, tok_ref, tok_nxt_ref, xp_hbm, xs_ref, buf, sems, *, tm_e):
    b = pl.program_id(0)
    slot = b % 2
    used = b < nb_ref[0]
    stripe = tm_e // DISPATCH_STRIPES

    def row_copy(idx_ref, s, u, k8, r):
        k0 = pl.multiple_of(k8 * SUBLANES, SUBLANES)
        return pltpu.make_async_copy(xp_hbm.at[pl.ds(idx_ref[u * stripe + k0 + r], 1)],
                                     buf.at[s, u, pl.ds(k0 + r, 1)], sems.at[s])

    def issue(idx_ref, s):
        def body(k8, c):
            for r in range(SUBLANES):
                for u in range(DISPATCH_STRIPES):
                    row_copy(idx_ref, s, u, k8, r).start(priority=u % 2)
            return c
        lax.fori_loop(0, stripe // SUBLANES, body, 0)

    @pl.when(b == 0)
    def _():
        issue(tok_ref, slot)

    @pl.when(b + 1 < nb_ref[0])
    def _():
        issue(tok_nxt_ref, 1 - slot)

    @pl.when(jnp.logical_not(used))
    def _():
        xs_ref[...] = jnp.zeros_like(xs_ref)

    @pl.when(used)
    def _():
        def drain(k8, c):
            for r in range(SUBLANES):
                for u in range(DISPATCH_STRIPES):
                    row_copy(tok_ref, slot, u, k8, r).wait()
            return c

        lax.fori_loop(0, stripe // SUBLANES, drain, 0)
        w = buf[slot].reshape(tm_e, buf.shape[-1])
        half = w.shape[1]
        xs_ref[:, :half] = pltpu.bitcast(w << 16, F32).astype(BF16)
        xs_ref[:, half:] = pltpu.bitcast(w & jnp.uint32(HI_MASK), F32).astype(BF16)


def _dispatch(tok_buf, nb_used, xp, tm_e):
    dh = xp.shape[1]
    n_rows = tok_buf.shape[0]
    last = n_rows // tm_e - 1
    grid_spec = pltpu.PrefetchScalarGridSpec(
        num_scalar_prefetch=1,
        grid=(n_rows // tm_e,),
        in_specs=[pl.BlockSpec((tm_e,), lambda b, nb: (b,), memory_space=pltpu.SMEM),
                  pl.BlockSpec((tm_e,), lambda b, nb: (jnp.minimum(b + 1, last),), memory_space=pltpu.SMEM),
                  pl.BlockSpec(memory_space=pl.ANY)],
        out_specs=pl.BlockSpec((tm_e, 2 * dh), lambda b, nb: (b, 0)),
        scratch_shapes=[pltpu.VMEM((2, DISPATCH_STRIPES, tm_e // DISPATCH_STRIPES, dh), U32),
                        pltpu.SemaphoreType.DMA((2,))],
    )
    return pl.pallas_call(
        functools.partial(_dispatch_kernel, tm_e=tm_e),
        grid_spec=grid_spec,
        out_shape=jax.ShapeDtypeStruct((n_rows, 2 * dh), BF16),
        compiler_params=_params("arbitrary"),
        name="dispatch",
    )(nb_used, tok_buf, tok_buf, xp)


def _expert_changed(be_ref, nb_ref):
    b = pl.program_id(1)
    fresh = jnp.logical_or(b == 0, be_ref[b] != be_ref[jnp.maximum(b - 1, 0)])
    return b < nb_ref[0], fresh


def _stream_expert_weights(be_ref, nxt_ref, wrap_ref, used, fresh, tile_copies, stage, work):
    j, b = pl.program_id(0), pl.program_id(1)
    nj = pl.num_programs(0)

    @pl.when(jnp.logical_and(j == 0, b == 0))
    def _():
        for c in tile_copies(be_ref[0], 0):
            c.start()

    @pl.when(jnp.logical_and(used, fresh))
    def _():
        for c in tile_copies(be_ref[b], j):
            c.wait()
        for src, dst in zip(stage, work):
            dst[...] = src[...].astype(BF16)
        jn = j + wrap_ref[b]

        @pl.when(jn < nj)
        def _():
            for c in tile_copies(nxt_ref[b], jn):
                c.start()


def _for_valid_rows(used, bv_ref, out_ref, compute):
    tm = out_ref.shape[0]
    half = tm // 2
    few = bv_ref[pl.program_id(1)] <= half

    @pl.when(jnp.logical_and(used, jnp.logical_not(few)))
    def _():
        compute(tm)

    @pl.when(jnp.logical_and(used, few))
    def _():
        compute(half)
        out_ref[half:, :] = jnp.zeros((tm - half, out_ref.shape[1]), out_ref.dtype)


def _moe1_kernel(be_ref, nb_ref, nxt_ref, wrap_ref, bv_ref, xs_ref, w_hbm, bg_ref, bl_ref, h_ref, sg, sl, wg_s, wl_s,
                 sems, *, tn):
    used, fresh = _expert_changed(be_ref, nb_ref)
    n_lin = pl.num_programs(0)

    def tile_copies(e, jt):
        return (pltpu.make_async_copy(w_hbm.at[e, :, pl.ds(pl.multiple_of(jt * tn, tn), tn)], sg, sems.at[0]),
                pltpu.make_async_copy(w_hbm.at[e, :, pl.ds(pl.multiple_of((n_lin + jt) * tn, tn), tn)], sl,
                                      sems.at[1]))

    _stream_expert_weights(be_ref, nxt_ref, wrap_ref, used, fresh, tile_copies, (sg, sl), (wg_s, wl_s))

    @pl.when(jnp.logical_not(used))
    def _():
        h_ref[...] = jnp.zeros_like(h_ref)

    def compute(rows):
        x = xs_ref[:rows, :]
        gate = jnp.minimum(_dot(x, wg_s[...]) + bg_ref[...], SWIGLU_LIMIT)
        lin = jnp.clip(_dot(x, wl_s[...]) + bl_ref[...], -SWIGLU_LIMIT, SWIGLU_LIMIT)
        h_ref[:rows, :] = (gate * _sigmoid(SWIGLU_ALPHA * gate) * (lin + 1.0)).astype(h_ref.dtype)

    _for_valid_rows(used, bv_ref, h_ref, compute)


def _moe1(sched, xs, w_gu, b_gu, tm, tn):
    n_rows, dh = xs.shape
    n_exp, d, de2 = w_gu.shape
    de = de2 // 2
    tn = _pick(de, tn)
    nj = de // tn

    def blk(b, nb):
        return jnp.minimum(b, nb[0] - 1)

    grid_spec = pltpu.PrefetchScalarGridSpec(
        num_scalar_prefetch=5,
        grid=(nj, n_rows // tm),
        in_specs=[pl.BlockSpec((tm, dh), lambda j, b, be, nb, *_: (blk(b, nb), 0)),
                  pl.BlockSpec(memory_space=pl.ANY),
                  pl.BlockSpec((None, 1, tn), lambda j, b, be, nb, *_: (be[blk(b, nb)], 0, j)),
                  pl.BlockSpec((None, 1, tn), lambda j, b, be, nb, *_: (be[blk(b, nb)], 0, nj + j))],
        out_specs=pl.BlockSpec((tm, tn), lambda j, b, *_: (b, j)),
        scratch_shapes=[pltpu.VMEM((d, tn), F32), pltpu.VMEM((d, tn), F32),
                        pltpu.VMEM((d, tn), BF16), pltpu.VMEM((d, tn), BF16),
                        pltpu.SemaphoreType.DMA((2,))],
    )
    return pl.pallas_call(
        functools.partial(_moe1_kernel, tn=tn),
        grid_spec=grid_spec,
        out_shape=jax.ShapeDtypeStruct((n_rows, de), BF16),
        compiler_params=_params("arbitrary", "arbitrary"),
        name="moe_up",
    )(*sched, xs, w_gu, b_gu, b_gu)


def _moe2_kernel(be_ref, nb_ref, nxt_ref, wrap_ref, bv_ref, h_ref, w_hbm, b_ref, y_ref, stage, w_s, sems, *, tn):
    used, fresh = _expert_changed(be_ref, nb_ref)

    def tile_copies(e, jt):
        return (pltpu.make_async_copy(w_hbm.at[e, :, pl.ds(pl.multiple_of(jt * tn, tn), tn)], stage, sems.at[0]),)

    _stream_expert_weights(be_ref, nxt_ref, wrap_ref, used, fresh, tile_copies, (stage,), (w_s,))

    @pl.when(jnp.logical_not(used))
    def _():
        y_ref[...] = jnp.zeros_like(y_ref)

    def compute(rows):
        y_ref[:rows, :] = _dot(h_ref[:rows, :], w_s[...]) + b_ref[...]

    _for_valid_rows(used, bv_ref, y_ref, compute)


def _moe2(sched, h, w_down, b_down, tm, tn):
    n_rows, de = h.shape
    n_exp, _, d = w_down.shape
    tn = _pick(d, tn)

    def blk(b, nb):
        return jnp.minimum(b, nb[0] - 1)

    grid_spec = pltpu.PrefetchScalarGridSpec(
        num_scalar_prefetch=5,
        grid=(d // tn, n_rows // tm),
        in_specs=[pl.BlockSpec((tm, de), lambda j, b, be, nb, *_: (blk(b, nb), 0)),
                  pl.BlockSpec(memory_space=pl.ANY),
                  pl.BlockSpec((None, 1, tn), lambda j, b, be, nb, *_: (be[blk(b, nb)], 0, j))],
        out_specs=pl.BlockSpec((tm, tn), lambda j, b, *_: (b, j)),
        scratch_shapes=[pltpu.VMEM((de, tn), F32), pltpu.VMEM((de, tn), BF16), pltpu.SemaphoreType.DMA((1,))],
    )
    return pl.pallas_call(
        functools.partial(_moe2_kernel, tn=tn),
        grid_spec=grid_spec,
        out_shape=jax.ShapeDtypeStruct((n_rows, d), F32),
        compiler_params=_params("arbitrary", "arbitrary"),
        name="moe_down",
    )(*sched, h, w_down, b_down)


def _combine_kernel(dest_ref, dest_nxt_ref, ys_hbm, x1_ref, wgt_ref, g_ref, b_ref, o_ref, buf, sems, *, tc, top_k,
                    alpha):
    i = pl.program_id(0)
    slot = i % 2

    def row_copy(idx_ref, s, t8, r, j):
        t0 = pl.multiple_of(t8 * SUBLANES, SUBLANES)
        return pltpu.make_async_copy(ys_hbm.at[pl.ds(idx_ref[(t0 + r) * top_k + j], 1)],
                                     buf.at[s, j, pl.ds(t0 + r, 1)], sems.at[s])

    def issue(idx_ref, s):
        def body(t8, c):
            for r in range(SUBLANES):
                for j in range(top_k):
                    row_copy(idx_ref, s, t8, r, j).start(priority=j % 2)
            return c
        lax.fori_loop(0, tc // SUBLANES, body, 0)

    @pl.when(i == 0)
    def _():
        issue(dest_ref, slot)

    @pl.when(i + 1 < pl.num_programs(0))
    def _():
        issue(dest_nxt_ref, 1 - slot)

    def drain(t8, c):
        for r in range(SUBLANES):
            for j in range(top_k):
                row_copy(dest_ref, slot, t8, r, j).wait()
        return c

    lax.fori_loop(0, tc // SUBLANES, drain, 0)
    y = wgt_ref[:, 0:1] * buf[slot, 0]
    for j in range(1, top_k):
        y = y + wgt_ref[:, j:j + 1] * buf[slot, j]
    o_ref[...] = _layer_norm(alpha * x1_ref[...] + y, g_ref[...], b_ref[...])


def _combine(dest_flat, ys, x1, wgt, g, b, alpha, tc):
    t, d = x1.shape
    tc = _pick(t, tc)
    last = t // tc - 1
    return pl.pallas_call(
        functools.partial(_combine_kernel, tc=tc, top_k=TOP_K, alpha=alpha),
        grid=(t // tc,),
        in_specs=[pl.BlockSpec((tc * TOP_K,), lambda i: (i,), memory_space=pltpu.SMEM),
                  pl.BlockSpec((tc * TOP_K,), lambda i: (jnp.minimum(i + 1, last),), memory_space=pltpu.SMEM),
                  pl.BlockSpec(memory_space=pl.ANY),
                  pl.BlockSpec((tc, d), lambda i: (i, 0)),
                  pl.BlockSpec((tc, LANES), lambda i: (i, 0)),
                  pl.BlockSpec((1, d), lambda i: (0, 0)),
                  pl.BlockSpec((1, d), lambda i: (0, 0))],
        out_specs=pl.BlockSpec((tc, d), lambda i: (i, 0)),
        out_shape=jax.ShapeDtypeStruct((t, d), F32),
        scratch_shapes=[pltpu.VMEM((2, TOP_K, tc, d), F32), pltpu.SemaphoreType.DMA((2,))],
        compiler_params=_params("arbitrary"),
        name="combine",
    )(dest_flat, dest_flat, ys, x1, wgt, g, b)


def _layer(x2d, w_in, b_in, a_ws, a_bs, a_ln_g, a_ln_b, gla_w_lr, gla_b_lr, gla_gn_g, w_br_a, w_br_b, w_o,
           ln1_g, ln1_b, w_router, b_router, w_gu, b_gu, w_down, b_down, ln2_g, ln2_b, alpha):
    t, d = x2d.shape
    n_groups, a_chunk, _ = a_ws.shape
    aw = a_ln_g.size
    rank, kw = gla_w_lr.shape
    heads, dv = gla_gn_g.shape
    vw = heads * dv
    n_exp = w_router.shape[1]
    de = w_down.shape[1]
    off_lr = 2 * aw + 2 * kw + 2 * vw
    off_q = 2 * aw

    w_in_t = w_in.T
    w_lr_in = jnp.pad(w_in[:, off_lr:off_lr + rank], ((0, 0), (0, LANES - rank))).astype(BF16)
    b_lr_in = jnp.pad(b_in[off_lr:off_lr + rank], (0, LANES - rank)).reshape(1, LANES)
    wlr = jnp.pad(gla_w_lr, ((0, LANES - rank), (0, 0))).astype(BF16)
    wr = jnp.pad(w_router, ((0, 0), (0, LANES - n_exp))).astype(BF16)
    br = jnp.pad(b_router, (0, LANES - n_exp), constant_values=-jnp.inf).reshape(1, LANES)

    xb, lr = _cast_and_decay_proj(x2d, w_lr_in, b_lr_in, TILE_CAST_ROWS)
    z = _matmul_bias_stream(xb, w_in_t, b_in[:off_lr].reshape(1, -1), 0, F32, *TILE_IN_PROJ)
    zg = _matmul_bias_stream(xb, w_in_t, b_in[off_lr + rank:].reshape(1, -1), off_lr + rank, F32, *TILE_IN_PROJ)

    ya = _branch_a(z, a_ws, a_bs, a_ln_g, a_ln_b, aw, TILE_BRANCH_A_CHUNKS * a_chunk)
    yb = _gla(z, lr, wlr, gla_b_lr.reshape(1, kw), gla_gn_g.reshape(1, vw), kw, vw, off_q, heads,
              TILE_GLA_ROWS)
    merged = _merge(ya, w_br_a.astype(BF16), yb, w_br_b.astype(BF16), zg, 0, d, *TILE_MERGE)
    h1 = _out_proj_resid(merged, w_o.astype(BF16), x2d, alpha, *TILE_OUT_PROJ)

    x1, xp, idx, wgt, rnk, cnt = _ln_router(h1, ln1_g.reshape(1, d), ln1_b.reshape(1, d), wr, br, TILE_ROUTER_ROWS)

    n_assign = t * TOP_K
    tm_e = min(EXPERT_BLOCK_ROWS, max(n_assign // n_exp // 2, 8 * DISPATCH_STRIPES))
    n_blocks = -(-(n_assign + n_exp * (tm_e - 1)) // tm_e)
    n_rows = n_blocks * tm_e
    counts = cnt[0, :n_exp].astype(I32)
    padded = (counts + tm_e - 1) // tm_e * tm_e
    pend = jnp.cumsum(padded)
    pstart = pend - padded
    sel = idx[:, :TOP_K, None] == jnp.arange(n_exp, dtype=I32)[None, None, :]
    dest = jnp.sum(jnp.where(sel, pstart[None, None, :], 0), axis=-1) + rnk[:, :TOP_K]
    dest_flat = dest.reshape(-1).astype(I32)
    nb_used = (pend[-1] // tm_e).astype(I32).reshape(1)
    block_start = jnp.arange(n_blocks, dtype=I32) * tm_e
    block_e = jnp.minimum(jnp.sum((pend[None, :] <= block_start[:, None]).astype(I32), axis=1), n_exp - 1)

    tok_buf = jnp.zeros((n_rows,), I32).at[dest_flat].set(jnp.arange(n_assign, dtype=I32) // TOP_K,
                                                          unique_indices=True)
    xs = _dispatch(tok_buf, nb_used, xp, tm_e)
    experts = jnp.arange(n_exp, dtype=I32)
    later = jnp.where((padded > 0)[None, :] & (experts[None, :] > block_e[:, None]), experts[None, :], n_exp)
    nxt = jnp.min(later, axis=1)
    wrap = (nxt == n_exp).astype(I32)
    nxt_e = jnp.where(nxt == n_exp, block_e[0], nxt).astype(I32)
    mine = experts[None, :] == block_e[:, None]
    group_end = jnp.sum(jnp.where(mine, (pstart + counts)[None, :], 0), axis=1)
    bvalid = jnp.clip(group_end - block_start, 0, tm_e).astype(I32)
    sched = (block_e, nb_used, nxt_e, wrap, bvalid)
    hmid = _moe1(sched, xs, w_gu, b_gu.reshape(n_exp, 1, 2 * de), tm_e, TILE_EXPERT_UP_COLS)
    ys = _moe2(sched, hmid, w_down, b_down.reshape(n_exp, 1, d), tm_e, TILE_EXPERT_DOWN_COLS)
    return _combine(dest_flat, ys, x1, wgt, ln2_g.reshape(1, d), ln2_b.reshape(1, d), alpha, TILE_COMBINE_ROWS)


def kernel(x, w_in, b_in, a_ws, a_bs, a_ln_g, a_ln_b, gla_w_lr, gla_b_lr, gla_gn_g, w_br_a, w_br_b, w_o, ln1_g, ln1_b, w_router, b_router, w_gu, b_gu, w_down, b_down, ln2_g, ln2_b):
    bsz, seq, d = x.shape
    depth = w_in.shape[0]
    alpha = (2 * depth) ** 0.25
    outs = []
    for bi in range(bsz):
        h = x[bi]
        for l in range(depth):
            h = _layer(h, w_in[l], b_in[l], a_ws[l], a_bs[l], a_ln_g[l], a_ln_b[l], gla_w_lr[l], gla_b_lr[l],
                       gla_gn_g[l], w_br_a[l], w_br_b[l], w_o[l], ln1_g[l], ln1_b[l], w_router[l], b_router[l],
                       w_gu[l], b_gu[l], w_down[l], b_down[l], ln2_g[l], ln2_b[l], alpha)
        outs.append(h)
    return jnp.stack(outs) if bsz > 1 else outs[0][None]
```

```python
import functools

import jax
import jax.numpy as jnp
from jax import lax
from jax.experimental import pallas as pl
from jax.experimental.pallas import tpu as pltpu

F32 = jnp.float32
BF16 = jnp.bfloat16
U32 = jnp.uint32
I32 = jnp.int32

LN_EPS = 1e-5
GLA_CHUNK = 64
GLA_GATE_TAU = 16.0
TOP_K = 4
SWIGLU_LIMIT = 7.0
SWIGLU_ALPHA = 1.702
LANES = 128
SUBLANES = 8
VMEM_LIMIT = 56 * 1024 * 1024
HI_MASK = 0xFFFF0000
DISPATCH_STRIPES = 8

TILE_IN_PROJ = (1024, 1024)
TILE_CAST_ROWS = 512
TILE_BRANCH_A_CHUNKS = 2
TILE_GLA_ROWS = 512
TILE_MERGE = (1024, 512)
TILE_OUT_PROJ = (1024, 512)
TILE_ROUTER_ROWS = 512
EXPERT_BLOCK_ROWS = 512
TILE_EXPERT_UP_COLS = 768
TILE_EXPERT_DOWN_COLS = 2048
TILE_COMBINE_ROWS = 256


def _pick(n, pref):
    t = min(pref, n)
    while n % t:
        t //= 2
    return t


def _dot(a, b):
    return jnp.dot(a, b, preferred_element_type=F32)


def _dot_t0(a, b):
    return lax.dot_general(a, b, (((0,), (0,)), ((), ())), preferred_element_type=F32)


def _dot_t1(a, b):
    return lax.dot_general(a, b, (((1,), (1,)), ((), ())), preferred_element_type=F32)


def _sigmoid(x):
    return 1.0 / (1.0 + jnp.exp(-x))


def _gelu(x):
    return 0.5 * x * (1.0 + lax.erf(x * (2.0 ** -0.5)))


def _layer_norm(x, g, b):
    mu = jnp.mean(x, axis=-1, keepdims=True)
    xc = x - mu
    var = jnp.mean(xc * xc, axis=-1, keepdims=True)
    return xc * lax.rsqrt(var + LN_EPS) * g + b


def _params(*sem):
    return pltpu.CompilerParams(dimension_semantics=sem, vmem_limit_bytes=VMEM_LIMIT)


def _cast_lr_kernel(x_ref, w_ref, b_ref, xb_ref, lr_ref):
    xb = x_ref[...].astype(BF16)
    xb_ref[...] = xb
    lr_ref[...] = _dot(xb, w_ref[...]) + b_ref[...]


def _cast_and_decay_proj(x, w, b, tm):
    m, k = x.shape
    n = w.shape[1]
    tm = _pick(m, tm)
    return pl.pallas_call(
        _cast_lr_kernel,
        grid=(m // tm,),
        in_specs=[pl.BlockSpec((tm, k), lambda i: (i, 0)),
                  pl.BlockSpec((k, n), lambda i: (0, 0)),
                  pl.BlockSpec((1, n), lambda i: (0, 0))],
        out_specs=[pl.BlockSpec((tm, k), lambda i: (i, 0)),
                   pl.BlockSpec((tm, n), lambda i: (i, 0))],
        out_shape=[jax.ShapeDtypeStruct((m, k), BF16), jax.ShapeDtypeStruct((m, n), F32)],
        compiler_params=_params("parallel"),
        name="cast_lr",
    )(x, w, b)


def _mm_stream_kernel(x_ref, wt_hbm, b_ref, o_ref, stage, work, sem, *, tn, row0):
    j, i = pl.program_id(0), pl.program_id(1)

    def w_copy(jt):
        return pltpu.make_async_copy(wt_hbm.at[pl.ds(pl.multiple_of(row0 + jt * tn, 16), tn), :], stage, sem.at[0])

    @pl.when(jnp.logical_and(j == 0, i == 0))
    def _():
        w_copy(0).start()

    @pl.when(i == 0)
    def _():
        w_copy(j).wait()
        work[...] = stage[...].astype(BF16)

        @pl.when(j + 1 < pl.num_programs(0))
        def _():
            w_copy(j + 1).start()

    o_ref[...] = (_dot_t1(x_ref[...], work[...]) + b_ref[...]).astype(o_ref.dtype)


def _matmul_bias_stream(xb, wt, b, row0, out_dtype, tm, tn):
    m, k = xb.shape
    n = b.shape[1]
    tm, tn = _pick(m, tm), _pick(n, tn)
    assert row0 % 16 == 0 and tn % 16 == 0
    return pl.pallas_call(
        functools.partial(_mm_stream_kernel, tn=tn, row0=row0),
        grid=(n // tn, m // tm),
        in_specs=[pl.BlockSpec((tm, k), lambda j, i: (i, 0)),
                  pl.BlockSpec(memory_space=pl.ANY),
                  pl.BlockSpec((1, tn), lambda j, i: (0, j))],
        out_specs=pl.BlockSpec((tm, tn), lambda j, i: (i, j)),
        out_shape=jax.ShapeDtypeStruct((m, n), out_dtype),
        scratch_shapes=[pltpu.VMEM((tn, k), F32), pltpu.VMEM((tn, k), BF16), pltpu.SemaphoreType.DMA((1,))],
        compiler_params=_params("arbitrary", "arbitrary"),
        name="in_proj",
    )(xb, wt, b)


def _merge_kernel(ya_ref, wa_ref, yb_ref, wb_ref, ga_ref, gb_ref, o_ref):
    a = _sigmoid(ga_ref[...]) * _dot(ya_ref[...], wa_ref[...])
    o_ref[...] = (a + _sigmoid(gb_ref[...]) * _dot(yb_ref[...], wb_ref[...])).astype(o_ref.dtype)


def _merge(ya, wa, yb, wb, z, off_ga, off_gb, tm, tn):
    m, ka = ya.shape
    kb = yb.shape[1]
    n = wa.shape[1]
    tm, tn = _pick(m, tm), _pick(n, tn)
    while off_ga % tn or off_gb % tn:
        tn //= 2
    oa, ob = off_ga // tn, off_gb // tn
    return pl.pallas_call(
        _merge_kernel,
        grid=(m // tm, n // tn),
        in_specs=[pl.BlockSpec((tm, ka), lambda i, j: (i, 0)),
                  pl.BlockSpec((ka, tn), lambda i, j: (0, j)),
                  pl.BlockSpec((tm, kb), lambda i, j: (i, 0)),
                  pl.BlockSpec((kb, tn), lambda i, j: (0, j)),
                  pl.BlockSpec((tm, tn), lambda i, j: (i, oa + j)),
                  pl.BlockSpec((tm, tn), lambda i, j: (i, ob + j))],
        out_specs=pl.BlockSpec((tm, tn), lambda i, j: (i, j)),
        out_shape=jax.ShapeDtypeStruct((m, n), BF16),
        compiler_params=_params("parallel", "parallel"),
        name="merge",
    )(ya, wa, yb, wb, z, z)


def _mm_resid_kernel(m_ref, w_ref, x_ref, o_ref, *, alpha):
    o_ref[...] = alpha * x_ref[...] + _dot(m_ref[...], w_ref[...])


def _out_proj_resid(mb, w, x, alpha, tm, tn):
    m, k = mb.shape
    n = w.shape[1]
    tm, tn = _pick(m, tm), _pick(n, tn)
    return pl.pallas_call(
        functools.partial(_mm_resid_kernel, alpha=alpha),
        grid=(m // tm, n // tn),
        in_specs=[pl.BlockSpec((tm, k), lambda i, j: (i, 0)),
                  pl.BlockSpec((k, tn), lambda i, j: (0, j)),
                  pl.BlockSpec((tm, tn), lambda i, j: (i, j))],
        out_specs=pl.BlockSpec((tm, tn), lambda i, j: (i, j)),
        out_shape=jax.ShapeDtypeStruct((m, n), F32),
        compiler_params=_params("parallel", "parallel"),
        name="out_proj",
    )(mb, w, x)


def _branch_a_kernel(u_ref, v_ref, ws_ref, bs_ref, g_ref, b_ref, o_ref, *, n_groups, chunk):
    tm, aw = u_ref.shape
    gd = aw // n_groups
    row = lax.broadcasted_iota(I32, (chunk, chunk), 0)
    col = lax.broadcasted_iota(I32, (chunk, chunk), 1)
    causal = row >= col
    for g in range(n_groups):
        ws = jnp.where(causal, ws_ref[g], 0.0).astype(BF16)
        cs = slice(g * gd, (g + 1) * gd)
        for c in range(tm // chunk):
            rs = slice(c * chunk, (c + 1) * chunk)
            vn = _layer_norm(_gelu(v_ref[rs, cs]), g_ref[:, cs], b_ref[:, cs])
            mixed = _dot(ws, vn.astype(BF16)) + bs_ref[:, g:g + 1]
            o_ref[rs, cs] = (_gelu(u_ref[rs, cs]) * mixed).astype(o_ref.dtype)


def _branch_a(z, a_ws, a_bs, a_ln_g, a_ln_b, aw, tm):
    t = z.shape[0]
    n_groups, chunk, _ = a_ws.shape
    tm = max(_pick(t, tm), chunk)
    return pl.pallas_call(
        functools.partial(_branch_a_kernel, n_groups=n_groups, chunk=chunk),
        grid=(t // tm,),
        in_specs=[pl.BlockSpec((tm, aw), lambda i: (i, 0)),
                  pl.BlockSpec((tm, aw), lambda i: (i, 1)),
                  pl.BlockSpec((n_groups, chunk, chunk), lambda i: (0, 0, 0)),
                  pl.BlockSpec((chunk, n_groups), lambda i: (0, 0)),
                  pl.BlockSpec((1, aw), lambda i: (0, 0)),
                  pl.BlockSpec((1, aw), lambda i: (0, 0))],
        out_specs=pl.BlockSpec((tm, aw), lambda i: (i, 0)),
        out_shape=jax.ShapeDtypeStruct((t, aw), BF16),
        compiler_params=_params("parallel"),
        name="branch_a",
    )(z, z, a_ws, a_bs.T, a_ln_g.reshape(1, aw), a_ln_b.reshape(1, aw))


def _gla_kernel(q_ref, k_ref, v_ref, r_ref, lr_ref, wlr_ref, blr_ref, gn_ref, o_ref, s_ref, *, heads, chunk):
    @pl.when(pl.program_id(0) == 0)
    def _():
        s_ref[...] = jnp.zeros_like(s_ref)

    tg, kw = q_ref.shape
    vw = v_ref.shape[1]
    dk, dv = kw // heads, vw // heads
    row = lax.broadcasted_iota(I32, (chunk, chunk), 0)
    col = lax.broadcasted_iota(I32, (chunk, chunk), 1)
    causal = row >= col
    tri = causal.astype(BF16)
    ones = jnp.ones((chunk, LANES), BF16)
    scale = dk ** -0.5

    def body(c, carry):
        rs = pl.ds(pl.multiple_of(c * chunk, chunk), chunk)
        gl = _dot(lr_ref[rs, :].astype(BF16), wlr_ref[...]) + blr_ref[...]
        la = (jnp.minimum(gl, 0.0) - jnp.log1p(jnp.exp(-jnp.abs(gl)))) * (1.0 / GLA_GATE_TAU)
        la_hi = la.astype(BF16)
        la_lo = (la - la_hi.astype(F32)).astype(BF16)
        cum = _dot(tri, la_hi) + _dot(tri, la_lo)
        tot = cum[chunk - 1:chunk, :]
        tot_col = _dot_t0(la_hi, ones) + _dot_t0(la_lo, ones)
        dec_col = jnp.exp(tot_col)
        q = q_ref[rs, :] * scale
        k = k_ref[rs, :]
        qt = (q * jnp.exp(cum)).astype(BF16)
        kt = (k * jnp.exp(-cum)).astype(BF16)
        kl = (k * jnp.exp(tot - cum)).astype(BF16)
        for h in range(heads):
            ks = slice(h * dk, (h + 1) * dk)
            vs = slice(h * dv, (h + 1) * dv)
            vh = v_ref[rs, vs].astype(BF16)
            sc = jnp.where(causal, _dot_t1(qt[:, ks], kt[:, ks]), 0.0).astype(BF16)
            state = s_ref[h]
            o = _dot(sc, vh) + _dot(qt[:, ks], state.astype(BF16))
            dec = jnp.concatenate([dec_col[ks, :]] * (dv // LANES), axis=1) if dv >= LANES else dec_col[ks, :dv]
            s_ref[h] = dec * state + _dot_t0(kl[:, ks], vh)
            on = o * lax.rsqrt(jnp.mean(o * o, axis=-1, keepdims=True) + LN_EPS) * gn_ref[:, vs]
            rr = r_ref[rs, vs]
            o_ref[rs, vs] = (on * (rr * _sigmoid(rr))).astype(o_ref.dtype)
        return carry

    lax.fori_loop(0, tg // chunk, body, 0)


def _gla(z, lr, wlr, blr, gn, kw, vw, off_q, heads, tg):
    t = z.shape[0]
    tg = max(_pick(t, tg), GLA_CHUNK)
    oq = off_q // kw
    ok = oq + 1
    ov = (off_q + 2 * kw) // vw
    orr = ov + 1
    return pl.pallas_call(
        functools.partial(_gla_kernel, heads=heads, chunk=GLA_CHUNK),
        grid=(t // tg,),
        in_specs=[pl.BlockSpec((tg, kw), lambda i: (i, oq)),
                  pl.BlockSpec((tg, kw), lambda i: (i, ok)),
                  pl.BlockSpec((tg, vw), lambda i: (i, ov)),
                  pl.BlockSpec((tg, vw), lambda i: (i, orr)),
                  pl.BlockSpec((tg, LANES), lambda i: (i, 0)),
                  pl.BlockSpec((LANES, kw), lambda i: (0, 0)),
                  pl.BlockSpec((1, kw), lambda i: (0, 0)),
                  pl.BlockSpec((1, vw), lambda i: (0, 0))],
        out_specs=pl.BlockSpec((tg, vw), lambda i: (i, 0)),
        out_shape=jax.ShapeDtypeStruct((t, vw), BF16),
        scratch_shapes=[pltpu.VMEM((heads, kw // heads, vw // heads), F32)],
        compiler_params=_params("arbitrary"),
        name="gla",
    )(z, z, z, z, lr, wlr, blr, gn)


def _ln_router_kernel(h_ref, g_ref, b_ref, wr_ref, br_ref, x1_ref, xp_ref, idx_ref, wgt_ref, rank_ref, cnt_ref,
                      base_ref, *, top_k):
    @pl.when(pl.program_id(0) == 0)
    def _():
        base_ref[...] = jnp.zeros_like(base_ref)

    tm, d = h_ref.shape
    half = d // 2
    x1 = _layer_norm(h_ref[...], g_ref[...], b_ref[...])
    x1_ref[...] = x1
    lo = pltpu.bitcast(x1[:, :half].astype(BF16).astype(F32), U32) >> 16
    hi = pltpu.bitcast(x1[:, half:].astype(BF16).astype(F32), U32) & jnp.uint32(HI_MASK)
    xp_ref[...] = lo | hi

    logits = _dot(x1.astype(BF16), wr_ref[...]) + br_ref[...]
    lane = lax.broadcasted_iota(I32, (tm, LANES), 1).astype(F32)
    member = jnp.zeros((tm, LANES), F32)
    vals, idxs = [], []
    cur = logits
    for _ in range(top_k):
        mx = jnp.max(cur, axis=-1, keepdims=True)
        ix = jnp.min(jnp.where(cur == mx, lane, float(LANES)), axis=-1, keepdims=True)
        sel = lane == ix
        vals.append(mx)
        idxs.append(ix)
        member = member + sel.astype(F32)
        cur = jnp.where(sel, -jnp.inf, cur)
    exps = [jnp.exp(v - vals[0]) for v in vals]
    denom = exps[0]
    for e in exps[1:]:
        denom = denom + e
    r_i = lax.broadcasted_iota(I32, (tm, tm), 0)
    c_i = lax.broadcasted_iota(I32, (tm, tm), 1)
    before = _dot((r_i > c_i).astype(BF16), member.astype(BF16)) + base_ref[...]
    idx_out = jnp.zeros((tm, LANES), F32)
    wgt_out = jnp.zeros((tm, LANES), F32)
    rank_out = jnp.zeros((tm, LANES), F32)
    for j in range(top_k):
        rk = jnp.sum(jnp.where(lane == idxs[j], before, 0.0), axis=-1, keepdims=True)
        at = lane == float(j)
        idx_out = jnp.where(at, idxs[j], idx_out)
        wgt_out = jnp.where(at, exps[j] / denom, wgt_out)
        rank_out = jnp.where(at, rk, rank_out)
    idx_ref[...] = idx_out.astype(I32)
    wgt_ref[...] = wgt_out
    rank_ref[...] = rank_out.astype(I32)
    base_ref[...] = base_ref[...] + jnp.sum(member, axis=0, keepdims=True)
    cnt_ref[...] = base_ref[...]


def _ln_router(h1, g, b, wr, br, tm):
    t, d = h1.shape
    tm = _pick(t, tm)
    row = lambda i: (i, 0)
    fixed = lambda i: (0, 0)
    return pl.pallas_call(
        functools.partial(_ln_router_kernel, top_k=TOP_K),
        grid=(t // tm,),
        in_specs=[pl.BlockSpec((tm, d), row),
                  pl.BlockSpec((1, d), fixed),
                  pl.BlockSpec((1, d), fixed),
                  pl.BlockSpec((d, LANES), fixed),
                  pl.BlockSpec((1, LANES), fixed)],
        out_specs=[pl.BlockSpec((tm, d), row),
                   pl.BlockSpec((tm, d // 2), row),
                   pl.BlockSpec((tm, LANES), row),
                   pl.BlockSpec((tm, LANES), row),
                   pl.BlockSpec((tm, LANES), row),
                   pl.BlockSpec((1, LANES), fixed)],
        out_shape=[jax.ShapeDtypeStruct((t, d), F32),
                   jax.ShapeDtypeStruct((t, d // 2), U32),
                   jax.ShapeDtypeStruct((t, LANES), I32),
                   jax.ShapeDtypeStruct((t, LANES), F32),
                   jax.ShapeDtypeStruct((t, LANES), I32),
                   jax.ShapeDtypeStruct((1, LANES), F32)],
        scratch_shapes=[pltpu.VMEM((1, LANES), F32)],
        compiler_params=_params("arbitrary"),
        name="ln_router",
    )(h1, g, b, wr, br)


def _dispatch_kernel(nb_ref, tok_ref, tok_nxt_ref, xp_hbm, xs_ref, buf, sems, *, tm_e):
    b = pl.program_id(0)
    slot = b % 2
    used = b < nb_ref[0]
    stripe = tm_e // DISPATCH_STRIPES

    def row_copy(idx_ref, s, u, k8, r):
        k0 = pl.multiple_of(k8 * SUBLANES, SUBLANES)
        return pltpu.make_async_copy(xp_hbm.at[pl.ds(idx_ref[u * stripe + k0 + r], 1)],
                                     buf.at[s, u, pl.ds(k0 + r, 1)], sems.at[s])

    def issue(idx_ref, s):
        def body(k8, c):
            for r in range(SUBLANES):
                for u in range(DISPATCH_STRIPES):
                    row_copy(idx_ref, s, u, k8, r).start(priority=u % 2)
            return c
        lax.fori_loop(0, stripe // SUBLANES, body, 0)

    @pl.when(b == 0)
    def _():
        issue(tok_ref, slot)

    @pl.when(b + 1 < nb_ref[0])
    def _():
        issue(tok_nxt_ref, 1 - slot)

    @pl.when(jnp.logical_not(used))
    def _():
        xs_ref[...] = jnp.zeros_like(xs_ref)

    @pl.when(used)
    def _():
        def drain(k8, c):
            for r in range(SUBLANES):
                for u in range(DISPATCH_STRIPES):
                    row_copy(tok_ref, slot, u, k8, r).wait()
            return c

        lax.fori_loop(0, stripe // SUBLANES, drain, 0)
        w = buf[slot].reshape(tm_e, buf.shape[-1])
        half = w.shape[1]
        xs_ref[:, :half] = pltpu.bitcast(w << 16, F32).astype(BF16)
        xs_ref[:, half:] = pltpu.bitcast(w & jnp.uint32(HI_MASK), F32).astype(BF16)


def _dispatch(tok_buf, nb_used, xp, tm_e):
    dh = xp.shape[1]
    n_rows = tok_buf.shape[0]
    last = n_rows // tm_e - 1
    grid_spec = pltpu.PrefetchScalarGridSpec(
        num_scalar_prefetch=1,
        grid=(n_rows // tm_e,),
        in_specs=[pl.BlockSpec((tm_e,), lambda b, nb: (b,), memory_space=pltpu.SMEM),
                  pl.BlockSpec((tm_e,), lambda b, nb: (jnp.minimum(b + 1, last),), memory_space=pltpu.SMEM),
                  pl.BlockSpec(memory_space=pl.ANY)],
        out_specs=pl.BlockSpec((tm_e, 2 * dh), lambda b, nb: (b, 0)),
        scratch_shapes=[pltpu.VMEM((2, DISPATCH_STRIPES, tm_e // DISPATCH_STRIPES, dh), U32),
                        pltpu.SemaphoreType.DMA((2,))],
    )
    return pl.pallas_call(
        functools.partial(_dispatch_kernel, tm_e=tm_e),
        grid_spec=grid_spec,
        out_shape=jax.ShapeDtypeStruct((n_rows, 2 * dh), BF16),
        compiler_params=_params("arbitrary"),
        name="dispatch",
    )(nb_used, tok_buf, tok_buf, xp)


def _expert_changed(be_ref, nb_ref):
    b = pl.program_id(1)
    fresh = jnp.logical_or(b == 0, be_ref[b] != be_ref[jnp.maximum(b - 1, 0)])
    return b < nb_ref[0], fresh


def _stream_expert_weights(be_ref, nxt_ref, wrap_ref, used, fresh, tile_copies, stage, work):
    j, b = pl.program_id(0), pl.program_id(1)
    nj = pl.num_programs(0)

    @pl.when(jnp.logical_and(j == 0, b == 0))
    def _():
        for c in tile_copies(be_ref[0], 0):
            c.start()

    @pl.when(jnp.logical_and(used, fresh))
    def _():
        for c in tile_copies(be_ref[b], j):
            c.wait()
        for src, dst in zip(stage, work):
            dst[...] = src[...].astype(BF16)
        jn = j + wrap_ref[b]

        @pl.when(jn < nj)
        def _():
            for c in tile_copies(nxt_ref[b], jn):
                c.start()


def _for_valid_rows(used, bv_ref, out_ref, compute):
    tm = out_ref.shape[0]
    half = tm // 2
    few = bv_ref[pl.program_id(1)] <= half

    @pl.when(jnp.logical_and(used, jnp.logical_not(few)))
    def _():
        compute(tm)

    @pl.when(jnp.logical_and(used, few))
    def _():
        compute(half)
        out_ref[half:, :] = jnp.zeros((tm - half, out_ref.shape[1]), out_ref.dtype)


def _moe1_kernel(be_ref, nb_ref, nxt_ref, wrap_ref, bv_ref, xs_ref, w_hbm, bg_ref, bl_ref, h_ref, sg, sl, wg_s, wl_s,
                 sems, *, tn):
    used, fresh = _expert_changed(be_ref, nb_ref)
    n_lin = pl.num_programs(0)

    def tile_copies(e, jt):
        return (pltpu.make_async_copy(w_hbm.at[e, :, pl.ds(pl.multiple_of(jt * tn, tn), tn)], sg, sems.at[0]),
                pltpu.make_async_copy(w_hbm.at[e, :, pl.ds(pl.multiple_of((n_lin + jt) * tn, tn), tn)], sl,
                                      sems.at[1]))

    _stream_expert_weights(be_ref, nxt_ref, wrap_ref, used, fresh, tile_copies, (sg, sl), (wg_s, wl_s))

    @pl.when(jnp.logical_not(used))
    def _():
        h_ref[...] = jnp.zeros_like(h_ref)

    def compute(rows):
        x = xs_ref[:rows, :]
        gate = jnp.minimum(_dot(x, wg_s[...]) + bg_ref[...], SWIGLU_LIMIT)
        lin = jnp.clip(_dot(x, wl_s[...]) + bl_ref[...], -SWIGLU_LIMIT, SWIGLU_LIMIT)
        h_ref[:rows, :] = (gate * _sigmoid(SWIGLU_ALPHA * gate) * (lin + 1.0)).astype(h_ref.dtype)

    _for_valid_rows(used, bv_ref, h_ref, compute)


def _moe1(sched, xs, w_gu, b_gu, tm, tn):
    n_rows, dh = xs.shape
    n_exp, d, de2 = w_gu.shape
    de = de2 // 2
    tn = _pick(de, tn)
    nj = de // tn

    def blk(b, nb):
        return jnp.minimum(b, nb[0] - 1)

    grid_spec = pltpu.PrefetchScalarGridSpec(
        num_scalar_prefetch=5,
        grid=(nj, n_rows // tm),
        in_specs=[pl.BlockSpec((tm, dh), lambda j, b, be, nb, *_: (blk(b, nb), 0)),
                  pl.BlockSpec(memory_space=pl.ANY),
                  pl.BlockSpec((None, 1, tn), lambda j, b, be, nb, *_: (be[blk(b, nb)], 0, j)),
                  pl.BlockSpec((None, 1, tn), lambda j, b, be, nb, *_: (be[blk(b, nb)], 0, nj + j))],
        out_specs=pl.BlockSpec((tm, tn), lambda j, b, *_: (b, j)),
        scratch_shapes=[pltpu.VMEM((d, tn), F32), pltpu.VMEM((d, tn), F32),
                        pltpu.VMEM((d, tn), BF16), pltpu.VMEM((d, tn), BF16),
                        pltpu.SemaphoreType.DMA((2,))],
    )
    return pl.pallas_call(
        functools.partial(_moe1_kernel, tn=tn),
        grid_spec=grid_spec,
        out_shape=jax.ShapeDtypeStruct((n_rows, de), BF16),
        compiler_params=_params("arbitrary", "arbitrary"),
        name="moe_up",
    )(*sched, xs, w_gu, b_gu, b_gu)


def _moe2_kernel(be_ref, nb_ref, nxt_ref, wrap_ref, bv_ref, h_ref, w_hbm, b_ref, y_ref, stage, w_s, sems, *, tn):
    used, fresh = _expert_changed(be_ref, nb_ref)

    def tile_copies(e, jt):
        return (pltpu.make_async_copy(w_hbm.at[e, :, pl.ds(pl.multiple_of(jt * tn, tn), tn)], stage, sems.at[0]),)

    _stream_expert_weights(be_ref, nxt_ref, wrap_ref, used, fresh, tile_copies, (stage,), (w_s,))

    @pl.when(jnp.logical_not(used))
    def _():
        y_ref[...] = jnp.zeros_like(y_ref)

    def compute(rows):
        y_ref[:rows, :] = _dot(h_ref[:rows, :], w_s[...]) + b_ref[...]

    _for_valid_rows(used, bv_ref, y_ref, compute)


def _moe2(sched, h, w_down, b_down, tm, tn):
    n_rows, de = h.shape
    n_exp, _, d = w_down.shape
    tn = _pick(d, tn)

    def blk(b, nb):
        return jnp.minimum(b, nb[0] - 1)

    grid_spec = pltpu.PrefetchScalarGridSpec(
        num_scalar_prefetch=5,
        grid=(d // tn, n_rows // tm),
        in_specs=[pl.BlockSpec((tm, de), lambda j, b, be, nb, *_: (blk(b, nb), 0)),
                  pl.BlockSpec(memory_space=pl.ANY),
                  pl.BlockSpec((None, 1, tn), lambda j, b, be, nb, *_: (be[blk(b, nb)], 0, j))],
        out_specs=pl.BlockSpec((tm, tn), lambda j, b, *_: (b, j)),
        scratch_shapes=[pltpu.VMEM((de, tn), F32), pltpu.VMEM((de, tn), BF16), pltpu.SemaphoreType.DMA((1,))],
    )
    return pl.pallas_call(
        functools.partial(_moe2_kernel, tn=tn),
        grid_spec=grid_spec,
        out_shape=jax.ShapeDtypeStruct((n_rows, d), F32),
        compiler_params=_params("arbitrary", "arbitrary"),
        name="moe_down",
    )(*sched, h, w_down, b_down)


def _combine_kernel(dest_ref, dest_nxt_ref, ys_hbm, x1_ref, wgt_ref, g_ref, b_ref, o_ref, buf, sems, *, tc, top_k,
                    alpha):
    i = pl.program_id(0)
    slot = i % 2

    def row_copy(idx_ref, s, t8, r, j):
        t0 = pl.multiple_of(t8 * SUBLANES, SUBLANES)
        return pltpu.make_async_copy(ys_hbm.at[pl.ds(idx_ref[(t0 + r) * top_k + j], 1)],
                                     buf.at[s, j, pl.ds(t0 + r, 1)], sems.at[s])

    def issue(idx_ref, s):
        def body(t8, c):
            for r in range(SUBLANES):
                for j in range(top_k):
                    row_copy(idx_ref, s, t8, r, j).start(priority=j % 2)
            return c
        lax.fori_loop(0, tc // SUBLANES, body, 0)

    @pl.when(i == 0)
    def _():
        issue(dest_ref, slot)

    @pl.when(i + 1 < pl.num_programs(0))
    def _():
        issue(dest_nxt_ref, 1 - slot)

    def drain(t8, c):
        for r in range(SUBLANES):
            for j in range(top_k):
                row_copy(dest_ref, slot, t8, r, j).wait()
        return c

    lax.fori_loop(0, tc // SUBLANES, drain, 0)
    y = wgt_ref[:, 0:1] * buf[slot, 0]
    for j in range(1, top_k):
        y = y + wgt_ref[:, j:j + 1] * buf[slot, j]
    o_ref[...] = _layer_norm(alpha * x1_ref[...] + y, g_ref[...], b_ref[...])


def _combine(dest_flat, ys, x1, wgt, g, b, alpha, tc):
    t, d = x1.shape
    tc = _pick(t, tc)
    last = t // tc - 1
    return pl.pallas_call(
        functools.partial(_combine_kernel, tc=tc, top_k=TOP_K, alpha=alpha),
        grid=(t // tc,),
        in_specs=[pl.BlockSpec((tc * TOP_K,), lambda i: (i,), memory_space=pltpu.SMEM),
                  pl.BlockSpec((tc * TOP_K,), lambda i: (jnp.minimum(i + 1, last),), memory_space=pltpu.SMEM),
                  pl.BlockSpec(memory_space=pl.ANY),
                  pl.BlockSpec((tc, d), lambda i: (i, 0)),
                  pl.BlockSpec((tc, LANES), lambda i: (i, 0)),
                  pl.BlockSpec((1, d), lambda i: (0, 0)),
                  pl.BlockSpec((1, d), lambda i: (0, 0))],
        out_specs=pl.BlockSpec((tc, d), lambda i: (i, 0)),
        out_shape=jax.ShapeDtypeStruct((t, d), F32),
        scratch_shapes=[pltpu.VMEM((2, TOP_K, tc, d), F32), pltpu.SemaphoreType.DMA((2,))],
        compiler_params=_params("arbitrary"),
        name="combine",
    )(dest_flat, dest_flat, ys, x1, wgt, g, b)


def _layer(x2d, w_in, b_in, a_ws, a_bs, a_ln_g, a_ln_b, gla_w_lr, gla_b_lr, gla_gn_g, w_br_a, w_br_b, w_o,
           ln1_g, ln1_b, w_router, b_router, w_gu, b_gu, w_down, b_down, ln2_g, ln2_b, alpha):
    t, d = x2d.shape
    n_groups, a_chunk, _ = a_ws.shape
    aw = a_ln_g.size
    rank, kw = gla_w_lr.shape
    heads, dv = gla_gn_g.shape
    vw = heads * dv
    n_exp = w_router.shape[1]
    de = w_down.shape[1]
    off_lr = 2 * aw + 2 * kw + 2 * vw
    off_q = 2 * aw

    w_in_t = w_in.T
    w_lr_in = jnp.pad(w_in[:, off_lr:off_lr + rank], ((0, 0), (0, LANES - rank))).astype(BF16)
    b_lr_in = jnp.pad(b_in[off_lr:off_lr + rank], (0, LANES - rank)).reshape(1, LANES)
    wlr = jnp.pad(gla_w_lr, ((0, LANES - rank), (0, 0))).astype(BF16)
    wr = jnp.pad(w_router, ((0, 0), (0, LANES - n_exp))).astype(BF16)
    br = jnp.pad(b_router, (0, LANES - n_exp), constant_values=-jnp.inf).reshape(1, LANES)

    xb, lr = _cast_and_decay_proj(x2d, w_lr_in, b_lr_in, TILE_CAST_ROWS)
    z = _matmul_bias_stream(xb, w_in_t, b_in[:off_lr].reshape(1, -1), 0, F32, *TILE_IN_PROJ)
    zg = _matmul_bias_stream(xb, w_in_t, b_in[off_lr + rank:].reshape(1, -1), off_lr + rank, F32, *TILE_IN_PROJ)

    ya = _branch_a(z, a_ws, a_bs, a_ln_g, a_ln_b, aw, TILE_BRANCH_A_CHUNKS * a_chunk)
    yb = _gla(z, lr, wlr, gla_b_lr.reshape(1, kw), gla_gn_g.reshape(1, vw), kw, vw, off_q, heads,
              TILE_GLA_ROWS)
    merged = _merge(ya, w_br_a.astype(BF16), yb, w_br_b.astype(BF16), zg, 0, d, *TILE_MERGE)
    h1 = _out_proj_resid(merged, w_o.astype(BF16), x2d, alpha, *TILE_OUT_PROJ)

    x1, xp, idx, wgt, rnk, cnt = _ln_router(h1, ln1_g.reshape(1, d), ln1_b.reshape(1, d), wr, br, TILE_ROUTER_ROWS)

    n_assign = t * TOP_K
    tm_e = min(EXPERT_BLOCK_ROWS, max(n_assign // n_exp // 2, 8 * DISPATCH_STRIPES))
    n_blocks = -(-(n_assign + n_exp * (tm_e - 1)) // tm_e)
    n_rows = n_blocks * tm_e
    counts = cnt[0, :n_exp].astype(I32)
    padded = (counts + tm_e - 1) // tm_e * tm_e
    pend = jnp.cumsum(padded)
    pstart = pend - padded
    sel = idx[:, :TOP_K, None] == jnp.arange(n_exp, dtype=I32)[None, None, :]
    dest = jnp.sum(jnp.where(sel, pstart[None, None, :], 0), axis=-1) + rnk[:, :TOP_K]
    dest_flat = dest.reshape(-1).astype(I32)
    nb_used = (pend[-1] // tm_e).astype(I32).reshape(1)
    block_start = jnp.arange(n_blocks, dtype=I32) * tm_e
    block_e = jnp.minimum(jnp.sum((pend[None, :] <= block_start[:, None]).astype(I32), axis=1), n_exp - 1)

    tok_buf = jnp.zeros((n_rows,), I32).at[dest_flat].set(jnp.arange(n_assign, dtype=I32) // TOP_K,
                                                          unique_indices=True)
    xs = _dispatch(tok_buf, nb_used, xp, tm_e)
    experts = jnp.arange(n_exp, dtype=I32)
    later = jnp.where((padded > 0)[None, :] & (experts[None, :] > block_e[:, None]), experts[None, :], n_exp)
    nxt = jnp.min(later, axis=1)
    wrap = (nxt == n_exp).astype(I32)
    nxt_e = jnp.where(nxt == n_exp, block_e[0], nxt).astype(I32)
    mine = experts[None, :] == block_e[:, None]
    group_end = jnp.sum(jnp.where(mine, (pstart + counts)[None, :], 0), axis=1)
    bvalid = jnp.clip(group_end - block_start, 0, tm_e).astype(I32)
    sched = (block_e, nb_used, nxt_e, wrap, bvalid)
    hmid = _moe1(sched, xs, w_gu, b_gu.reshape(n_exp, 1, 2 * de), tm_e, TILE_EXPERT_UP_COLS)
    ys = _moe2(sched, hmid, w_down, b_down.reshape(n_exp, 1, d), tm_e, TILE_EXPERT_DOWN_COLS)
    return _combine(dest_flat, ys, x1, wgt, ln2_g.reshape(1, d), ln2_b.reshape(1, d), alpha, TILE_COMBINE_ROWS)


def kernel(x, w_in, b_in, a_ws, a_bs, a_ln_g, a_ln_b, gla_w_lr, gla_b_lr, gla_gn_g, w_br_a, w_br_b, w_o, ln1_g, ln1_b, w_router, b_router, w_gu, b_gu, w_down, b_down, ln2_g, ln2_b):
    bsz, seq, d = x.shape
    depth = w_in.shape[0]
    alpha = (2 * depth) ** 0.25
    outs = []
    for bi in range(bsz):
        h = x[bi]
        for l in range(depth):
            h = _layer(h, w_in[l], b_in[l], a_ws[l], a_bs[l], a_ln_g[l], a_ln_b[l], gla_w_lr[l], gla_b_lr[l],
                       gla_gn_g[l], w_br_a[l], w_br_b[l], w_o[l], ln1_g[l], ln1_b[l], w_router[l], b_router[l],
                       w_gu[l], b_gu[l], w_down[l], b_down[l], ln2_g[l], ln2_b[l], alpha)
        outs.append(h)
    return jnp.stack(outs) if bsz > 1 else outs[0][None]
```

```python
import functools

import jax
import jax.numpy as jnp
from jax import lax
from jax.experimental import pallas as pl
from jax.experimental.pallas import tpu as pltpu

F32 = jnp.float32
BF16 = jnp.bfloat16
U32 = jnp.uint32
I32 = jnp.int32

LN_EPS = 1e-5
GLA_CHUNK = 64
GLA_GATE_TAU = 16.0
TOP_K = 4
SWIGLU_LIMIT = 7.0
SWIGLU_ALPHA = 1.702
LANES = 128
SUBLANES = 8
VMEM_LIMIT = 56 * 1024 * 1024
HI_MASK = 0xFFFF0000
DISPATCH_STRIPES = 8

TILE_IN_PROJ = (1024, 1024)
TILE_CAST_ROWS = 512
TILE_BRANCH_A_CHUNKS = 2
TILE_GLA_ROWS = 512
TILE_MERGE = (1024, 512)
TILE_OUT_PROJ = (1024, 512)
TILE_ROUTER_ROWS = 512
EXPERT_BLOCK_ROWS = 512
TILE_EXPERT_UP_COLS = 768
TILE_EXPERT_DOWN_COLS = 2048
TILE_COMBINE_ROWS = 256


def _pick(n, pref):
    t = min(pref, n)
    while n % t:
        t //= 2
    return t


def _dot(a, b):
    return jnp.dot(a, b, preferred_element_type=F32)


def _dot_t0(a, b):
    return lax.dot_general(a, b, (((0,), (0,)), ((), ())), preferred_element_type=F32)


def _dot_t1(a, b):
    return lax.dot_general(a, b, (((1,), (1,)), ((), ())), preferred_element_type=F32)


def _sigmoid(x):
    return 1.0 / (1.0 + jnp.exp(-x))


def _gelu(x):
    return 0.5 * x * (1.0 + lax.erf(x * (2.0 ** -0.5)))


def _layer_norm(x, g, b):
    mu = jnp.mean(x, axis=-1, keepdims=True)
    xc = x - mu
    var = jnp.mean(xc * xc, axis=-1, keepdims=True)
    return xc * lax.rsqrt(var + LN_EPS) * g + b


def _params(*sem):
    return pltpu.CompilerParams(dimension_semantics=sem, vmem_limit_bytes=VMEM_LIMIT)


def _cast_lr_kernel(x_ref, w_ref, b_ref, xb_ref, lr_ref):
    xb = x_ref[...].astype(BF16)
    xb_ref[...] = xb
    lr_ref[...] = _dot(xb, w_ref[...]) + b_ref[...]


def _cast_and_decay_proj(x, w, b, tm):
    m, k = x.shape
    n = w.shape[1]
    tm = _pick(m, tm)
    return pl.pallas_call(
        _cast_lr_kernel,
        grid=(m // tm,),
        in_specs=[pl.BlockSpec((tm, k), lambda i: (i, 0)),
                  pl.BlockSpec((k, n), lambda i: (0, 0)),
                  pl.BlockSpec((1, n), lambda i: (0, 0))],
        out_specs=[pl.BlockSpec((tm, k), lambda i: (i, 0)),
                   pl.BlockSpec((tm, n), lambda i: (i, 0))],
        out_shape=[jax.ShapeDtypeStruct((m, k), BF16), jax.ShapeDtypeStruct((m, n), F32)],
        compiler_params=_params("parallel"),
        name="cast_lr",
    )(x, w, b)


def _mm_stream_kernel(x_ref, wt_hbm, b_ref, o_ref, stage, work, sem, *, tn, row0):
    j, i = pl.program_id(0), pl.program_id(1)

    def w_copy(jt):
        return pltpu.make_async_copy(wt_hbm.at[pl.ds(pl.multiple_of(row0 + jt * tn, 16), tn), :], stage, sem.at[0])

    @pl.when(jnp.logical_and(j == 0, i == 0))
    def _():
        w_copy(0).start()

    @pl.when(i == 0)
    def _():
        w_copy(j).wait()
        work[...] = stage[...].astype(BF16)

        @pl.when(j + 1 < pl.num_programs(0))
        def _():
            w_copy(j + 1).start()

    o_ref[...] = (_dot_t1(x_ref[...], work[...]) + b_ref[...]).astype(o_ref.dtype)


def _matmul_bias_stream(xb, wt, b, row0, out_dtype, tm, tn):
    m, k = xb.shape
    n = b.shape[1]
    tm, tn = _pick(m, tm), _pick(n, tn)
    assert row0 % 16 == 0 and tn % 16 == 0
    return pl.pallas_call(
        functools.partial(_mm_stream_kernel, tn=tn, row0=row0),
        grid=(n // tn, m // tm),
        in_specs=[pl.BlockSpec((tm, k), lambda j, i: (i, 0)),
                  pl.BlockSpec(memory_space=pl.ANY),
                  pl.BlockSpec((1, tn), lambda j, i: (0, j))],
        out_specs=pl.BlockSpec((tm, tn), lambda j, i: (i, j)),
        out_shape=jax.ShapeDtypeStruct((m, n), out_dtype),
        scratch_shapes=[pltpu.VMEM((tn, k), F32), pltpu.VMEM((tn, k), BF16), pltpu.SemaphoreType.DMA((1,))],
        compiler_params=_params("arbitrary", "arbitrary"),
        name="in_proj",
    )(xb, wt, b)


def _merge_kernel(ya_ref, wa_ref, yb_ref, wb_ref, ga_ref, gb_ref, o_ref):
    a = _sigmoid(ga_ref[...]) * _dot(ya_ref[...], wa_ref[...])
    o_ref[...] = (a + _sigmoid(gb_ref[...]) * _dot(yb_ref[...], wb_ref[...])).astype(o_ref.dtype)


def _merge(ya, wa, yb, wb, z, off_ga, off_gb, tm, tn):
    m, ka = ya.shape
    kb = yb.shape[1]
    n = wa.shape[1]
    tm, tn = _pick(m, tm), _pick(n, tn)
    while off_ga % tn or off_gb % tn:
        tn //= 2
    oa, ob = off_ga // tn, off_gb // tn
    return pl.pallas_call(
        _merge_kernel,
        grid=(m // tm, n // tn),
        in_specs=[pl.BlockSpec((tm, ka), lambda i, j: (i, 0)),
                  pl.BlockSpec((ka, tn), lambda i, j: (0, j)),
                  pl.BlockSpec((tm, kb), lambda i, j: (i, 0)),
                  pl.BlockSpec((kb, tn), lambda i, j: (0, j)),
                  pl.BlockSpec((tm, tn), lambda i, j: (i, oa + j)),
                  pl.BlockSpec((tm, tn), lambda i, j: (i, ob + j))],
        out_specs=pl.BlockSpec((tm, tn), lambda i, j: (i, j)),
        out_shape=jax.ShapeDtypeStruct((m, n), BF16),
        compiler_params=_params("parallel", "parallel"),
        name="merge",
    )(ya, wa, yb, wb, z, z)


def _mm_resid_kernel(m_ref, w_ref, x_ref, o_ref, *, alpha):
    o_ref[...] = alpha * x_ref[...] + _dot(m_ref[...], w_ref[...])


def _out_proj_resid(mb, w, x, alpha, tm, tn):
    m, k = mb.shape
    n = w.shape[1]
    tm, tn = _pick(m, tm), _pick(n, tn)
    return pl.pallas_call(
        functools.partial(_mm_resid_kernel, alpha=alpha),
        grid=(m // tm, n // tn),
        in_specs=[pl.BlockSpec((tm, k), lambda i, j: (i, 0)),
                  pl.BlockSpec((k, tn), lambda i, j: (0, j)),
                  pl.BlockSpec((tm, tn), lambda i, j: (i, j))],
        out_specs=pl.BlockSpec((tm, tn), lambda i, j: (i, j)),
        out_shape=jax.ShapeDtypeStruct((m, n), F32),
        compiler_params=_params("parallel", "parallel"),
        name="out_proj",
    )(mb, w, x)


def _branch_a_kernel(u_ref, v_ref, ws_ref, bs_ref, g_ref, b_ref, o_ref, *, n_groups, chunk):
    tm, aw = u_ref.shape
    gd = aw // n_groups
    row = lax.broadcasted_iota(I32, (chunk, chunk), 0)
    col = lax.broadcasted_iota(I32, (chunk, chunk), 1)
    causal = row >= col
    for g in range(n_groups):
        ws = jnp.where(causal, ws_ref[g], 0.0).astype(BF16)
        cs = slice(g * gd, (g + 1) * gd)
        for c in range(tm // chunk):
            rs = slice(c * chunk, (c + 1) * chunk)
            vn = _layer_norm(_gelu(v_ref[rs, cs]), g_ref[:, cs], b_ref[:, cs])
            mixed = _dot(ws, vn.astype(BF16)) + bs_ref[:, g:g + 1]
            o_ref[rs, cs] = (_gelu(u_ref[rs, cs]) * mixed).astype(o_ref.dtype)


def _branch_a(z, a_ws, a_bs, a_ln_g, a_ln_b, aw, tm):
    t = z.shape[0]
    n_groups, chunk, _ = a_ws.shape
    tm = max(_pick(t, tm), chunk)
    return pl.pallas_call(
        functools.partial(_branch_a_kernel, n_groups=n_groups, chunk=chunk),
        grid=(t // tm,),
        in_specs=[pl.BlockSpec((tm, aw), lambda i: (i, 0)),
                  pl.BlockSpec((tm, aw), lambda i: (i, 1)),
                  pl.BlockSpec((n_groups, chunk, chunk), lambda i: (0, 0, 0)),
                  pl.BlockSpec((chunk, n_groups), lambda i: (0, 0)),
                  pl.BlockSpec((1, aw), lambda i: (0, 0)),
                  pl.BlockSpec((1, aw), lambda i: (0, 0))],
        out_specs=pl.BlockSpec((tm, aw), lambda i: (i, 0)),
        out_shape=jax.ShapeDtypeStruct((t, aw), BF16),
        compiler_params=_params("parallel"),
        name="branch_a",
    )(z, z, a_ws, a_bs.T, a_ln_g.reshape(1, aw), a_ln_b.reshape(1, aw))


def _gla_kernel(q_ref, k_ref, v_ref, r_ref, lr_ref, wlr_ref, blr_ref, gn_ref, o_ref, s_ref, *, heads, chunk):
    @pl.when(pl.program_id(0) == 0)
    def _():
        s_ref[...] = jnp.zeros_like(s_ref)

    tg, kw = q_ref.shape
    vw = v_ref.shape[1]
    dk, dv = kw // heads, vw // heads
    row = lax.broadcasted_iota(I32, (chunk, chunk), 0)
    col = lax.broadcasted_iota(I32, (chunk, chunk), 1)
    causal = row >= col
    tri = causal.astype(BF16)
    ones = jnp.ones((chunk, LANES), BF16)
    scale = dk ** -0.5

    def body(c, carry):
        rs = pl.ds(pl.multiple_of(c * chunk, chunk), chunk)
        gl = _dot(lr_ref[rs, :].astype(BF16), wlr_ref[...]) + blr_ref[...]
        la = (jnp.minimum(gl, 0.0) - jnp.log1p(jnp.exp(-jnp.abs(gl)))) * (1.0 / GLA_GATE_TAU)
        la_hi = la.astype(BF16)
        la_lo = (la - la_hi.astype(F32)).astype(BF16)
        cum = _dot(tri, la_hi) + _dot(tri, la_lo)
        tot = cum[chunk - 1:chunk, :]
        tot_col = _dot_t0(la_hi, ones) + _dot_t0(la_lo, ones)
        dec_col = jnp.exp(tot_col)
        q = q_ref[rs, :] * scale
        k = k_ref[rs, :]
        qt = (q * jnp.exp(cum)).astype(BF16)
        kt = (k * jnp.exp(-cum)).astype(BF16)
        kl = (k * jnp.exp(tot - cum)).astype(BF16)
        for h in range(heads):
            ks = slice(h * dk, (h + 1) * dk)
            vs = slice(h * dv, (h + 1) * dv)
            vh = v_ref[rs, vs].astype(BF16)
            sc = jnp.where(causal, _dot_t1(qt[:, ks], kt[:, ks]), 0.0).astype(BF16)
            state = s_ref[h]
            o = _dot(sc, vh) + _dot(qt[:, ks], state.astype(BF16))
            dec = jnp.concatenate([dec_col[ks, :]] * (dv // LANES), axis=1) if dv >= LANES else dec_col[ks, :dv]
            s_ref[h] = dec * state + _dot_t0(kl[:, ks], vh)
            on = o * lax.rsqrt(jnp.mean(o * o, axis=-1, keepdims=True) + LN_EPS) * gn_ref[:, vs]
            rr = r_ref[rs, vs]
            o_ref[rs, vs] = (on * (rr * _sigmoid(rr))).astype(o_ref.dtype)
        return carry

    lax.fori_loop(0, tg // chunk, body, 0)


def _gla(z, lr, wlr, blr, gn, kw, vw, off_q, heads, tg):
    t = z.shape[0]
    tg = max(_pick(t, tg), GLA_CHUNK)
    oq = off_q // kw
    ok = oq + 1
    ov = (off_q + 2 * kw) // vw
    orr = ov + 1
    return pl.pallas_call(
        functools.partial(_gla_kernel, heads=heads, chunk=GLA_CHUNK),
        grid=(t // tg,),
        in_specs=[pl.BlockSpec((tg, kw), lambda i: (i, oq)),
                  pl.BlockSpec((tg, kw), lambda i: (i, ok)),
                  pl.BlockSpec((tg, vw), lambda i: (i, ov)),
                  pl.BlockSpec((tg, vw), lambda i: (i, orr)),
                  pl.BlockSpec((tg, LANES), lambda i: (i, 0)),
                  pl.BlockSpec((LANES, kw), lambda i: (0, 0)),
                  pl.BlockSpec((1, kw), lambda i: (0, 0)),
                  pl.BlockSpec((1, vw), lambda i: (0, 0))],
        out_specs=pl.BlockSpec((tg, vw), lambda i: (i, 0)),
        out_shape=jax.ShapeDtypeStruct((t, vw), BF16),
        scratch_shapes=[pltpu.VMEM((heads, kw // heads, vw // heads), F32)],
        compiler_params=_params("arbitrary"),
        name="gla",
    )(z, z, z, z, lr, wlr, blr, gn)


def _ln_router_kernel(h_ref, g_ref, b_ref, wr_ref, br_ref, x1_ref, xp_ref, idx_ref, wgt_ref, rank_ref, cnt_ref,
                      base_ref, *, top_k):
    @pl.when(pl.program_id(0) == 0)
    def _():
        base_ref[...] = jnp.zeros_like(base_ref)

    tm, d = h_ref.shape
    half = d // 2
    x1 = _layer_norm(h_ref[...], g_ref[...], b_ref[...])
    x1_ref[...] = x1
    lo = pltpu.bitcast(x1[:, :half].astype(BF16).astype(F32), U32) >> 16
    hi = pltpu.bitcast(x1[:, half:].astype(BF16).astype(F32), U32) & jnp.uint32(HI_MASK)
    xp_ref[...] = lo | hi

    logits = _dot(x1.astype(BF16), wr_ref[...]) + br_ref[...]
    lane = lax.broadcasted_iota(I32, (tm, LANES), 1).astype(F32)
    member = jnp.zeros((tm, LANES), F32)
    vals, idxs = [], []
    cur = logits
    for _ in range(top_k):
        mx = jnp.max(cur, axis=-1, keepdims=True)
        ix = jnp.min(jnp.where(cur == mx, lane, float(LANES)), axis=-1, keepdims=True)
        sel = lane == ix
        vals.append(mx)
        idxs.append(ix)
        member = member + sel.astype(F32)
        cur = jnp.where(sel, -jnp.inf, cur)
    exps = [jnp.exp(v - vals[0]) for v in vals]
    denom = exps[0]
    for e in exps[1:]:
        denom = denom + e
    r_i = lax.broadcasted_iota(I32, (tm, tm), 0)
    c_i = lax.broadcasted_iota(I32, (tm, tm), 1)
    before = _dot((r_i > c_i).astype(BF16), member.astype(BF16)) + base_ref[...]
    idx_out = jnp.zeros((tm, LANES), F32)
    wgt_out = jnp.zeros((tm, LANES), F32)
    rank_out = jnp.zeros((tm, LANES), F32)
    for j in range(top_k):
        rk = jnp.sum(jnp.where(lane == idxs[j], before, 0.0), axis=-1, keepdims=True)
        at = lane == float(j)
        idx_out = jnp.where(at, idxs[j], idx_out)
        wgt_out = jnp.where(at, exps[j] / denom, wgt_out)
        rank_out = jnp.where(at, rk, rank_out)
    idx_ref[...] = idx_out.astype(I32)
    wgt_ref[...] = wgt_out
    rank_ref[...] = rank_out.astype(I32)
    base_ref[...] = base_ref[...] + jnp.sum(member, axis=0, keepdims=True)
    cnt_ref[...] = base_ref[...]


def _ln_router(h1, g, b, wr, br, tm):
    t, d = h1.shape
    tm = _pick(t, tm)
    row = lambda i: (i, 0)
    fixed = lambda i: (0, 0)
    return pl.pallas_call(
        functools.partial(_ln_router_kernel, top_k=TOP_K),
        grid=(t // tm,),
        in_specs=[pl.BlockSpec((tm, d), row),
                  pl.BlockSpec((1, d), fixed),
                  pl.BlockSpec((1, d), fixed),
                  pl.BlockSpec((d, LANES), fixed),
                  pl.BlockSpec((1, LANES), fixed)],
        out_specs=[pl.BlockSpec((tm, d), row),
                   pl.BlockSpec((tm, d // 2), row),
                   pl.BlockSpec((tm, LANES), row),
                   pl.BlockSpec((tm, LANES), row),
                   pl.BlockSpec((tm, LANES), row),
                   pl.BlockSpec((1, LANES), fixed)],
        out_shape=[jax.ShapeDtypeStruct((t, d), F32),
                   jax.ShapeDtypeStruct((t, d // 2), U32),
                   jax.ShapeDtypeStruct((t, LANES), I32),
                   jax.ShapeDtypeStruct((t, LANES), F32),
                   jax.ShapeDtypeStruct((t, LANES), I32),
                   jax.ShapeDtypeStruct((1, LANES), F32)],
        scratch_shapes=[pltpu.VMEM((1, LANES), F32)],
        compiler_params=_params("arbitrary"),
        name="ln_router",
    )(h1, g, b, wr, br)


def _invert_kernel(dest_ref, tok_ref, *, tile, top_k):
    i = pl.program_id(0)

    @pl.when(i == 0)
    def _():
        def clear(r, c):
            tok_ref[r] = 0
            return c
        lax.fori_loop(0, tok_ref.shape[0], clear, 0, unroll=8)

    def place(a, c):
        tok_ref[dest_ref[a]] = (i * tile + a) // top_k
        return c

    lax.fori_loop(0, tile, place, 0, unroll=8)


def _invert_dest(dest_flat, n_rows, tile):
    n_assign = dest_flat.shape[0]
    tile = _pick(n_assign, tile)
    return pl.pallas_call(
        functools.partial(_invert_kernel, tile=tile, top_k=TOP_K),
        grid=(n_assign // tile,),
        in_specs=[pl.BlockSpec((tile,), lambda i: (i,), memory_space=pltpu.SMEM)],
        out_specs=pl.BlockSpec((n_rows,), lambda i: (0,), memory_space=pltpu.SMEM),
        out_shape=jax.ShapeDtypeStruct((n_rows,), I32),
        compiler_params=_params("arbitrary"),
        name="invert_dest",
    )(dest_flat)


def _dispatch_kernel(nb_ref, tok_ref, tok_nxt_ref, xp_hbm, xs_ref, buf, sems, *, tm_e):
    b = pl.program_id(0)
    slot = b % 2
    used = b < nb_ref[0]
    stripe = tm_e // DISPATCH_STRIPES

    def row_copy(idx_ref, s, u, k8, r):
        k0 = pl.multiple_of(k8 * SUBLANES, SUBLANES)
        return pltpu.make_async_copy(xp_hbm.at[pl.ds(idx_ref[u * stripe + k0 + r], 1)],
                                     buf.at[s, u, pl.ds(k0 + r, 1)], sems.at[s])

    def issue(idx_ref, s):
        def body(k8, c):
            for r in range(SUBLANES):
                for u in range(DISPATCH_STRIPES):
                    row_copy(idx_ref, s, u, k8, r).start(priority=u % 2)
            return c
        lax.fori_loop(0, stripe // SUBLANES, body, 0)

    @pl.when(b == 0)
    def _():
        issue(tok_ref, slot)

    @pl.when(b + 1 < nb_ref[0])
    def _():
        issue(tok_nxt_ref, 1 - slot)

    @pl.when(jnp.logical_not(used))
    def _():
        xs_ref[...] = jnp.zeros_like(xs_ref)

    @pl.when(used)
    def _():
        def drain(k8, c):
            for r in range(SUBLANES):
                for u in range(DISPATCH_STRIPES):
                    row_copy(tok_ref, slot, u, k8, r).wait()
            return c

        lax.fori_loop(0, stripe // SUBLANES, drain, 0)
        w = buf[slot].reshape(tm_e, buf.shape[-1])
        half = w.shape[1]
        xs_ref[:, :half] = pltpu.bitcast(w << 16, F32).astype(BF16)
        xs_ref[:, half:] = pltpu.bitcast(w & jnp.uint32(HI_MASK), F32).astype(BF16)


def _dispatch(tok_buf, nb_used, xp, tm_e):
    dh = xp.shape[1]
    n_rows = tok_buf.shape[0]
    last = n_rows // tm_e - 1
    grid_spec = pltpu.PrefetchScalarGridSpec(
        num_scalar_prefetch=1,
        grid=(n_rows // tm_e,),
        in_specs=[pl.BlockSpec((tm_e,), lambda b, nb: (b,), memory_space=pltpu.SMEM),
                  pl.BlockSpec((tm_e,), lambda b, nb: (jnp.minimum(b + 1, last),), memory_space=pltpu.SMEM),
                  pl.BlockSpec(memory_space=pl.ANY)],
        out_specs=pl.BlockSpec((tm_e, 2 * dh), lambda b, nb: (b, 0)),
        scratch_shapes=[pltpu.VMEM((2, DISPATCH_STRIPES, tm_e // DISPATCH_STRIPES, dh), U32),
                        pltpu.SemaphoreType.DMA((2,))],
    )
    return pl.pallas_call(
        functools.partial(_dispatch_kernel, tm_e=tm_e),
        grid_spec=grid_spec,
        out_shape=jax.ShapeDtypeStruct((n_rows, 2 * dh), BF16),
        compiler_params=_params("arbitrary"),
        name="dispatch",
    )(nb_used, tok_buf, tok_buf, xp)


def _expert_changed(be_ref, nb_ref):
    b = pl.program_id(1)
    fresh = jnp.logical_or(b == 0, be_ref[b] != be_ref[jnp.maximum(b - 1, 0)])
    return b < nb_ref[0], fresh


def _stream_expert_weights(be_ref, nxt_ref, wrap_ref, used, fresh, tile_copies, stage, work):
    j, b = pl.program_id(0), pl.program_id(1)
    nj = pl.num_programs(0)

    @pl.when(jnp.logical_and(j == 0, b == 0))
    def _():
        for c in tile_copies(be_ref[0], 0):
            c.start()

    @pl.when(jnp.logical_and(used, fresh))
    def _():
        for c in tile_copies(be_ref[b], j):
            c.wait()
        for src, dst in zip(stage, work):
            dst[...] = src[...].astype(BF16)
        jn = j + wrap_ref[b]

        @pl.when(jn < nj)
        def _():
            for c in tile_copies(nxt_ref[b], jn):
                c.start()


def _for_valid_rows(used, bv_ref, out_ref, compute):
    tm = out_ref.shape[0]
    half = tm // 2
    few = bv_ref[pl.program_id(1)] <= half

    @pl.when(jnp.logical_and(used, jnp.logical_not(few)))
    def _():
        compute(tm)

    @pl.when(jnp.logical_and(used, few))
    def _():
        compute(half)
        out_ref[half:, :] = jnp.zeros((tm - half, out_ref.shape[1]), out_ref.dtype)


def _moe1_kernel(be_ref, nb_ref, nxt_ref, wrap_ref, bv_ref, xs_ref, w_hbm, bg_ref, bl_ref, h_ref, sg, sl, wg_s, wl_s,
                 sems, *, tn):
    used, fresh = _expert_changed(be_ref, nb_ref)
    n_lin = pl.num_programs(0)

    def tile_copies(e, jt):
        return (pltpu.make_async_copy(w_hbm.at[e, :, pl.ds(pl.multiple_of(jt * tn, tn), tn)], sg, sems.at[0]),
                pltpu.make_async_copy(w_hbm.at[e, :, pl.ds(pl.multiple_of((n_lin + jt) * tn, tn), tn)], sl,
                                      sems.at[1]))

    _stream_expert_weights(be_ref, nxt_ref, wrap_ref, used, fresh, tile_copies, (sg, sl), (wg_s, wl_s))

    @pl.when(jnp.logical_not(used))
    def _():
        h_ref[...] = jnp.zeros_like(h_ref)

    def compute(rows):
        x = xs_ref[:rows, :]
        gate = jnp.minimum(_dot(x, wg_s[...]) + bg_ref[...], SWIGLU_LIMIT)
        lin = jnp.clip(_dot(x, wl_s[...]) + bl_ref[...], -SWIGLU_LIMIT, SWIGLU_LIMIT)
        h_ref[:rows, :] = (gate * _sigmoid(SWIGLU_ALPHA * gate) * (lin + 1.0)).astype(h_ref.dtype)

    _for_valid_rows(used, bv_ref, h_ref, compute)


def _moe1(sched, xs, w_gu, b_gu, tm, tn):
    n_rows, dh = xs.shape
    n_exp, d, de2 = w_gu.shape
    de = de2 // 2
    tn = _pick(de, tn)
    nj = de // tn

    def blk(b, nb):
        return jnp.minimum(b, nb[0] - 1)

    grid_spec = pltpu.PrefetchScalarGridSpec(
        num_scalar_prefetch=5,
        grid=(nj, n_rows // tm),
        in_specs=[pl.BlockSpec((tm, dh), lambda j, b, be, nb, *_: (blk(b, nb), 0)),
                  pl.BlockSpec(memory_space=pl.ANY),
                  pl.BlockSpec((None, 1, tn), lambda j, b, be, nb, *_: (be[blk(b, nb)], 0, j)),
                  pl.BlockSpec((None, 1, tn), lambda j, b, be, nb, *_: (be[blk(b, nb)], 0, nj + j))],
        out_specs=pl.BlockSpec((tm, tn), lambda j, b, *_: (b, j)),
        scratch_shapes=[pltpu.VMEM((d, tn), F32), pltpu.VMEM((d, tn), F32),
                        pltpu.VMEM((d, tn), BF16), pltpu.VMEM((d, tn), BF16),
                        pltpu.SemaphoreType.DMA((2,))],
    )
    return pl.pallas_call(
        functools.partial(_moe1_kernel, tn=tn),
        grid_spec=grid_spec,
        out_shape=jax.ShapeDtypeStruct((n_rows, de), BF16),
        compiler_params=_params("arbitrary", "arbitrary"),
        name="moe_up",
    )(*sched, xs, w_gu, b_gu, b_gu)


def _moe2_kernel(be_ref, nb_ref, nxt_ref, wrap_ref, bv_ref, h_ref, w_hbm, b_ref, y_ref, stage, w_s, sems, *, tn):
    used, fresh = _expert_changed(be_ref, nb_ref)

    def tile_copies(e, jt):
        return (pltpu.make_async_copy(w_hbm.at[e, :, pl.ds(pl.multiple_of(jt * tn, tn), tn)], stage, sems.at[0]),)

    _stream_expert_weights(be_ref, nxt_ref, wrap_ref, used, fresh, tile_copies, (stage,), (w_s,))

    @pl.when(jnp.logical_not(used))
    def _():
        y_ref[...] = jnp.zeros_like(y_ref)

    def compute(rows):
        y_ref[:rows, :] = _dot(h_ref[:rows, :], w_s[...]) + b_ref[...]

    _for_valid_rows(used, bv_ref, y_ref, compute)


def _moe2(sched, h, w_down, b_down, tm, tn):
    n_rows, de = h.shape
    n_exp, _, d = w_down.shape
    tn = _pick(d, tn)

    def blk(b, nb):
        return jnp.minimum(b, nb[0] - 1)

    grid_spec = pltpu.PrefetchScalarGridSpec(
        num_scalar_prefetch=5,
        grid=(d // tn, n_rows // tm),
        in_specs=[pl.BlockSpec((tm, de), lambda j, b, be, nb, *_: (blk(b, nb), 0)),
                  pl.BlockSpec(memory_space=pl.ANY),
                  pl.BlockSpec((None, 1, tn), lambda j, b, be, nb, *_: (be[blk(b, nb)], 0, j))],
        out_specs=pl.BlockSpec((tm, tn), lambda j, b, *_: (b, j)),
        scratch_shapes=[pltpu.VMEM((de, tn), F32), pltpu.VMEM((de, tn), BF16), pltpu.SemaphoreType.DMA((1,))],
    )
    return pl.pallas_call(
        functools.partial(_moe2_kernel, tn=tn),
        grid_spec=grid_spec,
        out_shape=jax.ShapeDtypeStruct((n_rows, d), F32),
        compiler_params=_params("arbitrary", "arbitrary"),
        name="moe_down",
    )(*sched, h, w_down, b_down)


def _combine_kernel(dest_ref, dest_nxt_ref, ys_hbm, x1_ref, wgt_ref, g_ref, b_ref, o_ref, buf, sems, *, tc, top_k,
                    alpha):
    i = pl.program_id(0)
    slot = i % 2

    def row_copy(idx_ref, s, t8, r, j):
        t0 = pl.multiple_of(t8 * SUBLANES, SUBLANES)
        return pltpu.make_async_copy(ys_hbm.at[pl.ds(idx_ref[(t0 + r) * top_k + j], 1)],
                                     buf.at[s, j, pl.ds(t0 + r, 1)], sems.at[s])

    def issue(idx_ref, s):
        def body(t8, c):
            for r in range(SUBLANES):
                for j in range(top_k):
                    row_copy(idx_ref, s, t8, r, j).start(priority=j % 2)
            return c
        lax.fori_loop(0, tc // SUBLANES, body, 0)

    @pl.when(i == 0)
    def _():
        issue(dest_ref, slot)

    @pl.when(i + 1 < pl.num_programs(0))
    def _():
        issue(dest_nxt_ref, 1 - slot)

    def drain(t8, c):
        for r in range(SUBLANES):
            for j in range(top_k):
                row_copy(dest_ref, slot, t8, r, j).wait()
        return c

    lax.fori_loop(0, tc // SUBLANES, drain, 0)
    y = wgt_ref[:, 0:1] * buf[slot, 0]
    for j in range(1, top_k):
        y = y + wgt_ref[:, j:j + 1] * buf[slot, j]
    o_ref[...] = _layer_norm(alpha * x1_ref[...] + y, g_ref[...], b_ref[...])


def _combine(dest_flat, ys, x1, wgt, g, b, alpha, tc):
    t, d = x1.shape
    tc = _pick(t, tc)
    last = t // tc - 1
    return pl.pallas_call(
        functools.partial(_combine_kernel, tc=tc, top_k=TOP_K, alpha=alpha),
        grid=(t // tc,),
        in_specs=[pl.BlockSpec((tc * TOP_K,), lambda i: (i,), memory_space=pltpu.SMEM),
                  pl.BlockSpec((tc * TOP_K,), lambda i: (jnp.minimum(i + 1, last),), memory_space=pltpu.SMEM),
                  pl.BlockSpec(memory_space=pl.ANY),
                  pl.BlockSpec((tc, d), lambda i: (i, 0)),
                  pl.BlockSpec((tc, LANES), lambda i: (i, 0)),
                  pl.BlockSpec((1, d), lambda i: (0, 0)),
                  pl.BlockSpec((1, d), lambda i: (0, 0))],
        out_specs=pl.BlockSpec((tc, d), lambda i: (i, 0)),
        out_shape=jax.ShapeDtypeStruct((t, d), F32),
        scratch_shapes=[pltpu.VMEM((2, TOP_K, tc, d), F32), pltpu.SemaphoreType.DMA((2,))],
        compiler_params=_params("arbitrary"),
        name="combine",
    )(dest_flat, dest_flat, ys, x1, wgt, g, b)


def _layer(x2d, w_in, b_in, a_ws, a_bs, a_ln_g, a_ln_b, gla_w_lr, gla_b_lr, gla_gn_g, w_br_a, w_br_b, w_o,
           ln1_g, ln1_b, w_router, b_router, w_gu, b_gu, w_down, b_down, ln2_g, ln2_b, alpha):
    t, d = x2d.shape
    n_groups, a_chunk, _ = a_ws.shape
    aw = a_ln_g.size
    rank, kw = gla_w_lr.shape
    heads, dv = gla_gn_g.shape
    vw = heads * dv
    n_exp = w_router.shape[1]
    de = w_down.shape[1]
    off_lr = 2 * aw + 2 * kw + 2 * vw
    off_q = 2 * aw

    w_in_t = w_in.T
    w_lr_in = jnp.pad(w_in[:, off_lr:off_lr + rank], ((0, 0), (0, LANES - rank))).astype(BF16)
    b_lr_in = jnp.pad(b_in[off_lr:off_lr + rank], (0, LANES - rank)).reshape(1, LANES)
    wlr = jnp.pad(gla_w_lr, ((0, LANES - rank), (0, 0))).astype(BF16)
    wr = jnp.pad(w_router, ((0, 0), (0, LANES - n_exp))).astype(BF16)
    br = jnp.pad(b_router, (0, LANES - n_exp), constant_values=-jnp.inf).reshape(1, LANES)

    xb, lr = _cast_and_decay_proj(x2d, w_lr_in, b_lr_in, TILE_CAST_ROWS)
    z = _matmul_bias_stream(xb, w_in_t, b_in[:off_lr].reshape(1, -1), 0, F32, *TILE_IN_PROJ)
    zg = _matmul_bias_stream(xb, w_in_t, b_in[off_lr + rank:].reshape(1, -1), off_lr + rank, F32, *TILE_IN_PROJ)

    ya = _branch_a(z, a_ws, a_bs, a_ln_g, a_ln_b, aw, TILE_BRANCH_A_CHUNKS * a_chunk)
    yb = _gla(z, lr, wlr, gla_b_lr.reshape(1, kw), gla_gn_g.reshape(1, vw), kw, vw, off_q, heads,
              TILE_GLA_ROWS)
    merged = _merge(ya, w_br_a.astype(BF16), yb, w_br_b.astype(BF16), zg, 0, d, *TILE_MERGE)
    h1 = _out_proj_resid(merged, w_o.astype(BF16), x2d, alpha, *TILE_OUT_PROJ)

    x1, xp, idx, wgt, rnk, cnt = _ln_router(h1, ln1_g.reshape(1, d), ln1_b.reshape(1, d), wr, br, TILE_ROUTER_ROWS)

    n_assign = t * TOP_K
    tm_e = min(EXPERT_BLOCK_ROWS, max(n_assign // n_exp // 2, 8 * DISPATCH_STRIPES))
    n_blocks = -(-(n_assign + n_exp * (tm_e - 1)) // tm_e)
    n_rows = n_blocks * tm_e
    counts = cnt[0, :n_exp].astype(I32)
    padded = (counts + tm_e - 1) // tm_e * tm_e
    pend = jnp.cumsum(padded)
    pstart = pend - padded
    sel = idx[:, :TOP_K, None] == jnp.arange(n_exp, dtype=I32)[None, None, :]
    dest = jnp.sum(jnp.where(sel, pstart[None, None, :], 0), axis=-1) + rnk[:, :TOP_K]
    dest_flat = dest.reshape(-1).astype(I32)
    nb_used = (pend[-1] // tm_e).astype(I32).reshape(1)
    block_start = jnp.arange(n_blocks, dtype=I32) * tm_e
    block_e = jnp.minimum(jnp.sum((pend[None, :] <= block_start[:, None]).astype(I32), axis=1), n_exp - 1)

    tok_buf = _invert_dest(dest_flat, n_rows, 8192)
    xs = _dispatch(tok_buf, nb_used, xp, tm_e)
    experts = jnp.arange(n_exp, dtype=I32)
    later = jnp.where((padded > 0)[None, :] & (experts[None, :] > block_e[:, None]), experts[None, :], n_exp)
    nxt = jnp.min(later, axis=1)
    wrap = (nxt == n_exp).astype(I32)
    nxt_e = jnp.where(nxt == n_exp, block_e[0], nxt).astype(I32)
    mine = experts[None, :] == block_e[:, None]
    group_end = jnp.sum(jnp.where(mine, (pstart + counts)[None, :], 0), axis=1)
    bvalid = jnp.clip(group_end - block_start, 0, tm_e).astype(I32)
    sched = (block_e, nb_used, nxt_e, wrap, bvalid)
    hmid = _moe1(sched, xs, w_gu, b_gu.reshape(n_exp, 1, 2 * de), tm_e, TILE_EXPERT_UP_COLS)
    ys = _moe2(sched, hmid, w_down, b_down.reshape(n_exp, 1, d), tm_e, TILE_EXPERT_DOWN_COLS)
    return _combine(dest_flat, ys, x1, wgt, ln2_g.reshape(1, d), ln2_b.reshape(1, d), alpha, TILE_COMBINE_ROWS)


def kernel(x, w_in, b_in, a_ws, a_bs, a_ln_g, a_ln_b, gla_w_lr, gla_b_lr, gla_gn_g, w_br_a, w_br_b, w_o, ln1_g, ln1_b, w_router, b_router, w_gu, b_gu, w_down, b_down, ln2_g, ln2_b):
    bsz, seq, d = x.shape
    depth = w_in.shape[0]
    alpha = (2 * depth) ** 0.25
    outs = []
    for bi in range(bsz):
        h = x[bi]
        for l in range(depth):
            h = _layer(h, w_in[l], b_in[l], a_ws[l], a_bs[l], a_ln_g[l], a_ln_b[l], gla_w_lr[l], gla_b_lr[l],
                       gla_gn_g[l], w_br_a[l], w_br_b[l], w_o[l], ln1_g[l], ln1_b[l], w_router[l], b_router[l],
                       w_gu[l], b_gu[l], w_down[l], b_down[l], ln2_g[l], ln2_b[l], alpha)
        outs.append(h)
    return jnp.stack(outs) if bsz > 1 else outs[0][None]
```
